```python
import jax
import jax.numpy as jnp
from jax import lax
import numpy as np

D_MODEL = 1024
BATCH = 8
SEQ = 4096
DEPTH = 4

RET_HEADS = 4
RET_HEAD_DIM = D_MODEL // 2 // RET_HEADS
RET_VALUE_DIM = D_MODEL // 2 // RET_HEADS
RET_CHUNK = 128
RET_WIDTH = RET_HEADS * RET_VALUE_DIM
WIN_Q_HEADS = 8
WIN_KV_HEADS = 2
WIN_HEAD_DIM = D_MODEL // 2 // WIN_Q_HEADS
WINDOW = 128
WIN_BLOCK = WINDOW
WIN_WIDTH = WIN_Q_HEADS * WIN_HEAD_DIM
FOURIER_GROUPS = 4
ROPE_THETA = 10000.0
N_GROUPS = 4
EXPERTS_PER_GROUP = 8
N_EXPERTS = N_GROUPS * EXPERTS_PER_GROUP
TOP_K = 2
EXPERT_HIDDEN = D_MODEL // 2
LN_EPS = 1e-5
GN_EPS = 1e-6
DEEPNORM_ALPHA = (2.0 * DEPTH) ** 0.25
DEEPNORM_BETA = (8.0 * DEPTH) ** -0.25
N_EVEN = (DEPTH + 1) // 2
N_ODD = DEPTH // 2
RET_QK = RET_HEADS * RET_HEAD_DIM
WIN_KV = WIN_KV_HEADS * WIN_HEAD_DIM
EVEN_SPLITS = tuple(np.cumsum([RET_QK, RET_QK, RET_WIDTH, RET_WIDTH, WIN_WIDTH, WIN_KV])[:].tolist())
IN_EVEN = 2 * RET_QK + 2 * RET_WIDTH + WIN_WIDTH + 2 * WIN_KV

kernel_name = "hybrid_retention_swa_fnet_hmoe_encoder"


def layer_norm(x, gain, bias):
    xf = x.astype(jnp.float32)
    mu = jnp.mean(xf, axis=-1, keepdims=True)
    var = jnp.mean(jnp.square(xf - mu), axis=-1, keepdims=True)
    y = (xf - mu) * lax.rsqrt(var + LN_EPS) * gain.astype(jnp.float32) + bias.astype(jnp.float32)
    return y.astype(x.dtype)


def rope(x, pos):
    d = x.shape[-1]
    half = d // 2
    inv = ROPE_THETA ** (-jnp.arange(half, dtype=jnp.float32) / half)
    ang = pos.astype(jnp.float32)[:, None] * inv[None, :]
    cos = jnp.cos(ang)[None, :, None, :]
    sin = jnp.sin(ang)[None, :, None, :]
    xf = x.astype(jnp.float32)
    x1, x2 = xf[..., :half], xf[..., half:]
    return jnp.concatenate([x1 * cos - x2 * sin, x1 * sin + x2 * cos], axis=-1).astype(x.dtype)


def retention_one_direction(q, k, v, log_gamma, include_diag):
    B, H, S, dk = q.shape
    dv = v.shape[-1]
    C = RET_CHUNK
    NC = S // C
    qc = q.reshape(B, H, NC, C, dk)
    kc = k.reshape(B, H, NC, C, dk)
    vc = v.reshape(B, H, NC, C, dv)
    pos = jnp.arange(C, dtype=jnp.float32)
    lg = log_gamma[:, None, None]
    diff = pos[:, None] - pos[None, :]
    mask = diff >= 0 if include_diag else diff > 0
    dmat = jnp.where(mask[None], jnp.exp(lg * jnp.maximum(diff, 0.0)[None]), 0.0)
    scores = jnp.einsum('bhncd,bhnmd->bhncm', qc, kc) * dmat[None, :, None]
    y_inner = jnp.einsum('bhncm,bhnmv->bhncv', scores, vc)
    zeta = jnp.exp(log_gamma[:, None] * (C - 1.0 - pos)[None])
    kv = jnp.einsum('bhnmd,bhnmv->nbhdv', kc * zeta[None, :, None, :, None], vc)
    chunk_decay = jnp.exp(log_gamma * C)[None, :, None, None]

    def step(state, kv_n):
        return state * chunk_decay + kv_n, state

    _, r_prev = lax.scan(step, jnp.zeros((B, H, dk, dv), jnp.float32), kv)
    xi = jnp.exp(log_gamma[:, None] * (pos + 1.0)[None])
    y_cross = jnp.einsum('bhncd,nbhdv->bhncv', qc * xi[None, :, None, :, None], r_prev)
    return (y_inner + y_cross).reshape(B, H, S, dv)


def bidirectional_retention(q, k, v, decay_logit):
    qf = jnp.transpose(q.astype(jnp.float32), (0, 2, 1, 3))
    kf = jnp.transpose(k.astype(jnp.float32), (0, 2, 1, 3))
    vf = jnp.transpose(v.astype(jnp.float32), (0, 2, 1, 3))
    log_gamma = jax.nn.log_sigmoid(decay_logit.astype(jnp.float32))
    fwd = retention_one_direction(qf, kf, vf, log_gamma[0], True)
    bwd = jnp.flip(retention_one_direction(jnp.flip(qf, 2), jnp.flip(kf, 2), jnp.flip(vf, 2),
                                           log_gamma[1], False), 2)
    return jnp.transpose(fwd + bwd, (0, 2, 1, 3))


def banded_keys(t):
    B, S, Hkv, d = t.shape
    NB = S // WIN_BLOCK
    tp = jnp.pad(t, ((0, 0), (WINDOW, WINDOW), (0, 0), (0, 0))).reshape(B, NB + 2, WIN_BLOCK, Hkv, d)
    return jnp.concatenate([tp[:, :-2], tp[:, 1:-1], tp[:, 2:]], axis=2)


def window_attention_with_sink(q, k, v, sink_logit):
    B, S, Hq, d = q.shape
    Hkv = k.shape[2]
    G = Hq // Hkv
    NB = S // WIN_BLOCK
    qb = q.reshape(B, NB, WIN_BLOCK, Hkv, G, d)
    kb = banded_keys(k)
    vb = banded_keys(v)
    s = jnp.einsum('bnqhgd,bnkhd->bnhgqk', qb, kb).astype(jnp.float32)
    i = jnp.arange(WIN_BLOCK)[:, None]
    j = jnp.arange(3 * WIN_BLOCK)[None, :]
    rel_ok = jnp.abs(i - j + WINDOW) <= WINDOW
    kpos = jnp.arange(NB)[:, None] * WIN_BLOCK + jnp.arange(3 * WIN_BLOCK)[None, :] - WINDOW
    in_ok = (kpos >= 0) & (kpos < S)
    mask = rel_ok[None] & in_ok[:, None, :]
    s = jnp.where(mask[None, :, None, None], s, -1e30)
    sink = jnp.broadcast_to(sink_logit.astype(jnp.float32).reshape(1, 1, Hkv, G, 1, 1), s.shape[:-1] + (1,))
    p = jax.nn.softmax(jnp.concatenate([s, sink], axis=-1), axis=-1)[..., :-1]
    o = jnp.einsum('bnhgqk,bnkhd->bnqhgd', p.astype(v.dtype), vb)
    return o.reshape(B, S, Hq * d)


def even_mixer(x, w_in, decay_logit, gn_gain, sink_logit, w_out, pos):
    B, S, _ = x.shape
    h = x @ w_in
    qa, ka, va, ga, qb, kb, vb = jnp.split(h, EVEN_SPLITS, axis=-1)
    qa = rope(qa.reshape(B, S, RET_HEADS, RET_HEAD_DIM), pos)
    ka = rope(ka.reshape(B, S, RET_HEADS, RET_HEAD_DIM), pos) * (RET_HEAD_DIM ** -0.5)
    va = va.reshape(B, S, RET_HEADS, RET_VALUE_DIM)
    ya = bidirectional_retention(qa, ka, va, decay_logit)
    mu = jnp.mean(ya, axis=-1, keepdims=True)
    var = jnp.mean(jnp.square(ya - mu), axis=-1, keepdims=True)
    ya = ((ya - mu) * lax.rsqrt(var + GN_EPS)).reshape(B, S, RET_WIDTH) * gn_gain.astype(jnp.float32)
    ya = jax.nn.silu(ga) * ya.astype(x.dtype)
    qb = rope(qb.reshape(B, S, WIN_Q_HEADS, WIN_HEAD_DIM), pos) * (WIN_HEAD_DIM ** -0.5)
    kb = rope(kb.reshape(B, S, WIN_KV_HEADS, WIN_HEAD_DIM), pos)
    vb = vb.reshape(B, S, WIN_KV_HEADS, WIN_HEAD_DIM)
    yb = window_attention_with_sink(qb, kb, vb, sink_logit)
    return jnp.concatenate([ya, yb], axis=-1) @ w_out


def fourier_mixer(x, w_out):
    B, S, D = x.shape
    xg = x.astype(jnp.float32).reshape(B, S, FOURIER_GROUPS, D // FOURIER_GROUPS)
    y = jnp.fft.fft2(xg, axes=(1, 3), norm='ortho').real.reshape(B, S, D)
    return y.astype(x.dtype) @ w_out


def hierarchical_moe(x, w_coarse, b_coarse, w_fine, b_fine, w_gate, w_up, w_down):
    B, S, D = x.shape
    xt = x.reshape(B * S, D)
    T = B * S
    coarse = (xt @ w_coarse + b_coarse).astype(jnp.float32)
    p_group, g_idx = lax.top_k(jax.nn.softmax(coarse, axis=-1), 1)
    fine = (xt @ w_fine + b_fine).astype(jnp.float32).reshape(T, N_GROUPS, EXPERTS_PER_GROUP)
    fine_g = jnp.take_along_axis(fine, g_idx[:, :, None], axis=1)[:, 0]
    top_val, top_idx = lax.top_k(fine_g, TOP_K)
    gate = jax.nn.softmax(top_val, axis=-1) * p_group
    expert_id = (g_idx * EXPERTS_PER_GROUP + top_idx).reshape(-1)
    order = jnp.argsort(expert_id)
    tok = order // TOP_K
    sizes = jnp.bincount(expert_id, length=N_EXPERTS).astype(jnp.int32)
    xs = xt[tok]
    hid = jax.nn.silu(lax.ragged_dot(xs, w_gate, sizes)) * lax.ragged_dot(xs, w_up, sizes)
    ys = lax.ragged_dot(hid, w_down, sizes) * gate.reshape(-1)[order][:, None].astype(x.dtype)
    out = jax.ops.segment_sum(ys, tok, num_segments=T)
    return out.reshape(B, S, D)


def setup_inputs(seed: int = 0) -> dict:
    key = jax.random.key(seed)
    ks = jax.random.split(key, 20)
    f32 = jnp.float32
    nrm = lambda k, shape: jax.random.normal(k, shape, f32)
    base_logit = jnp.log(2.0 ** (5.0 + jnp.arange(RET_HEADS, dtype=f32)) - 1.0)
    return {
        "x": nrm(ks[0], (BATCH, SEQ, D_MODEL)),
        "w_in_even": nrm(ks[1], (N_EVEN, D_MODEL, IN_EVEN)) * D_MODEL ** -0.5,
        "ret_decay_logit": base_logit[None, None, :] + 0.1 * nrm(ks[2], (N_EVEN, 2, RET_HEADS)),
        "ret_gn_gain": 1.0 + 0.02 * nrm(ks[3], (N_EVEN, RET_WIDTH)),
        "sink_logit": 0.5 * nrm(ks[4], (N_EVEN, WIN_Q_HEADS)),
        "w_out_even": nrm(ks[5], (N_EVEN, D_MODEL, D_MODEL)) * D_MODEL ** -0.5 * DEEPNORM_BETA,
        "w_out_fourier": nrm(ks[6], (N_ODD, D_MODEL, D_MODEL)) * D_MODEL ** -0.5 * DEEPNORM_BETA,
        "ln1_gain": 1.0 + 0.02 * nrm(ks[7], (DEPTH, D_MODEL)),
        "ln1_bias": 0.02 * nrm(ks[8], (DEPTH, D_MODEL)),
        "ln2_gain": 1.0 + 0.02 * nrm(ks[9], (DEPTH, D_MODEL)),
        "ln2_bias": 0.02 * nrm(ks[10], (DEPTH, D_MODEL)),
        "router_coarse_w": nrm(ks[11], (DEPTH, D_MODEL, N_GROUPS)) * D_MODEL ** -0.5,
        "router_coarse_b": 0.01 * nrm(ks[12], (DEPTH, N_GROUPS)),
        "router_fine_w": nrm(ks[13], (DEPTH, D_MODEL, N_EXPERTS)) * D_MODEL ** -0.5,
        "router_fine_b": 0.01 * nrm(ks[14], (DEPTH, N_EXPERTS)),
        "expert_w_gate": nrm(ks[15], (DEPTH, N_EXPERTS, D_MODEL, EXPERT_HIDDEN)) * D_MODEL ** -0.5,
        "expert_w_up": nrm(ks[16], (DEPTH, N_EXPERTS, D_MODEL, EXPERT_HIDDEN)) * D_MODEL ** -0.5,
        "expert_w_down": nrm(ks[17], (DEPTH, N_EXPERTS, EXPERT_HIDDEN, D_MODEL)) * EXPERT_HIDDEN ** -0.5 * DEEPNORM_BETA,
    }


def reference(x, w_in_even, ret_decay_logit, ret_gn_gain, sink_logit, w_out_even, w_out_fourier,
              ln1_gain, ln1_bias, ln2_gain, ln2_bias, router_coarse_w, router_coarse_b,
              router_fine_w, router_fine_b, expert_w_gate, expert_w_up, expert_w_down):
    pos = jnp.arange(x.shape[1], dtype=jnp.int32)
    for layer in range(DEPTH):
        if layer % 2 == 0:
            e = layer // 2
            mix = even_mixer(x, w_in_even[e], ret_decay_logit[e], ret_gn_gain[e], sink_logit[e],
                             w_out_even[e], pos)
        else:
            mix = fourier_mixer(x, w_out_fourier[layer // 2])
        x = layer_norm(DEEPNORM_ALPHA * x + mix, ln1_gain[layer], ln1_bias[layer])
        ffn = hierarchical_moe(x, router_coarse_w[layer], router_coarse_b[layer], router_fine_w[layer],
                               router_fine_b[layer], expert_w_gate[layer], expert_w_up[layer],
                               expert_w_down[layer])
        x = layer_norm(DEEPNORM_ALPHA * x + ffn, ln2_gain[layer], ln2_bias[layer])
    return x
```

```python
import functools
import math

import jax
import jax.numpy as jnp
from jax import lax
from jax.experimental import pallas as pl
from jax.experimental.pallas import tpu as pltpu

F32 = jnp.float32
BF16 = jnp.bfloat16
I32 = jnp.int32

D_MODEL = 1024
DEPTH = 4
RET_HEADS = 4
RET_DIM = 128
RET_CHUNK = 128
RET_WIDTH = RET_HEADS * RET_DIM
WIN_Q_HEADS = 8
WIN_KV_HEADS = 2
WIN_DIM = 64
WINDOW = 128
WIN_WIDTH = WIN_Q_HEADS * WIN_DIM
WIN_KV = WIN_KV_HEADS * WIN_DIM
FOURIER_GROUPS = 4
GROUP_CH = D_MODEL // FOURIER_GROUPS
FFT_RADIX2 = 16
ROPE_THETA = 10000.0
N_GROUPS = 4
EXPERTS_PER_GROUP = 8
N_EXPERTS = N_GROUPS * EXPERTS_PER_GROUP
EXPERT_HIDDEN = D_MODEL // 2
LN_EPS = 1e-5
GN_EPS = 1e-6
ALPHA = (2.0 * DEPTH) ** 0.25
IN_EVEN = 2 * RET_WIDTH + 2 * RET_WIDTH + WIN_WIDTH + 2 * WIN_KV

LANES = 128
VMEM_LIMIT_BYTES = 48 * 1024 * 1024
NEG_MASK = -1e30

_NT = (((1,), (1,)), ((), ()))


def _params(*sem):
    return pltpu.CompilerParams(dimension_semantics=sem, vmem_limit_bytes=VMEM_LIMIT_BYTES)


def _silu(v):
    return v / (1.0 + jnp.exp(-v))


def _layer_norm(z, gain, bias):
    mu = jnp.mean(z, axis=-1, keepdims=True)
    zc = z - mu
    var = jnp.mean(zc * zc, axis=-1, keepdims=True)
    return zc * lax.rsqrt(var + LN_EPS) * gain + bias


def _rope128(h, cos, sin_signed):
    return h * cos + pltpu.roll(h, 64, 1) * sin_signed


def _rope64x2(h, cos, sin_lo, sin_hi):
    return h * cos + pltpu.roll(h, 96, 1) * sin_lo + pltpu.roll(h, 32, 1) * sin_hi


def _inproj_kernel(x_ref, w_ref, cr_ref, sr_ref, cw_ref, slo_ref, shi_ref,
                   qa_ref, ka_ref, va_ref, ga_ref, qb_ref, kb_ref, vb_ref):
    xb = x_ref[...].astype(BF16)

    def seg(lo, hi):
        return jnp.dot(xb, w_ref[:, lo:hi], preferred_element_type=F32)

    cr, sr = cr_ref[...], sr_ref[...]
    q = seg(0, RET_WIDTH)
    k = seg(RET_WIDTH, 2 * RET_WIDTH)
    for h in range(RET_HEADS):
        sl = slice(LANES * h, LANES * (h + 1))
        qa_ref[:, sl] = _rope128(q[:, sl], cr, sr).astype(BF16)
        ka_ref[:, sl] = (_rope128(k[:, sl], cr, sr) * RET_DIM ** -0.5).astype(BF16)
    va_ref[...] = seg(2 * RET_WIDTH, 3 * RET_WIDTH).astype(BF16)
    ga_ref[...] = _silu(seg(3 * RET_WIDTH, 4 * RET_WIDTH)).astype(BF16)
    cw, slo, shi = cw_ref[...], slo_ref[...], shi_ref[...]
    base = 4 * RET_WIDTH
    q = seg(base, base + WIN_WIDTH)
    for p in range(WIN_WIDTH // LANES):
        sl = slice(LANES * p, LANES * (p + 1))
        qb_ref[:, sl] = (_rope64x2(q[:, sl], cw, slo, shi) * WIN_DIM ** -0.5).astype(BF16)
    kv = seg(base + WIN_WIDTH, base + WIN_WIDTH + 2 * WIN_KV)
    kb_ref[...] = _rope64x2(kv[:, :WIN_KV], cw, slo, shi).astype(BF16)
    vb_ref[...] = kv[:, WIN_KV:].astype(BF16)


def _rope_tables(seq):
    pos = jnp.arange(seq, dtype=F32)[:, None]
    half = RET_DIM // 2
    inv = ROPE_THETA ** (-jnp.arange(half, dtype=F32) / half)
    ang = pos * inv[None, :]
    cr = jnp.concatenate([jnp.cos(ang), jnp.cos(ang)], axis=1)
    sr = jnp.concatenate([-jnp.sin(ang), jnp.sin(ang)], axis=1)
    half = WIN_DIM // 2
    inv = ROPE_THETA ** (-jnp.arange(half, dtype=F32) / half)
    ang = pos * inv[None, :]
    c, s, z = jnp.cos(ang), jnp.sin(ang), jnp.zeros_like(ang)
    cw = jnp.concatenate([c, c, c, c], axis=1)
    slo = jnp.concatenate([-s, z, -s, z], axis=1)
    shi = jnp.concatenate([z, s, z, s], axis=1)
    return cr, sr, cw, slo, shi


def _inproj(x2, w_bf16, tables, seq, tm):
    T = x2.shape[0]
    nseq = seq // tm
    row = lambda i: (i, 0)
    tab = lambda i: (i % nseq, 0)
    widths = (RET_WIDTH, RET_WIDTH, RET_WIDTH, RET_WIDTH, WIN_WIDTH, WIN_KV, WIN_KV)
    return pl.pallas_call(
        _inproj_kernel,
        grid=(T // tm,),
        in_specs=[pl.BlockSpec((tm, D_MODEL), row),
                  pl.BlockSpec((D_MODEL, IN_EVEN), lambda i: (0, 0))]
                 + [pl.BlockSpec((tm, LANES), tab)] * 5,
        out_specs=[pl.BlockSpec((tm, w), row) for w in widths],
        out_shape=[jax.ShapeDtypeStruct((T, w), BF16) for w in widths],
        compiler_params=_params("arbitrary"),
        name="inproj_rope",
    )(x2, w_bf16, *tables)


def _retention_kernel(logit_ref, q_ref, k_ref, v_ref, g_ref, gain_ref, o_ref, y_scr):
    h = pl.program_id(1)
    C = RET_CHUNK
    n_chunks = q_ref.shape[1] // C

    def log_gamma(d):
        v = jnp.full((1, 1), logit_ref[d, h], F32)
        return -(jnp.maximum(-v, 0.0) + jnp.log(1.0 + jnp.exp(-jnp.abs(v))))

    lgf, lgb = log_gamma(0), log_gamma(1)
    i = lax.broadcasted_iota(I32, (C, C), 0)
    j = lax.broadcasted_iota(I32, (C, C), 1)
    diff = (i - j).astype(F32)
    dmat = jnp.where(diff >= 0, jnp.exp(lgf * jnp.maximum(diff, 0.0)),
                     jnp.exp(lgb * jnp.maximum(-diff, 0.0)))
    c = lax.broadcasted_iota(I32, (C, 1), 0).astype(F32)
    xi_f, zeta_f, dec_f = jnp.exp(lgf * (c + 1.0)), jnp.exp(lgf * (C - 1.0 - c)), jnp.exp(lgf * C)
    xi_b, zeta_b, dec_b = jnp.exp(lgb * (C - c)), jnp.exp(lgb * c), jnp.exp(lgb * C)
    gain = gain_ref[...]

    def chunk(n):
        rows = pl.ds(pl.multiple_of(n * C, C), C)
        return rows, q_ref[0, rows, :], k_ref[0, rows, :], v_ref[0, rows, :]

    def state_update(state, k, v, zeta, dec):
        kz = (k.astype(F32) * zeta).T.astype(BF16)
        return state * dec + jnp.dot(kz, v, preferred_element_type=F32)

    def cross(q, xi, state):
        qx = (q.astype(F32) * xi).astype(BF16)
        return jnp.dot(qx, state.astype(BF16), preferred_element_type=F32)

    def fwd(n, state):
        rows, q, k, v = chunk(n)
        s = lax.dot_general(q, k, _NT, preferred_element_type=F32)
        y = jnp.dot((s * dmat).astype(BF16), v, preferred_element_type=F32)
        y_scr[rows, :] = y + cross(q, xi_f, state)
        return state_update(state, k, v, zeta_f, dec_f)

    def bwd(t, state):
        rows, q, k, v = chunk(n_chunks - 1 - t)
        y = y_scr[rows, :] + cross(q, xi_b, state)
        mu = jnp.mean(y, axis=-1, keepdims=True)
        yc = y - mu
        var = jnp.mean(yc * yc, axis=-1, keepdims=True)
        yn = yc * lax.rsqrt(var + GN_EPS) * gain
        o_ref[0, rows, :] = (g_ref[0, rows, :].astype(F32) * yn).astype(BF16)
        return state_update(state, k, v, zeta_b, dec_b)

    zero = jnp.zeros((C, C), F32)
    lax.fori_loop(0, n_chunks, fwd, zero)
    lax.fori_loop(0, n_chunks, bwd, zero)


def _retention(qa, ka, va, ga, decay_logit, gn_gain):
    B, S, _ = qa.shape
    head = pl.BlockSpec((1, S, RET_DIM), lambda b, h: (b, 0, h))
    return pl.pallas_call(
        _retention_kernel,
        grid=(B, RET_HEADS),
        in_specs=[pl.BlockSpec(memory_space=pltpu.SMEM), head, head, head, head,
                  pl.BlockSpec((1, RET_DIM), lambda b, h: (0, h))],
        out_specs=head,
        out_shape=jax.ShapeDtypeStruct((B, S, RET_WIDTH), BF16),
        scratch_shapes=[pltpu.VMEM((S, RET_DIM), F32)],
        compiler_params=_params("arbitrary", "arbitrary"),
        name="retention_gn_gate",
    )(decay_logit.astype(F32), qa, ka, va, ga, gn_gain.reshape(1, RET_WIDTH).astype(F32))


def _winattn_kernel(sink_ref, q_ref, kp_ref, kc_ref, kn_ref, vp_ref, vc_ref, vn_ref, o_ref):
    n = pl.program_id(1)
    nb = pl.num_programs(1)
    W = WINDOW
    i = lax.broadcasted_iota(I32, (W, W), 0)
    c = lax.broadcasted_iota(I32, (W, W), 1)
    off = 4 * W
    mask_p = c >= i + jnp.where(n > 0, 0, off)
    mask_n = c <= i - jnp.where(n < nb - 1, 0, off)
    group = WIN_Q_HEADS // WIN_KV_HEADS
    for h in range(WIN_Q_HEADS):
        ks = slice(WIN_DIM * (h // group), WIN_DIM * (h // group + 1))
        hs = slice(WIN_DIM * h, WIN_DIM * (h + 1))
        qh = q_ref[0, :, hs]
        s0 = jnp.where(mask_p, lax.dot_general(qh, kp_ref[0, :, ks], _NT, preferred_element_type=F32), NEG_MASK)
        s1 = lax.dot_general(qh, kc_ref[0, :, ks], _NT, preferred_element_type=F32)
        s2 = jnp.where(mask_n, lax.dot_general(qh, kn_ref[0, :, ks], _NT, preferred_element_type=F32), NEG_MASK)
        sink = sink_ref[0, h]
        m = jnp.maximum(jnp.maximum(jnp.max(s0, axis=1, keepdims=True), jnp.max(s1, axis=1, keepdims=True)),
                        jnp.maximum(jnp.max(s2, axis=1, keepdims=True), sink))
        e0, e1, e2 = jnp.exp(s0 - m), jnp.exp(s1 - m), jnp.exp(s2 - m)
        den = (jnp.sum(e0, axis=1, keepdims=True) + jnp.sum(e1, axis=1, keepdims=True)
               + jnp.sum(e2, axis=1, keepdims=True) + jnp.exp(sink - m))
        o = (jnp.dot(e0.astype(BF16), vp_ref[0, :, ks], preferred_element_type=F32)
             + jnp.dot(e1.astype(BF16), vc_ref[0, :, ks], preferred_element_type=F32)
             + jnp.dot(e2.astype(BF16), vn_ref[0, :, ks], preferred_element_type=F32))
        o_ref[0, :, hs] = (o / den).astype(BF16)


def _winattn(qb, kb, vb, sink_logit):
    B, S, _ = qb.shape
    nb = S // WINDOW
    prev = pl.BlockSpec((1, WINDOW, WIN_KV), lambda b, n: (b, jnp.maximum(n - 1, 0), 0))
    cur = pl.BlockSpec((1, WINDOW, WIN_KV), lambda b, n: (b, n, 0))
    nxt = pl.BlockSpec((1, WINDOW, WIN_KV), lambda b, n: (b, jnp.minimum(n + 1, nb - 1), 0))
    qspec = pl.BlockSpec((1, WINDOW, WIN_WIDTH), lambda b, n: (b, n, 0))
    return pl.pallas_call(
        _winattn_kernel,
        grid=(B, nb),
        in_specs=[pl.BlockSpec(memory_space=pltpu.SMEM), qspec, prev, cur, nxt, prev, cur, nxt],
        out_specs=qspec,
        out_shape=jax.ShapeDtypeStruct((B, S, WIN_WIDTH), BF16),
        compiler_params=_params("arbitrary", "arbitrary"),
        name="window_attention",
    )(sink_logit.reshape(1, WIN_Q_HEADS).astype(F32), qb, kb, kb, kb, vb, vb, vb)


def _chan_dft_kernel(x_ref, m_ref, wr_ref, wi_ref):
    for s in range(x_ref.shape[2] // D_MODEL):
        for g in range(FOURIER_GROUPS):
            lo = s * D_MODEL + g * GROUP_CH
            w = jnp.dot(x_ref[0, :, lo:lo + GROUP_CH].astype(BF16), m_ref[...], preferred_element_type=F32)
            gs = slice(g * GROUP_CH, (g + 1) * GROUP_CH)
            wr_ref[0, s, :, gs] = w[:, :GROUP_CH].astype(BF16)
            wi_ref[0, s, :, gs] = w[:, GROUP_CH:].astype(BF16)


def _real_dft16(ar, ai):
    n = FFT_RADIX2
    cs = [math.cos(2 * math.pi * m / n) for m in range(n)]
    sn = [math.sin(2 * math.pi * m / n) for m in range(n)]

    def axpy(acc, coef, v):
        if abs(coef) < 1e-12:
            return acc
        if abs(abs(coef) - 1.0) < 1e-12:
            if acc is None:
                return v if coef > 0 else -v
            return acc + v if coef > 0 else acc - v
        t = coef * v
        return t if acc is None else acc + t

    p = {s: ar[s] + ar[n - s] for s in range(1, n // 2)}
    d = {s: ai[s] - ai[n - s] for s in range(1, n // 2)}
    base = (ar[0] + ar[n // 2], ar[0] - ar[n // 2])
    y = [None] * n
    for k in range(n // 2 + 1):
        e = base[k % 2]
        for s in range(1, n // 2):
            e = axpy(e, cs[(s * k) % n], p[s])
        if k in (0, n // 2):
            y[k] = e
            continue
        o = None
        for s in range(1, n // 2):
            o = axpy(o, sn[(s * k) % n], d[s])
        y[k] = e + o
        y[n - k] = e - o
    return y


def _seq_dft_kernel(wr_ref, wi_ref, cf_ref, sf_ref, y_ref, ar_scr, ai_scr):
    n1 = wr_ref.shape[2]
    for s in range(FFT_RADIX2):
        cf, sf, wr, wi = cf_ref[s], sf_ref[s], wr_ref[0, s], wi_ref[0, s]
        ar_scr[s] = (jnp.dot(cf, wr, preferred_element_type=F32)
                     + jnp.dot(sf, wi, preferred_element_type=F32))
        ai_scr[s] = (jnp.dot(cf, wi, preferred_element_type=F32)
                     - jnp.dot(sf, wr, preferred_element_type=F32))

    def rows8(r, carry):
        rows = pl.ds(pl.multiple_of(r * 8, 8), 8)
        y = _real_dft16([ar_scr[s, rows, :] for s in range(FFT_RADIX2)],
                        [ai_scr[s, rows, :] for s in range(FFT_RADIX2)])
        for k in range(FFT_RADIX2):
            y_ref[0, k, rows, :] = y[k].astype(BF16)
        return carry

    lax.fori_loop(0, n1 // 8, rows8, 0)


def _fourier_tables(seq):
    n1 = seq // FFT_RADIX2
    c = jnp.arange(GROUP_CH, dtype=I32)
    m = (c[:, None] * c[None, :]) % GROUP_CH
    ang = m.astype(F32) * (2.0 * math.pi / GROUP_CH)
    scale = (seq * GROUP_CH) ** -0.5
    chan = (jnp.concatenate([jnp.cos(ang), -jnp.sin(ang)], axis=1) * scale).astype(BF16)
    k1 = jnp.arange(n1, dtype=I32)[None, :, None]
    s1 = jnp.arange(n1, dtype=I32)[None, None, :]
    s2 = jnp.arange(FFT_RADIX2, dtype=I32)[:, None, None]
    m = (FFT_RADIX2 * s1 * k1 + s2 * k1) % seq
    ang = m.astype(F32) * (2.0 * math.pi / seq)
    return chan, jnp.cos(ang).astype(BF16), jnp.sin(ang).astype(BF16)


def _fourier(x, tables, s2_block=4, tn=256):
    B, S, _ = x.shape
    n1 = S // FFT_RADIX2
    chan, cf, sf = tables
    wshape = jax.ShapeDtypeStruct((B, FFT_RADIX2, n1, D_MODEL), BF16)
    wblk = pl.BlockSpec((1, s2_block, n1, D_MODEL), lambda b, j: (b, j, 0, 0))
    wr, wi = pl.pallas_call(
        _chan_dft_kernel,
        grid=(B, FFT_RADIX2 // s2_block),
        in_specs=[pl.BlockSpec((1, n1, s2_block * D_MODEL), lambda b, j: (b, 0, j)),
                  pl.BlockSpec((GROUP_CH, 2 * GROUP_CH), lambda b, j: (0, 0))],
        out_specs=[wblk, wblk],
        out_shape=[wshape, wshape],
        compiler_params=_params("arbitrary", "arbitrary"),
        name="fourier_channel_dft",
    )(x.reshape(B, n1, FFT_RADIX2 * D_MODEL), chan)
    cols = pl.BlockSpec((1, FFT_RADIX2, n1, tn), lambda b, j: (b, 0, 0, j))
    full = pl.BlockSpec((FFT_RADIX2, n1, n1), lambda b, j: (0, 0, 0))
    y = pl.pallas_call(
        _seq_dft_kernel,
        grid=(B, D_MODEL // tn),
        in_specs=[cols, cols, full, full],
        out_specs=cols,
        out_shape=wshape,
        scratch_shapes=[pltpu.VMEM((FFT_RADIX2, n1, tn), F32), pltpu.VMEM((FFT_RADIX2, n1, tn), F32)],
        compiler_params=_params("arbitrary", "arbitrary"),
        name="fourier_sequence_dft",
    )(wr, wi, cf, sf)
    return y.reshape(B * S, D_MODEL)


def _proj_ln_route_kernel(a_ref, b_ref, x_ref, w_ref, gain_ref, bias_ref, wr_ref, br_ref,
                          xo_ref, ri_ref, rg_ref, cnt_ref, carry_scr):
    tm = x_ref.shape[0]
    half = a_ref.shape[1]

    @pl.when(pl.program_id(0) == 0)
    def _():
        carry_scr[...] = jnp.zeros_like(carry_scr)

    mix = (jnp.dot(a_ref[...], w_ref[:half, :], preferred_element_type=F32)
           + jnp.dot(b_ref[...], w_ref[half:, :], preferred_element_type=F32))
    y = _layer_norm(ALPHA * x_ref[...] + mix, gain_ref[...], bias_ref[...])
    xo_ref[...] = y

    logits = jnp.dot(y.astype(BF16), wr_ref[...], preferred_element_type=F32) + br_ref[...]
    lane = lax.broadcasted_iota(I32, (tm, LANES), 1)
    lanef = lane.astype(F32)
    ninf = -jnp.inf
    big = 1e9

    def rmax(v):
        return jnp.max(v, axis=1, keepdims=True)

    def first_lane(hit):
        return jnp.min(jnp.where(hit, lanef, big), axis=1, keepdims=True)

    cmask = (lane >= N_EXPERTS) & (lane < N_EXPERTS + N_GROUPS)
    cl = jnp.where(cmask, logits, ninf)
    cmax = rmax(cl)
    group = first_lane(cl == cmax) - float(N_EXPERTS)
    p_group = 1.0 / jnp.sum(jnp.where(cmask, jnp.exp(cl - cmax), 0.0), axis=1, keepdims=True)
    lo = group * float(EXPERTS_PER_GROUP)
    fmask = (lanef >= lo) & (lanef < lo + float(EXPERTS_PER_GROUP))
    fl = jnp.where(fmask, logits, ninf)
    v1 = rmax(fl)
    e1 = first_lane(fl == v1)
    fl2 = jnp.where(lanef == e1, ninf, fl)
    v2 = rmax(fl2)
    e2 = first_lane(fl2 == v2)
    t = jnp.exp(v2 - v1)
    g1 = p_group / (1.0 + t)
    g2 = p_group * t / (1.0 + t)

    hit1, hit2 = lanef == e1, lanef == e2
    onehot = jnp.where(hit1 | hit2, 1.0, 0.0)
    r = lax.broadcasted_iota(I32, (tm, tm), 0)
    cidx = lax.broadcasted_iota(I32, (tm, tm), 1)
    lower = jnp.where(r > cidx, 1.0, 0.0).astype(BF16)
    prefix = jnp.dot(lower, onehot.astype(BF16), preferred_element_type=F32) + carry_scr[...]
    rank1 = jnp.sum(jnp.where(hit1, prefix, 0.0), axis=1, keepdims=True)
    rank2 = jnp.sum(jnp.where(hit2, prefix, 0.0), axis=1, keepdims=True)
    carry_scr[...] += jnp.sum(onehot, axis=0, keepdims=True)
    cnt_ref[...] = carry_scr[...]
    ri_ref[...] = jnp.where(lane == 0, e1, jnp.where(lane == 1, e2,
                            jnp.where(lane == 2, rank1, rank2))).astype(I32)
    rg_ref[...] = jnp.where(lane == 0, g1, g2)


def _proj_ln_route(a, b, x2, w_bf16, gain, bias, w_router, b_router, tm):
    T = x2.shape[0]
    half = D_MODEL // 2
    row = lambda i: (i, 0)
    const = lambda i: (0, 0)
    a_spec = pl.BlockSpec((tm, half), row)
    b_spec = pl.BlockSpec((tm, half), row if b is not a else (lambda i: (i, 1)))
    return pl.pallas_call(
        _proj_ln_route_kernel,
        grid=(T // tm,),
        in_specs=[a_spec, b_spec, pl.BlockSpec((tm, D_MODEL), row),
                  pl.BlockSpec((D_MODEL, D_MODEL), const),
                  pl.BlockSpec((1, D_MODEL), const), pl.BlockSpec((1, D_MODEL), const),
                  pl.BlockSpec((D_MODEL, LANES), const), pl.BlockSpec((1, LANES), const)],
        out_specs=[pl.BlockSpec((tm, D_MODEL), row), pl.BlockSpec((tm, LANES), row),
                   pl.BlockSpec((tm, LANES), row), pl.BlockSpec((1, LANES), const)],
        out_shape=[jax.ShapeDtypeStruct((T, D_MODEL), F32), jax.ShapeDtypeStruct((T, LANES), I32),
                   jax.ShapeDtypeStruct((T, LANES), F32), jax.ShapeDtypeStruct((1, LANES), F32)],
        scratch_shapes=[pltpu.VMEM((1, LANES), F32)],
        compiler_params=_params("arbitrary"),
        name="proj_ln_router",
    )(a, b, x2, w_bf16, gain, bias, w_router, b_router)


def _row_copy(src, src_row, dst, dst_row, sem):
    return pltpu.make_async_copy(src.at[pl.ds(src_row, 1), :], dst.at[pl.ds(dst_row, 1), :], sem)


def _dispatch_kernel(fill_row_ref, fill_on_ref, pos_ref, x_ref, xs_hbm, zero_scr, sem, *, tile):
    tm = x_ref.shape[0]

    @pl.when(pl.program_id(0) == 0)
    def _():
        zero_scr[...] = jnp.zeros_like(zero_scr)

        def fill(e):
            first = pl.multiple_of(fill_row_ref[e], tile)
            return pltpu.make_async_copy(zero_scr, xs_hbm.at[pl.ds(first, tile), :], sem)

        for e in range(2 * N_EXPERTS):
            @pl.when(fill_on_ref[e] > 0)
            def _():
                fill(e).start()
        for e in range(2 * N_EXPERTS):
            @pl.when(fill_on_ref[e] > 0)
            def _():
                fill(e).wait()

    def issue(r, carry):
        for k in range(2):
            _row_copy(x_ref, r, xs_hbm, pos_ref[0, 0, 2 * r + k], sem).start()
        return carry

    def drain(r, carry):
        for k in range(2):
            _row_copy(x_ref, r, xs_hbm, pos_ref[0, 0, 2 * r + k], sem).wait()
        return carry

    lax.fori_loop(0, tm, issue, 0)
    lax.fori_loop(0, tm, drain, 0)


def _dispatch(x1, pos, fill_row, fill_on, n_rows, tile, tm):
    T = x1.shape[0]
    pos3 = pos.reshape(T // tm, 1, 2 * tm)
    return pl.pallas_call(
        functools.partial(_dispatch_kernel, tile=tile),
        grid_spec=pltpu.PrefetchScalarGridSpec(
            num_scalar_prefetch=2,
            grid=(T // tm,),
            in_specs=[pl.BlockSpec((1, 1, 2 * tm), lambda i, *_: (i, 0, 0), memory_space=pltpu.SMEM),
                      pl.BlockSpec((tm, D_MODEL), lambda i, *_: (i, 0))],
            out_specs=pl.BlockSpec(memory_space=pl.ANY),
            scratch_shapes=[pltpu.VMEM((tile, D_MODEL), F32), pltpu.SemaphoreType.DMA(())],
        ),
        out_shape=jax.ShapeDtypeStruct((n_rows, D_MODEL), F32),
        compiler_params=_params("arbitrary"),
        name="moe_dispatch",
    )(fill_row, fill_on, pos3, x1)


def _expert_kernel(te_ref, nv_ref, xs_ref, wg_ref, wu_ref, wd_ref, ys_ref, wg_scr, wu_scr, wd_scr):
    i = pl.program_id(0)

    @pl.when(i < nv_ref[0])
    def _():
        @pl.when((i == 0) | (te_ref[i] != te_ref[jnp.maximum(i - 1, 0)]))
        def _():
            wg_scr[...] = wg_ref[0, 0].astype(BF16)
            wu_scr[...] = wu_ref[0, 0].astype(BF16)
            wd_scr[...] = wd_ref[0, 0].astype(BF16)

        xb = xs_ref[...].astype(BF16)
        hg = jnp.dot(xb, wg_scr[...], preferred_element_type=F32)
        hu = jnp.dot(xb, wu_scr[...], preferred_element_type=F32)
        hid = (_silu(hg) * hu).astype(BF16)
        ys_ref[...] = jnp.dot(hid, wd_scr[...], preferred_element_type=F32)

    @pl.when(i >= nv_ref[0])
    def _():
        ys_ref[...] = jnp.zeros_like(ys_ref)


def _experts(xs, tile_expert, n_valid, w_gate, w_up, w_down, layer, tile):
    n_tiles = xs.shape[0] // tile
    row = lambda i, te, nv: (jnp.minimum(i, nv[0] - 1), 0)
    wsel = lambda i, te, nv: (layer, te[i], 0, 0)
    return pl.pallas_call(
        _expert_kernel,
        grid_spec=pltpu.PrefetchScalarGridSpec(
            num_scalar_prefetch=2,
            grid=(n_tiles,),
            in_specs=[pl.BlockSpec((tile, D_MODEL), row),
                      pl.BlockSpec((1, 1, D_MODEL, EXPERT_HIDDEN), wsel),
                      pl.BlockSpec((1, 1, D_MODEL, EXPERT_HIDDEN), wsel),
                      pl.BlockSpec((1, 1, EXPERT_HIDDEN, D_MODEL), wsel)],
            out_specs=pl.BlockSpec((tile, D_MODEL), lambda i, te, nv: (i, 0)),
            scratch_shapes=[pltpu.VMEM((D_MODEL, EXPERT_HIDDEN), BF16),
                            pltpu.VMEM((D_MODEL, EXPERT_HIDDEN), BF16),
                            pltpu.VMEM((EXPERT_HIDDEN, D_MODEL), BF16)],
        ),
        out_shape=jax.ShapeDtypeStruct(xs.shape, F32),
        compiler_params=_params("arbitrary"),
        name="moe_experts",
    )(tile_expert, n_valid, xs, w_gate, w_up, w_down)


def _combine_ln_kernel(pos_ref, x_ref, g_ref, gain_ref, bias_ref, ys_hbm, o_ref, buf, sem):
    tm = x_ref.shape[0]

    def issue(r, carry):
        for k in range(2):
            _row_copy(ys_hbm, pos_ref[0, 0, 2 * r + k], buf.at[k], r, sem).start()
        return carry

    def drain(r, carry):
        for k in range(2):
            _row_copy(ys_hbm, pos_ref[0, 0, 2 * r + k], buf.at[k], r, sem).wait()
        return carry

    lax.fori_loop(0, tm, issue, 0)
    lax.fori_loop(0, tm, drain, 0)
    g = g_ref[...]
    ffn = buf[0] * g[:, 0:1] + buf[1] * g[:, 1:2]
    o_ref[...] = _layer_norm(ALPHA * x_ref[...] + ffn, gain_ref[...], bias_ref[...])


def _combine_ln(x1, ys, pos, gates, gain, bias, tm):
    T = x1.shape[0]
    pos3 = pos.reshape(T // tm, 1, 2 * tm)
    row = lambda i: (i, 0)
    const = lambda i: (0, 0)
    return pl.pallas_call(
        _combine_ln_kernel,
        grid=(T // tm,),
        in_specs=[pl.BlockSpec((1, 1, 2 * tm), lambda i: (i, 0, 0), memory_space=pltpu.SMEM),
                  pl.BlockSpec((tm, D_MODEL), row), pl.BlockSpec((tm, LANES), row),
                  pl.BlockSpec((1, D_MODEL), const), pl.BlockSpec((1, D_MODEL), const),
                  pl.BlockSpec(memory_space=pl.ANY)],
        out_specs=pl.BlockSpec((tm, D_MODEL), row),
        out_shape=jax.ShapeDtypeStruct((T, D_MODEL), F32),
        scratch_shapes=[pltpu.VMEM((2, tm, D_MODEL), F32), pltpu.SemaphoreType.DMA(())],
        compiler_params=_params("arbitrary"),
        name="moe_combine_ln",
    )(pos3, x1, gates, gain, bias, ys)


def _moe(x1, route_i, route_g, counts_f, w_gate, w_up, w_down, layer, gain, bias, tile, tm_rows):
    T = x1.shape[0]
    expert = route_i[:, 0:2]
    rank = route_i[:, 2:4]
    counts = counts_f[0, :N_EXPERTS].astype(I32)
    padded = (counts + tile - 1) // tile * tile
    ends = jnp.cumsum(padded)
    starts = ends - padded
    eids = jnp.arange(N_EXPERTS, dtype=I32)
    pos = rank + jnp.sum(jnp.where(expert[..., None] == eids, starts, 0), axis=-1)
    n_tiles = (2 * T) // tile + N_EXPERTS
    last_expert = jnp.max(jnp.where(counts > 0, eids, 0))
    tile_ids = jnp.arange(n_tiles, dtype=I32)
    tile_expert = jnp.sum((tile_ids[:, None] >= (ends // tile)[None, :]).astype(I32), axis=1)
    tile_expert = jnp.minimum(tile_expert, last_expert).astype(I32)
    n_valid = (ends[-1:] // tile).astype(I32)
    tail = n_valid[0] + eids
    fill_row = jnp.concatenate([ends - tile, jnp.minimum(tail, n_tiles - 1) * tile]).astype(I32)
    fill_on = jnp.concatenate([padded > 0, tail < n_tiles]).astype(I32)
    xs = _dispatch(x1, pos.astype(I32), fill_row, fill_on, n_tiles * tile, tile, tm_rows)
    ys = _experts(xs, tile_expert, n_valid, w_gate, w_up, w_down, layer, tile)
    return _combine_ln(x1, ys, pos.astype(I32), route_g, gain, bias, tm_rows)


def _pick(n, pref):
    t = min(n, pref)
    while n % t:
        t //= 2
    return t


def kernel(x, w_in_even, ret_decay_logit, ret_gn_gain, sink_logit, w_out_even, w_out_fourier,
           ln1_gain, ln1_bias, ln2_gain, ln2_bias, router_coarse_w, router_coarse_b,
           router_fine_w, router_fine_b, expert_w_gate, expert_w_up, expert_w_down):
    B, S, D = x.shape
    assert D == D_MODEL and S % (FFT_RADIX2 * 8) == 0 and S % WINDOW == 0
    T = B * S
    tm = _pick(S, 512)
    tile = _pick(T, 512)
    tm_rows = _pick(T, 256)
    rope_tabs = _rope_tables(S)
    fourier_tabs = _fourier_tables(S)
    row = lambda v: v.reshape(1, -1).astype(F32)

    x2 = x.reshape(T, D).astype(F32)
    for layer in range(DEPTH):
        if layer % 2 == 0:
            e = layer // 2
            qa, ka, va, ga, qb, kb, vb = _inproj(x2, w_in_even[e].astype(BF16), rope_tabs, S, tm)
            shp = lambda v: v.reshape(B, S, v.shape[-1])
            ya = _retention(shp(qa), shp(ka), shp(va), shp(ga), ret_decay_logit[e], ret_gn_gain[e])
            yb = _winattn(shp(qb), shp(kb), shp(vb), sink_logit[e])
            a, b = ya.reshape(T, RET_WIDTH), yb.reshape(T, WIN_WIDTH)
            w_out = w_out_even[e]
        else:
            a = b = _fourier(x2.reshape(B, S, D), fourier_tabs)
            w_out = w_out_fourier[layer // 2]
        w_router = jnp.zeros((D, LANES), F32)
        w_router = w_router.at[:, :N_EXPERTS].set(router_fine_w[layer])
        w_router = w_router.at[:, N_EXPERTS:N_EXPERTS + N_GROUPS].set(router_coarse_w[layer])
        b_router = jnp.zeros((1, LANES), F32)
        b_router = b_router.at[0, :N_EXPERTS].set(router_fine_b[layer])
        b_router = b_router.at[0, N_EXPERTS:N_EXPERTS + N_GROUPS].set(router_coarse_b[layer])
        x1, route_i, route_g, counts = _proj_ln_route(
            a, b, x2, w_out.astype(BF16), row(ln1_gain[layer]), row(ln1_bias[layer]),
            w_router.astype(BF16), b_router, tm)
        x2 = _moe(x1, route_i, route_g, counts, expert_w_gate, expert_w_up, expert_w_down, layer,
                  row(ln2_gain[layer]), row(ln2_bias[layer]), tile, tm_rows)
    return x2.reshape(B, S, D).astype(x.dtype)
```

```python
import functools
import math

import jax
import jax.numpy as jnp
from jax import lax
from jax.experimental import pallas as pl
from jax.experimental.pallas import tpu as pltpu

F32 = jnp.float32
BF16 = jnp.bfloat16
I32 = jnp.int32

D_MODEL = 1024
DEPTH = 4
RET_HEADS = 4
RET_DIM = 128
RET_BLOCK = 256
RET_WIDTH = RET_HEADS * RET_DIM
WIN_Q_HEADS = 8
WIN_KV_HEADS = 2
WIN_DIM = 64
WINDOW = 128
WIN_WIDTH = WIN_Q_HEADS * WIN_DIM
WIN_KV = WIN_KV_HEADS * WIN_DIM
FOURIER_GROUPS = 4
GROUP_CH = D_MODEL // FOURIER_GROUPS
FFT_RADIX2 = 16
ROPE_THETA = 10000.0
N_GROUPS = 4
EXPERTS_PER_GROUP = 8
N_EXPERTS = N_GROUPS * EXPERTS_PER_GROUP
EXPERT_HIDDEN = D_MODEL // 2
LN_EPS = 1e-5
GN_EPS = 1e-6
ALPHA = (2.0 * DEPTH) ** 0.25
IN_EVEN = 2 * RET_WIDTH + 2 * RET_WIDTH + WIN_WIDTH + 2 * WIN_KV

LANES = 128
VMEM_LIMIT_BYTES = 48 * 1024 * 1024
NEG_MASK = -1e30
ROW_DMA_UNROLL = 8

_NT = (((1,), (1,)), ((), ()))


def _params(*sem):
    return pltpu.CompilerParams(dimension_semantics=sem, vmem_limit_bytes=VMEM_LIMIT_BYTES)


def _silu(v):
    return v / (1.0 + jnp.exp(-v))


def _layer_norm(z, gain, bias):
    mu = jnp.mean(z, axis=-1, keepdims=True)
    zc = z - mu
    var = jnp.mean(zc * zc, axis=-1, keepdims=True)
    return zc * lax.rsqrt(var + LN_EPS) * gain + bias


def _rope128(h, cos, sin_signed):
    return h * cos + pltpu.roll(h, 64, 1) * sin_signed


def _rope64x2(h, cos, sin_lo, sin_hi):
    return h * cos + pltpu.roll(h, 96, 1) * sin_lo + pltpu.roll(h, 32, 1) * sin_hi


def _inproj_kernel(x_ref, w_ref, cr_ref, sr_ref, cw_ref, slo_ref, shi_ref,
                   qa_ref, ka_ref, va_ref, ga_ref, qb_ref, kb_ref, vb_ref):
    xb = x_ref[...].astype(BF16)

    def seg(lo, hi):
        return jnp.dot(xb, w_ref[:, lo:hi], preferred_element_type=F32)

    cr, sr = cr_ref[...], sr_ref[...]
    q = seg(0, RET_WIDTH)
    k = seg(RET_WIDTH, 2 * RET_WIDTH)
    for h in range(RET_HEADS):
        sl = slice(LANES * h, LANES * (h + 1))
        qa_ref[:, sl] = _rope128(q[:, sl], cr, sr).astype(BF16)
        ka_ref[sl, :] = (_rope128(k[:, sl], cr, sr) * RET_DIM ** -0.5).T.astype(BF16)
    va_ref[...] = seg(2 * RET_WIDTH, 3 * RET_WIDTH).astype(BF16)
    ga_ref[...] = _silu(seg(3 * RET_WIDTH, 4 * RET_WIDTH)).astype(BF16)
    cw, slo, shi = cw_ref[...], slo_ref[...], shi_ref[...]
    base = 4 * RET_WIDTH
    q = seg(base, base + WIN_WIDTH)
    for p in range(WIN_WIDTH // LANES):
        sl = slice(LANES * p, LANES * (p + 1))
        qb_ref[:, sl] = (_rope64x2(q[:, sl], cw, slo, shi) * WIN_DIM ** -0.5).astype(BF16)
    kv = seg(base + WIN_WIDTH, base + WIN_WIDTH + 2 * WIN_KV)
    kb = _rope64x2(kv[:, :WIN_KV], cw, slo, shi)
    vb = kv[:, WIN_KV:]
    kb_ref[:, :WIN_KV] = kb.astype(BF16)
    kb_ref[:, WIN_KV:] = pltpu.roll(kb, WIN_DIM, 1).astype(BF16)
    vb_ref[:, :WIN_KV] = vb.astype(BF16)
    vb_ref[:, WIN_KV:] = pltpu.roll(vb, WIN_DIM, 1).astype(BF16)


def _rope_tables(seq):
    pos = jnp.arange(seq, dtype=F32)[:, None]
    half = RET_DIM // 2
    inv = ROPE_THETA ** (-jnp.arange(half, dtype=F32) / half)
    ang = pos * inv[None, :]
    cr = jnp.concatenate([jnp.cos(ang), jnp.cos(ang)], axis=1)
    sr = jnp.concatenate([-jnp.sin(ang), jnp.sin(ang)], axis=1)
    half = WIN_DIM // 2
    inv = ROPE_THETA ** (-jnp.arange(half, dtype=F32) / half)
    ang = pos * inv[None, :]
    c, s, z = jnp.cos(ang), jnp.sin(ang), jnp.zeros_like(ang)
    cw = jnp.concatenate([c, c, c, c], axis=1)
    slo = jnp.concatenate([-s, z, -s, z], axis=1)
    shi = jnp.concatenate([z, s, z, s], axis=1)
    return cr, sr, cw, slo, shi


def _inproj(x2, w_bf16, tables, seq, tm):
    T = x2.shape[0]
    nseq = seq // tm
    row = lambda i: (i, 0)
    tab = lambda i: (i % nseq, 0)
    widths = (RET_WIDTH, None, RET_WIDTH, RET_WIDTH, WIN_WIDTH, 2 * WIN_KV, 2 * WIN_KV)
    return pl.pallas_call(
        _inproj_kernel,
        grid=(T // tm,),
        in_specs=[pl.BlockSpec((tm, D_MODEL), row),
                  pl.BlockSpec((D_MODEL, IN_EVEN), lambda i: (0, 0))]
                 + [pl.BlockSpec((tm, LANES), tab)] * 5,
        out_specs=[pl.BlockSpec((tm, w), row) if w else pl.BlockSpec((RET_WIDTH, tm), lambda i: (0, i))
                   for w in widths],
        out_shape=[jax.ShapeDtypeStruct((T, w) if w else (RET_WIDTH, T), BF16) for w in widths],
        compiler_params=_params("arbitrary"),
        name="inproj_rope",
    )(x2, w_bf16, *tables)


def _retention_kernel(logit_ref, q_ref, kt_ref, v_ref, g_ref, gain_ref, o_ref,
                      dmat_scr, kv_scr, state_scr):
    h = pl.program_id(1)
    C = RET_BLOCK
    dk = RET_DIM
    n_chunks = q_ref.shape[1] // C

    def log_gamma(d):
        v = jnp.full((1, 1), logit_ref[d, h], F32)
        return -(jnp.maximum(-v, 0.0) + jnp.log(1.0 + jnp.exp(-jnp.abs(v))))

    lgf, lgb = log_gamma(0), log_gamma(1)
    i = lax.broadcasted_iota(I32, (C, C), 0)
    j = lax.broadcasted_iota(I32, (C, C), 1)
    diff = (i - j).astype(F32)
    dmat_scr[...] = jnp.where(diff >= 0, jnp.exp(lgf * jnp.maximum(diff, 0.0)),
                              jnp.exp(lgb * jnp.maximum(-diff, 0.0)))
    col = lax.broadcasted_iota(I32, (C, 1), 0).astype(F32)
    lane = lax.broadcasted_iota(I32, (1, C), 1).astype(F32)
    xi_f, xi_b = jnp.exp(lgf * (col + 1.0)), jnp.exp(lgb * (C - col))
    zeta_f, zeta_b = jnp.exp(lgf * (C - 1.0 - lane)), jnp.exp(lgb * lane)
    dec_f, dec_b = jnp.exp(lgf * C), jnp.exp(lgb * C)
    gain = gain_ref[...]

    def span(n):
        return pl.ds(pl.multiple_of(n * C, C), C)

    def kv_pass(n, carry):
        kt = kt_ref[:, span(n)].astype(F32)
        lhs = jnp.concatenate([(kt * zeta_f).astype(BF16), (kt * zeta_b).astype(BF16)], axis=0)
        kv_scr[n] = jnp.dot(lhs, v_ref[0, span(n), :], preferred_element_type=F32)
        return carry

    lax.fori_loop(0, n_chunks, kv_pass, 0)

    def scan_f(n, state):
        state_scr[n, :dk, :] = state.astype(BF16)
        return state * dec_f + kv_scr[n, :dk, :]

    def scan_b(t, state):
        n = n_chunks - 1 - t
        state_scr[n, dk:, :] = state.astype(BF16)
        return state * dec_b + kv_scr[n, dk:, :]

    zero = jnp.zeros((dk, dk), F32)
    lax.fori_loop(0, n_chunks, scan_f, zero)
    lax.fori_loop(0, n_chunks, scan_b, zero)

    def out_pass(n, carry):
        q = q_ref[0, span(n), :]
        s = jnp.dot(q, kt_ref[:, span(n)], preferred_element_type=F32)
        p = (s * dmat_scr[...]).astype(BF16)
        qf = q.astype(F32)
        qx = jnp.concatenate([(qf * xi_f).astype(BF16), (qf * xi_b).astype(BF16)], axis=1)
        y = (jnp.dot(p, v_ref[0, span(n), :], preferred_element_type=F32)
             + jnp.dot(qx, state_scr[n], preferred_element_type=F32))
        mu = jnp.mean(y, axis=-1, keepdims=True)
        yc = y - mu
        var = jnp.mean(yc * yc, axis=-1, keepdims=True)
        yn = yc * lax.rsqrt(var + GN_EPS) * gain
        o_ref[0, span(n), :] = (g_ref[0, span(n), :].astype(F32) * yn).astype(BF16)
        return carry

    lax.fori_loop(0, n_chunks, out_pass, 0)


def _retention(qa, ka_t, va, ga, decay_logit, gn_gain):
    B, S, _ = qa.shape
    n_chunks = S // RET_BLOCK
    head = pl.BlockSpec((1, S, RET_DIM), lambda b, h: (b, 0, h))
    return pl.pallas_call(
        _retention_kernel,
        grid=(B, RET_HEADS),
        in_specs=[pl.BlockSpec(memory_space=pltpu.SMEM), head,
                  pl.BlockSpec((RET_DIM, S), lambda b, h: (h, b)), head, head,
                  pl.BlockSpec((1, RET_DIM), lambda b, h: (0, h))],
        out_specs=head,
        out_shape=jax.ShapeDtypeStruct((B, S, RET_WIDTH), BF16),
        scratch_shapes=[pltpu.VMEM((RET_BLOCK, RET_BLOCK), F32),
                        pltpu.VMEM((n_chunks, 2 * RET_DIM, RET_DIM), F32),
                        pltpu.VMEM((n_chunks, 2 * RET_DIM, RET_DIM), BF16)],
        compiler_params=_params("arbitrary", "arbitrary"),
        name="retention_gn_gate",
    )(decay_logit.astype(F32), qa, ka_t, va, ga, gn_gain.reshape(1, RET_WIDTH).astype(F32))


def _winattn_kernel(sink_ref, q_ref, kp_ref, kc_ref, kn_ref, vp_ref, vc_ref, vn_ref, o_ref):
    step = pl.program_id(1)
    n_steps = pl.num_programs(1)
    W = WINDOW
    nq = q_ref.shape[1] // W
    group = WIN_Q_HEADS // WIN_KV_HEADS
    k_all = jnp.concatenate([kp_ref[0], kc_ref[0], kn_ref[0]], axis=0)
    v_all = jnp.concatenate([vp_ref[0], vc_ref[0], vn_ref[0]], axis=0)
    lo_half = lax.broadcasted_iota(I32, (1, LANES), 1) < WIN_DIM

    def placements(slab, g):
        first, second = slab[:, :LANES], slab[:, LANES:]
        zero = jnp.zeros_like(first)
        if g == 0:
            return jnp.where(lo_half, first, zero), jnp.where(lo_half, zero, second)
        return jnp.where(lo_half, second, zero), jnp.where(lo_half, zero, first)

    rows2 = lax.broadcasted_iota(I32, (2 * W, 1), 0)
    qi = lax.broadcasted_iota(I32, (2 * W, 3 * W), 0) & (W - 1)
    cj = lax.broadcasted_iota(I32, (2 * W, 3 * W), 1)
    rel = cj - qi
    band = (rel >= 0) & (rel <= 2 * W)

    for g in range(WIN_KV_HEADS):
        k_even, k_odd = placements(k_all, g)
        v_even, v_odd = placements(v_all, g)
        c0 = group * WIN_DIM * g
        sink_even = jnp.where(rows2 < W, sink_ref[0, group * g], sink_ref[0, group * g + 2])
        sink_odd = jnp.where(rows2 < W, sink_ref[0, group * g + 1], sink_ref[0, group * g + 3])
        for jq in range(nq):
            qrows = slice(jq * W, (jq + 1) * W)
            krows = slice(jq * W, (jq + 3) * W)
            mask = band
            if jq == 0:
                mask = mask & (cj >= jnp.where(step == 0, W, 0))
            if jq == nq - 1:
                mask = mask & (cj < jnp.where(step == n_steps - 1, 2 * W, 3 * W))
            q2 = jnp.concatenate([q_ref[0, qrows, c0:c0 + LANES],
                                  q_ref[0, qrows, c0 + LANES:c0 + 2 * LANES]], axis=0)

            def softmax_parts(k_placed, sink):
                s = lax.dot_general(q2, k_placed[krows], _NT, preferred_element_type=F32)
                s = jnp.where(mask, s, NEG_MASK)
                m = jnp.maximum(jnp.max(s, axis=1, keepdims=True), sink)
                e = jnp.exp(s - m)
                return e.astype(BF16), jnp.sum(e, axis=1, keepdims=True) + jnp.exp(sink - m)

            p_even, den_even = softmax_parts(k_even, sink_even)
            p_odd, den_odd = softmax_parts(k_odd, sink_odd)
            o = (jnp.dot(p_even, v_even[krows], preferred_element_type=F32)
                 + jnp.dot(p_odd, v_odd[krows], preferred_element_type=F32))
            o = (o / jnp.where(lo_half, den_even, den_odd)).astype(BF16)
            o_ref[0, qrows, c0:c0 + LANES] = o[:W]
            o_ref[0, qrows, c0 + LANES:c0 + 2 * LANES] = o[W:]


def _winattn(qb, kb2, vb2, sink_logit, blocks_per_step):
    B, S, _ = qb.shape
    nq = blocks_per_step
    nb = S // WINDOW
    prev = pl.BlockSpec((1, WINDOW, 2 * WIN_KV), lambda b, s: (b, jnp.maximum(s * nq - 1, 0), 0))
    cur = pl.BlockSpec((1, nq * WINDOW, 2 * WIN_KV), lambda b, s: (b, s, 0))
    nxt = pl.BlockSpec((1, WINDOW, 2 * WIN_KV), lambda b, s: (b, jnp.minimum((s + 1) * nq, nb - 1), 0))
    qspec = pl.BlockSpec((1, nq * WINDOW, WIN_WIDTH), lambda b, s: (b, s, 0))
    return pl.pallas_call(
        _winattn_kernel,
        grid=(B, nb // nq),
        in_specs=[pl.BlockSpec(memory_space=pltpu.SMEM), qspec, prev, cur, nxt, prev, cur, nxt],
        out_specs=qspec,
        out_shape=jax.ShapeDtypeStruct((B, S, WIN_WIDTH), BF16),
        compiler_params=_params("arbitrary", "arbitrary"),
        name="window_attention",
    )(sink_logit.reshape(1, WIN_Q_HEADS).astype(F32), qb, kb2, kb2, kb2, vb2, vb2, vb2)


def _chan_dft_kernel(xa_ref, xb_ref, m_ref, wr_ref, wi_ref):
    n1 = xa_ref.shape[1] // FFT_RADIX2
    for s in range(FFT_RADIX2):
        rows = pl.ds(s, n1, stride=FFT_RADIX2)
        w = (jnp.dot(xa_ref[0, rows, :].astype(BF16), m_ref[:LANES, :], preferred_element_type=F32)
             + jnp.dot(xb_ref[0, rows, :].astype(BF16), m_ref[LANES:, :], preferred_element_type=F32))
        wr_ref[0, s] = w[:, :GROUP_CH].astype(BF16)
        wi_ref[0, s] = w[:, GROUP_CH:].astype(BF16)


def _real_dft16(ar, ai):
    n = FFT_RADIX2
    cs = [math.cos(2 * math.pi * m / n) for m in range(n)]
    sn = [math.sin(2 * math.pi * m / n) for m in range(n)]

    def axpy(acc, coef, v):
        if abs(coef) < 1e-12:
            return acc
        if abs(abs(coef) - 1.0) < 1e-12:
            if acc is None:
                return v if coef > 0 else -v
            return acc + v if coef > 0 else acc - v
        t = coef * v
        return t if acc is None else acc + t

    p = {s: ar[s] + ar[n - s] for s in range(1, n // 2)}
    d = {s: ai[s] - ai[n - s] for s in range(1, n // 2)}
    base = (ar[0] + ar[n // 2], ar[0] - ar[n // 2])
    y = [None] * n
    for k in range(n // 2 + 1):
        e = base[k % 2]
        for s in range(1, n // 2):
            e = axpy(e, cs[(s * k) % n], p[s])
        if k in (0, n // 2):
            y[k] = e
            continue
        o = None
        for s in range(1, n // 2):
            o = axpy(o, sn[(s * k) % n], d[s])
        y[k] = e + o
        y[n - k] = e - o
    return y


def _seq_dft_kernel(wr_ref, wi_ref, cf_ref, sf_ref, y_ref, ar_scr, ai_scr):
    n1 = wr_ref.shape[2]
    for s in range(FFT_RADIX2):
        cf, sf, wr, wi = cf_ref[s], sf_ref[s], wr_ref[0, s], wi_ref[0, s]
        ar_scr[s] = (jnp.dot(cf, wr, preferred_element_type=F32)
                     + jnp.dot(sf, wi, preferred_element_type=F32))
        ai_scr[s] = (jnp.dot(cf, wi, preferred_element_type=F32)
                     - jnp.dot(sf, wr, preferred_element_type=F32))

    def rows8(r, carry):
        rows = pl.ds(pl.multiple_of(r * 8, 8), 8)
        y = _real_dft16([ar_scr[s, rows, :] for s in range(FFT_RADIX2)],
                        [ai_scr[s, rows, :] for s in range(FFT_RADIX2)])
        for k in range(FFT_RADIX2):
            y_ref[0, k, rows, :] = y[k].astype(BF16)
        return carry

    lax.fori_loop(0, n1 // 8, rows8, 0)


def _fourier_tables(seq):
    n1 = seq // FFT_RADIX2
    c = jnp.arange(GROUP_CH, dtype=I32)
    m = (c[:, None] * c[None, :]) % GROUP_CH
    ang = m.astype(F32) * (2.0 * math.pi / GROUP_CH)
    scale = (seq * GROUP_CH) ** -0.5
    chan = (jnp.concatenate([jnp.cos(ang), -jnp.sin(ang)], axis=1) * scale).astype(BF16)
    k1 = jnp.arange(n1, dtype=I32)[None, :, None]
    s1 = jnp.arange(n1, dtype=I32)[None, None, :]
    s2 = jnp.arange(FFT_RADIX2, dtype=I32)[:, None, None]
    m = (FFT_RADIX2 * s1 * k1 + s2 * k1) % seq
    ang = m.astype(F32) * (2.0 * math.pi / seq)
    return chan, jnp.cos(ang).astype(BF16), jnp.sin(ang).astype(BF16)


def _fourier(x, tables, tn=256):
    B, S, _ = x.shape
    n1 = S // FFT_RADIX2
    chan, cf, sf = tables
    wshape = jax.ShapeDtypeStruct((B, FFT_RADIX2, n1, D_MODEL), BF16)
    wblk = pl.BlockSpec((1, FFT_RADIX2, n1, GROUP_CH), lambda b, g: (b, 0, 0, g))
    wr, wi = pl.pallas_call(
        _chan_dft_kernel,
        grid=(B, FOURIER_GROUPS),
        in_specs=[pl.BlockSpec((1, S, LANES), lambda b, g: (b, 0, 2 * g)),
                  pl.BlockSpec((1, S, LANES), lambda b, g: (b, 0, 2 * g + 1)),
                  pl.BlockSpec((GROUP_CH, 2 * GROUP_CH), lambda b, g: (0, 0))],
        out_specs=[wblk, wblk],
        out_shape=[wshape, wshape],
        compiler_params=_params("arbitrary", "arbitrary"),
        name="fourier_channel_dft",
    )(x, x, chan)
    cols = pl.BlockSpec((1, FFT_RADIX2, n1, tn), lambda b, j: (b, 0, 0, j))
    full = pl.BlockSpec((FFT_RADIX2, n1, n1), lambda b, j: (0, 0, 0))
    y = pl.pallas_call(
        _seq_dft_kernel,
        grid=(B, D_MODEL // tn),
        in_specs=[cols, cols, full, full],
        out_specs=cols,
        out_shape=wshape,
        scratch_shapes=[pltpu.VMEM((FFT_RADIX2, n1, tn), F32), pltpu.VMEM((FFT_RADIX2, n1, tn), F32)],
        compiler_params=_params("arbitrary", "arbitrary"),
        name="fourier_sequence_dft",
    )(wr, wi, cf, sf)
    return y.reshape(B * S, D_MODEL)


def _proj_ln_route_kernel(a_ref, b_ref, x_ref, w_ref, gain_ref, bias_ref, wr_ref, br_ref,
                          xo_ref, ri_ref, rg_ref, cnt_ref, carry_scr):
    tm = x_ref.shape[0]
    half = a_ref.shape[1]

    @pl.when(pl.program_id(0) == 0)
    def _():
        carry_scr[...] = jnp.zeros_like(carry_scr)

    mix = (jnp.dot(a_ref[...], w_ref[:half, :], preferred_element_type=F32)
           + jnp.dot(b_ref[...], w_ref[half:, :], preferred_element_type=F32))
    y = _layer_norm(ALPHA * x_ref[...] + mix, gain_ref[...], bias_ref[...])
    xo_ref[...] = y

    logits = jnp.dot(y.astype(BF16), wr_ref[...], preferred_element_type=F32) + br_ref[...]
    lane = lax.broadcasted_iota(I32, (tm, LANES), 1)
    lanef = lane.astype(F32)
    ninf = -jnp.inf
    big = 1e9

    def rmax(v):
        return jnp.max(v, axis=1, keepdims=True)

    def first_lane(hit):
        return jnp.min(jnp.where(hit, lanef, big), axis=1, keepdims=True)

    cmask = (lane >= N_EXPERTS) & (lane < N_EXPERTS + N_GROUPS)
    cl = jnp.where(cmask, logits, ninf)
    cmax = rmax(cl)
    group = first_lane(cl == cmax) - float(N_EXPERTS)
    p_group = 1.0 / jnp.sum(jnp.where(cmask, jnp.exp(cl - cmax), 0.0), axis=1, keepdims=True)
    lo = group * float(EXPERTS_PER_GROUP)
    fmask = (lanef >= lo) & (lanef < lo + float(EXPERTS_PER_GROUP))
    fl = jnp.where(fmask, logits, ninf)
    v1 = rmax(fl)
    e1 = first_lane(fl == v1)
    fl2 = jnp.where(lanef == e1, ninf, fl)
    v2 = rmax(fl2)
    e2 = first_lane(fl2 == v2)
    t = jnp.exp(v2 - v1)
    g1 = p_group / (1.0 + t)
    g2 = p_group * t / (1.0 + t)

    hit1, hit2 = lanef == e1, lanef == e2
    onehot = jnp.where(hit1 | hit2, 1.0, 0.0)
    r = lax.broadcasted_iota(I32, (tm, tm), 0)
    cidx = lax.broadcasted_iota(I32, (tm, tm), 1)
    lower = jnp.where(r > cidx, 1.0, 0.0).astype(BF16)
    prefix = jnp.dot(lower, onehot.astype(BF16), preferred_element_type=F32) + carry_scr[...]
    rank1 = jnp.sum(jnp.where(hit1, prefix, 0.0), axis=1, keepdims=True)
    rank2 = jnp.sum(jnp.where(hit2, prefix, 0.0), axis=1, keepdims=True)
    carry_scr[...] += jnp.sum(onehot, axis=0, keepdims=True)
    cnt_ref[...] = carry_scr[...]
    ri_ref[...] = jnp.where(lane == 0, e1, jnp.where(lane == 1, e2,
                            jnp.where(lane == 2, rank1, rank2))).astype(I32)
    rg_ref[...] = jnp.where(lane == 0, g1, g2)


def _proj_ln_route(a, b, x2, w_bf16, gain, bias, w_router, b_router, tm):
    T = x2.shape[0]
    half = D_MODEL // 2
    row = lambda i: (i, 0)
    const = lambda i: (0, 0)
    a_spec = pl.BlockSpec((tm, half), row)
    b_spec = pl.BlockSpec((tm, half), row if b is not a else (lambda i: (i, 1)))
    return pl.pallas_call(
        _proj_ln_route_kernel,
        grid=(T // tm,),
        in_specs=[a_spec, b_spec, pl.BlockSpec((tm, D_MODEL), row),
                  pl.BlockSpec((D_MODEL, D_MODEL), const),
                  pl.BlockSpec((1, D_MODEL), const), pl.BlockSpec((1, D_MODEL), const),
                  pl.BlockSpec((D_MODEL, LANES), const), pl.BlockSpec((1, LANES), const)],
        out_specs=[pl.BlockSpec((tm, D_MODEL), row), pl.BlockSpec((tm, LANES), row),
                   pl.BlockSpec((tm, LANES), row), pl.BlockSpec((1, LANES), const)],
        out_shape=[jax.ShapeDtypeStruct((T, D_MODEL), F32), jax.ShapeDtypeStruct((T, LANES), I32),
                   jax.ShapeDtypeStruct((T, LANES), F32), jax.ShapeDtypeStruct((1, LANES), F32)],
        scratch_shapes=[pltpu.VMEM((1, LANES), F32)],
        compiler_params=_params("arbitrary"),
        name="proj_ln_router",
    )(a, b, x2, w_bf16, gain, bias, w_router, b_router)


def _row_copy(src, src_row, dst, dst_row, sem):
    return pltpu.make_async_copy(src.at[pl.ds(src_row, 1), :], dst.at[pl.ds(dst_row, 1), :], sem)


def _dispatch_kernel(fill_row_ref, fill_on_ref, pos_ref, x_ref, xs_hbm, zero_scr, sem, *, tile):
    tm = x_ref.shape[0]

    @pl.when(pl.program_id(0) == 0)
    def _():
        zero_scr[...] = jnp.zeros_like(zero_scr)

        def fill(e):
            first = pl.multiple_of(fill_row_ref[e], tile)
            return pltpu.make_async_copy(zero_scr, xs_hbm.at[pl.ds(first, tile), :], sem)

        for e in range(2 * N_EXPERTS):
            @pl.when(fill_on_ref[e] > 0)
            def _():
                fill(e).start()
        for e in range(2 * N_EXPERTS):
            @pl.when(fill_on_ref[e] > 0)
            def _():
                fill(e).wait()

    def issue(r, carry):
        for k in range(2):
            _row_copy(x_ref, r, xs_hbm, pos_ref[0, 0, 2 * r + k], sem).start()
        return carry

    lax.fori_loop(0, tm, issue, 0, unroll=ROW_DMA_UNROLL)
    for k in range(2):
        pltpu.make_async_copy(x_ref, xs_hbm.at[pl.ds(0, tm), :], sem).wait()


def _dispatch(x1, pos, fill_row, fill_on, n_rows, tile, tm):
    T = x1.shape[0]
    pos3 = pos.reshape(T // tm, 1, 2 * tm)
    return pl.pallas_call(
        functools.partial(_dispatch_kernel, tile=tile),
        grid_spec=pltpu.PrefetchScalarGridSpec(
            num_scalar_prefetch=2,
            grid=(T // tm,),
            in_specs=[pl.BlockSpec((1, 1, 2 * tm), lambda i, *_: (i, 0, 0), memory_space=pltpu.SMEM),
                      pl.BlockSpec((tm, D_MODEL), lambda i, *_: (i, 0))],
            out_specs=pl.BlockSpec(memory_space=pl.ANY),
            scratch_shapes=[pltpu.VMEM((tile, D_MODEL), F32), pltpu.SemaphoreType.DMA(())],
        ),
        out_shape=jax.ShapeDtypeStruct((n_rows, D_MODEL), F32),
        compiler_params=_params("arbitrary"),
        name="moe_dispatch",
    )(fill_row, fill_on, pos3, x1)


def _expert_kernel(te_ref, nv_ref, xs_ref, wg_ref, wu_ref, wd_ref, ys_ref, wg_scr, wu_scr, wd_scr):
    i = pl.program_id(0)

    @pl.when(i < nv_ref[0])
    def _():
        @pl.when((i == 0) | (te_ref[i] != te_ref[jnp.maximum(i - 1, 0)]))
        def _():
            wg_scr[...] = wg_ref[0, 0].astype(BF16)
            wu_scr[...] = wu_ref[0, 0].astype(BF16)
            wd_scr[...] = wd_ref[0, 0].astype(BF16)

        xb = xs_ref[...].astype(BF16)
        hg = jnp.dot(xb, wg_scr[...], preferred_element_type=F32)
        hu = jnp.dot(xb, wu_scr[...], preferred_element_type=F32)
        hid = (_silu(hg) * hu).astype(BF16)
        ys_ref[...] = jnp.dot(hid, wd_scr[...], preferred_element_type=F32)

    @pl.when(i >= nv_ref[0])
    def _():
        ys_ref[...] = jnp.zeros_like(ys_ref)


def _experts(xs, tile_expert, n_valid, w_gate, w_up, w_down, layer, tile):
    n_tiles = xs.shape[0] // tile
    row = lambda i, te, nv: (jnp.minimum(i, nv[0] - 1), 0)
    wsel = lambda i, te, nv: (layer, te[i], 0, 0)
    return pl.pallas_call(
        _expert_kernel,
        grid_spec=pltpu.PrefetchScalarGridSpec(
            num_scalar_prefetch=2,
            grid=(n_tiles,),
            in_specs=[pl.BlockSpec((tile, D_MODEL), row),
                      pl.BlockSpec((1, 1, D_MODEL, EXPERT_HIDDEN), wsel),
                      pl.BlockSpec((1, 1, D_MODEL, EXPERT_HIDDEN), wsel),
                      pl.BlockSpec((1, 1, EXPERT_HIDDEN, D_MODEL), wsel)],
            out_specs=pl.BlockSpec((tile, D_MODEL), lambda i, te, nv: (i, 0)),
            scratch_shapes=[pltpu.VMEM((D_MODEL, EXPERT_HIDDEN), BF16),
                            pltpu.VMEM((D_MODEL, EXPERT_HIDDEN), BF16),
                            pltpu.VMEM((EXPERT_HIDDEN, D_MODEL), BF16)],
        ),
        out_shape=jax.ShapeDtypeStruct(xs.shape, F32),
        compiler_params=_params("arbitrary"),
        name="moe_experts",
    )(tile_expert, n_valid, xs, w_gate, w_up, w_down)


def _combine_ln_kernel(pos_ref, x_ref, g_ref, gain_ref, bias_ref, ys_hbm, o_ref, buf, sem):
    tm = x_ref.shape[0]

    def issue(r, carry):
        for k in range(2):
            _row_copy(ys_hbm, pos_ref[0, 0, 2 * r + k], buf.at[k], r, sem).start()
        return carry

    lax.fori_loop(0, tm, issue, 0, unroll=ROW_DMA_UNROLL)
    for k in range(2):
        pltpu.make_async_copy(ys_hbm.at[pl.ds(0, tm), :], buf.at[k], sem).wait()
    g = g_ref[...]
    ffn = buf[0] * g[:, 0:1] + buf[1] * g[:, 1:2]
    o_ref[...] = _layer_norm(ALPHA * x_ref[...] + ffn, gain_ref[...], bias_ref[...])


def _combine_ln(x1, ys, pos, gates, gain, bias, tm):
    T = x1.shape[0]
    pos3 = pos.reshape(T // tm, 1, 2 * tm)
    row = lambda i: (i, 0)
    const = lambda i: (0, 0)
    return pl.pallas_call(
        _combine_ln_kernel,
        grid=(T // tm,),
        in_specs=[pl.BlockSpec((1, 1, 2 * tm), lambda i: (i, 0, 0), memory_space=pltpu.SMEM),
                  pl.BlockSpec((tm, D_MODEL), row), pl.BlockSpec((tm, LANES), row),
                  pl.BlockSpec((1, D_MODEL), const), pl.BlockSpec((1, D_MODEL), const),
                  pl.BlockSpec(memory_space=pl.ANY)],
        out_specs=pl.BlockSpec((tm, D_MODEL), row),
        out_shape=jax.ShapeDtypeStruct((T, D_MODEL), F32),
        scratch_shapes=[pltpu.VMEM((2, tm, D_MODEL), F32), pltpu.SemaphoreType.DMA(())],
        compiler_params=_params("arbitrary"),
        name="moe_combine_ln",
    )(pos3, x1, gates, gain, bias, ys)


def _moe(x1, route_i, route_g, counts_f, w_gate, w_up, w_down, layer, gain, bias, tile, tm_rows):
    T = x1.shape[0]
    expert = route_i[:, 0:2]
    rank = route_i[:, 2:4]
    counts = counts_f[0, :N_EXPERTS].astype(I32)
    padded = (counts + tile - 1) // tile * tile
    ends = jnp.cumsum(padded)
    starts = ends - padded
    eids = jnp.arange(N_EXPERTS, dtype=I32)
    pos = rank + jnp.sum(jnp.where(expert[..., None] == eids, starts, 0), axis=-1)
    n_tiles = (2 * T) // tile + N_EXPERTS
    last_expert = jnp.max(jnp.where(counts > 0, eids, 0))
    tile_ids = jnp.arange(n_tiles, dtype=I32)
    tile_expert = jnp.sum((tile_ids[:, None] >= (ends // tile)[None, :]).astype(I32), axis=1)
    tile_expert = jnp.minimum(tile_expert, last_expert).astype(I32)
    n_valid = (ends[-1:] // tile).astype(I32)
    tail = n_valid[0] + eids
    fill_row = jnp.concatenate([ends - tile, jnp.minimum(tail, n_tiles - 1) * tile]).astype(I32)
    fill_on = jnp.concatenate([padded > 0, tail < n_tiles]).astype(I32)
    xs = _dispatch(x1, pos.astype(I32), fill_row, fill_on, n_tiles * tile, tile, tm_rows)
    ys = _experts(xs, tile_expert, n_valid, w_gate, w_up, w_down, layer, tile)
    return _combine_ln(x1, ys, pos.astype(I32), route_g, gain, bias, tm_rows)


def _pick(n, pref):
    t = min(n, pref)
    while n % t:
        t //= 2
    return t


def kernel(x, w_in_even, ret_decay_logit, ret_gn_gain, sink_logit, w_out_even, w_out_fourier,
           ln1_gain, ln1_bias, ln2_gain, ln2_bias, router_coarse_w, router_coarse_b,
           router_fine_w, router_fine_b, expert_w_gate, expert_w_up, expert_w_down):
    B, S, D = x.shape
    assert D == D_MODEL and S % (FFT_RADIX2 * 8) == 0 and S % WINDOW == 0 and S % RET_BLOCK == 0
    T = B * S
    tm = _pick(S, 512)
    tile = _pick(T, 512)
    tm_rows = _pick(T, 256)
    rope_tabs = _rope_tables(S)
    fourier_tabs = _fourier_tables(S)
    row = lambda v: v.reshape(1, -1).astype(F32)

    x2 = x.reshape(T, D).astype(F32)
    for layer in range(DEPTH):
        if layer % 2 == 0:
            e = layer // 2
            qa, ka_t, va, ga, qb, kb2, vb2 = _inproj(x2, w_in_even[e].astype(BF16), rope_tabs, S, tm)
            shp = lambda v: v.reshape(B, S, v.shape[-1])
            ya = _retention(shp(qa), ka_t, shp(va), shp(ga), ret_decay_logit[e], ret_gn_gain[e])
            yb = _winattn(shp(qb), shp(kb2), shp(vb2), sink_logit[e], _pick(S // WINDOW, 4))
            a, b = ya.reshape(T, RET_WIDTH), yb.reshape(T, WIN_WIDTH)
            w_out = w_out_even[e]
        else:
            a = b = _fourier(x2.reshape(B, S, D), fourier_tabs)
            w_out = w_out_fourier[layer // 2]
        w_router = jnp.zeros((D, LANES), F32)
        w_router = w_router.at[:, :N_EXPERTS].set(router_fine_w[layer])
        w_router = w_router.at[:, N_EXPERTS:N_EXPERTS + N_GROUPS].set(router_coarse_w[layer])
        b_router = jnp.zeros((1, LANES), F32)
        b_router = b_router.at[0, :N_EXPERTS].set(router_fine_b[layer])
        b_router = b_router.at[0, N_EXPERTS:N_EXPERTS + N_GROUPS].set(router_coarse_b[layer])
        x1, route_i, route_g, counts = _proj_ln_route(
            a, b, x2, w_out.astype(BF16), row(ln1_gain[layer]), row(ln1_bias[layer]),
            w_router.astype(BF16), b_router, tm)
        x2 = _moe(x1, route_i, route_g, counts, expert_w_gate, expert_w_up, expert_w_down, layer,
                  row(ln2_gain[layer]), row(ln2_bias[layer]), tile, tm_rows)
    return x2.reshape(B, S, D).astype(x.dtype)
```

```python
import functools
import math

import jax
import jax.numpy as jnp
from jax import lax
from jax.experimental import pallas as pl
from jax.experimental.pallas import tpu as pltpu

F32 = jnp.float32
BF16 = jnp.bfloat16
I32 = jnp.int32

D_MODEL = 1024
DEPTH = 4
RET_HEADS = 4
RET_DIM = 128
RET_BLOCK = 256
RET_WIDTH = RET_HEADS * RET_DIM
WIN_Q_HEADS = 8
WIN_KV_HEADS = 2
WIN_DIM = 64
WINDOW = 128
WIN_WIDTH = WIN_Q_HEADS * WIN_DIM
WIN_KV = WIN_KV_HEADS * WIN_DIM
FOURIER_GROUPS = 4
GROUP_CH = D_MODEL // FOURIER_GROUPS
FFT_RADIX2 = 16
ROPE_THETA = 10000.0
N_GROUPS = 4
EXPERTS_PER_GROUP = 8
N_EXPERTS = N_GROUPS * EXPERTS_PER_GROUP
EXPERT_HIDDEN = D_MODEL // 2
LN_EPS = 1e-5
GN_EPS = 1e-6
ALPHA = (2.0 * DEPTH) ** 0.25
IN_EVEN = 2 * RET_WIDTH + 2 * RET_WIDTH + WIN_WIDTH + 2 * WIN_KV

LANES = 128
VMEM_LIMIT_BYTES = 48 * 1024 * 1024
NEG_MASK = -1e30
ROW_DMA_UNROLL = 8

_NT = (((1,), (1,)), ((), ()))


def _params(*sem):
    return pltpu.CompilerParams(dimension_semantics=sem, vmem_limit_bytes=VMEM_LIMIT_BYTES)


def _silu(v):
    return v / (1.0 + jnp.exp(-v))


def _layer_norm(z, gain, bias):
    mu = jnp.mean(z, axis=-1, keepdims=True)
    zc = z - mu
    var = jnp.mean(zc * zc, axis=-1, keepdims=True)
    return zc * lax.rsqrt(var + LN_EPS) * gain + bias


def _rope128(h, cos, sin_signed):
    return h * cos + pltpu.roll(h, 64, 1) * sin_signed


def _rope64x2(h, cos, sin_lo, sin_hi):
    return h * cos + pltpu.roll(h, 96, 1) * sin_lo + pltpu.roll(h, 32, 1) * sin_hi


def _inproj_kernel(x_ref, w_ref, cr_ref, sr_ref, cw_ref, slo_ref, shi_ref,
                   qa_ref, ka_ref, va_ref, ga_ref, qb_ref, kb_ref, vb_ref):
    xb = x_ref[...].astype(BF16)

    def seg(lo, hi):
        return jnp.dot(xb, w_ref[:, lo:hi], preferred_element_type=F32)

    cr, sr = cr_ref[...], sr_ref[...]
    q = seg(0, RET_WIDTH)
    k = seg(RET_WIDTH, 2 * RET_WIDTH)
    for h in range(RET_HEADS):
        sl = slice(LANES * h, LANES * (h + 1))
        qa_ref[:, sl] = _rope128(q[:, sl], cr, sr).astype(BF16)
        ka_ref[sl, :] = (_rope128(k[:, sl], cr, sr) * RET_DIM ** -0.5).T.astype(BF16)
    va_ref[...] = seg(2 * RET_WIDTH, 3 * RET_WIDTH).astype(BF16)
    ga_ref[...] = _silu(seg(3 * RET_WIDTH, 4 * RET_WIDTH)).astype(BF16)
    cw, slo, shi = cw_ref[...], slo_ref[...], shi_ref[...]
    base = 4 * RET_WIDTH
    q = seg(base, base + WIN_WIDTH)
    for p in range(WIN_WIDTH // LANES):
        sl = slice(LANES * p, LANES * (p + 1))
        qb_ref[:, sl] = (_rope64x2(q[:, sl], cw, slo, shi) * WIN_DIM ** -0.5).astype(BF16)
    kv = seg(base + WIN_WIDTH, base + WIN_WIDTH + 2 * WIN_KV)
    kb = _rope64x2(kv[:, :WIN_KV], cw, slo, shi)
    vb = kv[:, WIN_KV:]
    kb_ref[:, :WIN_KV] = kb.astype(BF16)
    kb_ref[:, WIN_KV:] = pltpu.roll(kb, WIN_DIM, 1).astype(BF16)
    vb_ref[:, :WIN_KV] = vb.astype(BF16)
    vb_ref[:, WIN_KV:] = pltpu.roll(vb, WIN_DIM, 1).astype(BF16)


def _rope_tables(seq):
    pos = jnp.arange(seq, dtype=F32)[:, None]
    half = RET_DIM // 2
    inv = ROPE_THETA ** (-jnp.arange(half, dtype=F32) / half)
    ang = pos * inv[None, :]
    cr = jnp.concatenate([jnp.cos(ang), jnp.cos(ang)], axis=1)
    sr = jnp.concatenate([-jnp.sin(ang), jnp.sin(ang)], axis=1)
    half = WIN_DIM // 2
    inv = ROPE_THETA ** (-jnp.arange(half, dtype=F32) / half)
    ang = pos * inv[None, :]
    c, s, z = jnp.cos(ang), jnp.sin(ang), jnp.zeros_like(ang)
    cw = jnp.concatenate([c, c, c, c], axis=1)
    slo = jnp.concatenate([-s, z, -s, z], axis=1)
    shi = jnp.concatenate([z, s, z, s], axis=1)
    return cr, sr, cw, slo, shi


def _inproj(x2, w_bf16, tables, seq, tm):
    T = x2.shape[0]
    nseq = seq // tm
    row = lambda i: (i, 0)
    tab = lambda i: (i % nseq, 0)
    widths = (RET_WIDTH, None, RET_WIDTH, RET_WIDTH, WIN_WIDTH, 2 * WIN_KV, 2 * WIN_KV)
    return pl.pallas_call(
        _inproj_kernel,
        grid=(T // tm,),
        in_specs=[pl.BlockSpec((tm, D_MODEL), row),
                  pl.BlockSpec((D_MODEL, IN_EVEN), lambda i: (0, 0))]
                 + [pl.BlockSpec((tm, LANES), tab)] * 5,
        out_specs=[pl.BlockSpec((tm, w), row) if w else pl.BlockSpec((RET_WIDTH, tm), lambda i: (0, i))
                   for w in widths],
        out_shape=[jax.ShapeDtypeStruct((T, w) if w else (RET_WIDTH, T), BF16) for w in widths],
        compiler_params=_params("arbitrary"),
        name="inproj_rope",
    )(x2, w_bf16, *tables)


def _retention_kernel(logit_ref, q_ref, kt_ref, v_ref, g_ref, gain_ref, o_ref,
                      dmat_scr, kv_scr, state_scr):
    h = pl.program_id(1)
    C = RET_BLOCK
    dk = RET_DIM
    n_chunks = q_ref.shape[1] // C

    def log_gamma(d):
        v = jnp.full((1, 1), logit_ref[d, h], F32)
        return -(jnp.maximum(-v, 0.0) + jnp.log(1.0 + jnp.exp(-jnp.abs(v))))

    lgf, lgb = log_gamma(0), log_gamma(1)
    i = lax.broadcasted_iota(I32, (C, C), 0)
    j = lax.broadcasted_iota(I32, (C, C), 1)
    diff = (i - j).astype(F32)
    dmat_scr[...] = jnp.where(diff >= 0, jnp.exp(lgf * jnp.maximum(diff, 0.0)),
                              jnp.exp(lgb * jnp.maximum(-diff, 0.0)))
    col = lax.broadcasted_iota(I32, (C, 1), 0).astype(F32)
    lane = lax.broadcasted_iota(I32, (1, C), 1).astype(F32)
    xi_f, xi_b = jnp.exp(lgf * (col + 1.0)), jnp.exp(lgb * (C - col))
    zeta_f, zeta_b = jnp.exp(lgf * (C - 1.0 - lane)), jnp.exp(lgb * lane)
    dec_f, dec_b = jnp.exp(lgf * C), jnp.exp(lgb * C)
    gain = gain_ref[...]

    def span(n):
        return pl.ds(pl.multiple_of(n * C, C), C)

    def kv_pass(n, carry):
        kt = kt_ref[:, span(n)].astype(F32)
        lhs = jnp.concatenate([(kt * zeta_f).astype(BF16), (kt * zeta_b).astype(BF16)], axis=0)
        kv_scr[n] = jnp.dot(lhs, v_ref[0, span(n), :], preferred_element_type=F32)
        return carry

    lax.fori_loop(0, n_chunks, kv_pass, 0, unroll=4)

    def scan_f(n, state):
        state_scr[n, :dk, :] = state.astype(BF16)
        return state * dec_f + kv_scr[n, :dk, :]

    def scan_b(t, state):
        n = n_chunks - 1 - t
        state_scr[n, dk:, :] = state.astype(BF16)
        return state * dec_b + kv_scr[n, dk:, :]

    zero = jnp.zeros((dk, dk), F32)
    lax.fori_loop(0, n_chunks, scan_f, zero)
    lax.fori_loop(0, n_chunks, scan_b, zero)

    def out_pass(n, carry):
        q = q_ref[0, span(n), :]
        s = jnp.dot(q, kt_ref[:, span(n)], preferred_element_type=F32)
        p = (s * dmat_scr[...]).astype(BF16)
        qf = q.astype(F32)
        qx = jnp.concatenate([(qf * xi_f).astype(BF16), (qf * xi_b).astype(BF16)], axis=1)
        y = (jnp.dot(p, v_ref[0, span(n), :], preferred_element_type=F32)
             + jnp.dot(qx, state_scr[n], preferred_element_type=F32))
        mu = jnp.mean(y, axis=-1, keepdims=True)
        yc = y - mu
        var = jnp.mean(yc * yc, axis=-1, keepdims=True)
        yn = yc * lax.rsqrt(var + GN_EPS) * gain
        o_ref[0, span(n), :] = (g_ref[0, span(n), :].astype(F32) * yn).astype(BF16)
        return carry

    lax.fori_loop(0, n_chunks, out_pass, 0, unroll=4)


def _retention(qa, ka_t, va, ga, decay_logit, gn_gain):
    B, S, _ = qa.shape
    n_chunks = S // RET_BLOCK
    head = pl.BlockSpec((1, S, RET_DIM), lambda b, h: (b, 0, h))
    return pl.pallas_call(
        _retention_kernel,
        grid=(B, RET_HEADS),
        in_specs=[pl.BlockSpec(memory_space=pltpu.SMEM), head,
                  pl.BlockSpec((RET_DIM, S), lambda b, h: (h, b)), head, head,
                  pl.BlockSpec((1, RET_DIM), lambda b, h: (0, h))],
        out_specs=head,
        out_shape=jax.ShapeDtypeStruct((B, S, RET_WIDTH), BF16),
        scratch_shapes=[pltpu.VMEM((RET_BLOCK, RET_BLOCK), F32),
                        pltpu.VMEM((n_chunks, 2 * RET_DIM, RET_DIM), F32),
                        pltpu.VMEM((n_chunks, 2 * RET_DIM, RET_DIM), BF16)],
        compiler_params=_params("arbitrary", "arbitrary"),
        name="retention_gn_gate",
    )(decay_logit.astype(F32), qa, ka_t, va, ga, gn_gain.reshape(1, RET_WIDTH).astype(F32))


def _winattn_kernel(sink_ref, q_ref, kp_ref, kc_ref, kn_ref, vp_ref, vc_ref, vn_ref, o_ref):
    step = pl.program_id(1)
    n_steps = pl.num_programs(1)
    W = WINDOW
    nq = q_ref.shape[1] // W
    group = WIN_Q_HEADS // WIN_KV_HEADS
    k_all = jnp.concatenate([kp_ref[0], kc_ref[0], kn_ref[0]], axis=0)
    v_all = jnp.concatenate([vp_ref[0], vc_ref[0], vn_ref[0]], axis=0)
    lo_half = lax.broadcasted_iota(I32, (1, LANES), 1) < WIN_DIM

    def placements(slab, g):
        first, second = slab[:, :LANES], slab[:, LANES:]
        zero = jnp.zeros_like(first)
        if g == 0:
            return jnp.where(lo_half, first, zero), jnp.where(lo_half, zero, second)
        return jnp.where(lo_half, second, zero), jnp.where(lo_half, zero, first)

    rows2 = lax.broadcasted_iota(I32, (2 * W, 1), 0)
    qi = lax.broadcasted_iota(I32, (2 * W, 3 * W), 0) & (W - 1)
    cj = lax.broadcasted_iota(I32, (2 * W, 3 * W), 1)
    rel = cj - qi
    band = (rel >= 0) & (rel <= 2 * W)

    for g in range(WIN_KV_HEADS):
        k_even, k_odd = placements(k_all, g)
        v_even, v_odd = placements(v_all, g)
        c0 = group * WIN_DIM * g
        sink_even = jnp.where(rows2 < W, sink_ref[0, group * g], sink_ref[0, group * g + 2])
        sink_odd = jnp.where(rows2 < W, sink_ref[0, group * g + 1], sink_ref[0, group * g + 3])
        for jq in range(nq):
            qrows = slice(jq * W, (jq + 1) * W)
            krows = slice(jq * W, (jq + 3) * W)
            mask = band
            if jq == 0:
                mask = mask & (cj >= jnp.where(step == 0, W, 0))
            if jq == nq - 1:
                mask = mask & (cj < jnp.where(step == n_steps - 1, 2 * W, 3 * W))
            q2 = jnp.concatenate([q_ref[0, qrows, c0:c0 + LANES],
                                  q_ref[0, qrows, c0 + LANES:c0 + 2 * LANES]], axis=0)

            def softmax_parts(k_placed, sink):
                s = lax.dot_general(q2, k_placed[krows], _NT, preferred_element_type=F32)
                s = jnp.where(mask, s, NEG_MASK)
                m = jnp.maximum(jnp.max(s, axis=1, keepdims=True), sink)
                e = jnp.exp(s - m)
                return e.astype(BF16), jnp.sum(e, axis=1, keepdims=True) + jnp.exp(sink - m)

            p_even, den_even = softmax_parts(k_even, sink_even)
            p_odd, den_odd = softmax_parts(k_odd, sink_odd)
            o = (jnp.dot(p_even, v_even[krows], preferred_element_type=F32)
                 + jnp.dot(p_odd, v_odd[krows], preferred_element_type=F32))
            o = (o / jnp.where(lo_half, den_even, den_odd)).astype(BF16)
            o_ref[0, qrows, c0:c0 + LANES] = o[:W]
            o_ref[0, qrows, c0 + LANES:c0 + 2 * LANES] = o[W:]


def _winattn(qb, kb2, vb2, sink_logit, blocks_per_step):
    B, S, _ = qb.shape
    nq = blocks_per_step
    nb = S // WINDOW
    prev = pl.BlockSpec((1, WINDOW, 2 * WIN_KV), lambda b, s: (b, jnp.maximum(s * nq - 1, 0), 0))
    cur = pl.BlockSpec((1, nq * WINDOW, 2 * WIN_KV), lambda b, s: (b, s, 0))
    nxt = pl.BlockSpec((1, WINDOW, 2 * WIN_KV), lambda b, s: (b, jnp.minimum((s + 1) * nq, nb - 1), 0))
    qspec = pl.BlockSpec((1, nq * WINDOW, WIN_WIDTH), lambda b, s: (b, s, 0))
    return pl.pallas_call(
        _winattn_kernel,
        grid=(B, nb // nq),
        in_specs=[pl.BlockSpec(memory_space=pltpu.SMEM), qspec, prev, cur, nxt, prev, cur, nxt],
        out_specs=qspec,
        out_shape=jax.ShapeDtypeStruct((B, S, WIN_WIDTH), BF16),
        compiler_params=_params("arbitrary", "arbitrary"),
        name="window_attention",
    )(sink_logit.reshape(1, WIN_Q_HEADS).astype(F32), qb, kb2, kb2, kb2, vb2, vb2, vb2)


def _chan_dft_kernel(xa_ref, xb_ref, m_ref, wr_ref, wi_ref):
    n1 = xa_ref.shape[1] // FFT_RADIX2
    for s in range(FFT_RADIX2):
        rows = pl.ds(s, n1, stride=FFT_RADIX2)
        w = (jnp.dot(xa_ref[0, rows, :].astype(BF16), m_ref[:LANES, :], preferred_element_type=F32)
             + jnp.dot(xb_ref[0, rows, :].astype(BF16), m_ref[LANES:, :], preferred_element_type=F32))
        wr_ref[0, s] = w[:, :GROUP_CH].astype(BF16)
        wi_ref[0, s] = w[:, GROUP_CH:].astype(BF16)


def _real_dft16(ar, ai):
    n = FFT_RADIX2
    cs = [math.cos(2 * math.pi * m / n) for m in range(n)]
    sn = [math.sin(2 * math.pi * m / n) for m in range(n)]

    def axpy(acc, coef, v):
        if abs(coef) < 1e-12:
            return acc
        if abs(abs(coef) - 1.0) < 1e-12:
            if acc is None:
                return v if coef > 0 else -v
            return acc + v if coef > 0 else acc - v
        t = coef * v
        return t if acc is None else acc + t

    p = {s: ar[s] + ar[n - s] for s in range(1, n // 2)}
    d = {s: ai[s] - ai[n - s] for s in range(1, n // 2)}
    base = (ar[0] + ar[n // 2], ar[0] - ar[n // 2])
    y = [None] * n
    for k in range(n // 2 + 1):
        e = base[k % 2]
        for s in range(1, n // 2):
            e = axpy(e, cs[(s * k) % n], p[s])
        if k in (0, n // 2):
            y[k] = e
            continue
        o = None
        for s in range(1, n // 2):
            o = axpy(o, sn[(s * k) % n], d[s])
        y[k] = e + o
        y[n - k] = e - o
    return y


def _seq_dft_kernel(wr_ref, wi_ref, cf_ref, sf_ref, y_ref, ar_scr, ai_scr):
    n1 = wr_ref.shape[2]
    for s in range(FFT_RADIX2):
        cf, sf, wr, wi = cf_ref[s], sf_ref[s], wr_ref[0, s], wi_ref[0, s]
        ar_scr[s] = (jnp.dot(cf, wr, preferred_element_type=F32)
                     + jnp.dot(sf, wi, preferred_element_type=F32))
        ai_scr[s] = (jnp.dot(cf, wi, preferred_element_type=F32)
                     - jnp.dot(sf, wr, preferred_element_type=F32))

    def rows8(r, carry):
        rows = pl.ds(pl.multiple_of(r * 8, 8), 8)
        y = _real_dft16([ar_scr[s, rows, :] for s in range(FFT_RADIX2)],
                        [ai_scr[s, rows, :] for s in range(FFT_RADIX2)])
        for k in range(FFT_RADIX2):
            y_ref[0, k, rows, :] = y[k].astype(BF16)
        return carry

    lax.fori_loop(0, n1 // 8, rows8, 0)


def _fourier_tables(seq):
    n1 = seq // FFT_RADIX2
    c = jnp.arange(GROUP_CH, dtype=I32)
    m = (c[:, None] * c[None, :]) % GROUP_CH
    ang = m.astype(F32) * (2.0 * math.pi / GROUP_CH)
    scale = (seq * GROUP_CH) ** -0.5
    chan = (jnp.concatenate([jnp.cos(ang), -jnp.sin(ang)], axis=1) * scale).astype(BF16)
    k1 = jnp.arange(n1, dtype=I32)[None, :, None]
    s1 = jnp.arange(n1, dtype=I32)[None, None, :]
    s2 = jnp.arange(FFT_RADIX2, dtype=I32)[:, None, None]
    m = (FFT_RADIX2 * s1 * k1 + s2 * k1) % seq
    ang = m.astype(F32) * (2.0 * math.pi / seq)
    return chan, jnp.cos(ang).astype(BF16), jnp.sin(ang).astype(BF16)


def _fourier(x, tables, tn=256):
    B, S, _ = x.shape
    n1 = S // FFT_RADIX2
    chan, cf, sf = tables
    wshape = jax.ShapeDtypeStruct((B, FFT_RADIX2, n1, D_MODEL), BF16)
    wblk = pl.BlockSpec((1, FFT_RADIX2, n1, GROUP_CH), lambda b, g: (b, 0, 0, g))
    wr, wi = pl.pallas_call(
        _chan_dft_kernel,
        grid=(B, FOURIER_GROUPS),
        in_specs=[pl.BlockSpec((1, S, LANES), lambda b, g: (b, 0, 2 * g)),
                  pl.BlockSpec((1, S, LANES), lambda b, g: (b, 0, 2 * g + 1)),
                  pl.BlockSpec((GROUP_CH, 2 * GROUP_CH), lambda b, g: (0, 0))],
        out_specs=[wblk, wblk],
        out_shape=[wshape, wshape],
        compiler_params=_params("arbitrary", "arbitrary"),
        name="fourier_channel_dft",
    )(x, x, chan)
    cols = pl.BlockSpec((1, FFT_RADIX2, n1, tn), lambda b, j: (b, 0, 0, j))
    full = pl.BlockSpec((FFT_RADIX2, n1, n1), lambda b, j: (0, 0, 0))
    y = pl.pallas_call(
        _seq_dft_kernel,
        grid=(B, D_MODEL // tn),
        in_specs=[cols, cols, full, full],
        out_specs=cols,
        out_shape=wshape,
        scratch_shapes=[pltpu.VMEM((FFT_RADIX2, n1, tn), F32), pltpu.VMEM((FFT_RADIX2, n1, tn), F32)],
        compiler_params=_params("arbitrary", "arbitrary"),
        name="fourier_sequence_dft",
    )(wr, wi, cf, sf)
    return y.reshape(B * S, D_MODEL)


def _proj_ln_route_kernel(a_ref, b_ref, x_ref, w_ref, gain_ref, bias_ref, wr_ref, br_ref,
                          xo_ref, ri_ref, rg_ref, cnt_ref, carry_scr):
    tm = x_ref.shape[0]
    half = a_ref.shape[1]

    @pl.when(pl.program_id(0) == 0)
    def _():
        carry_scr[...] = jnp.zeros_like(carry_scr)

    mix = (jnp.dot(a_ref[...], w_ref[:half, :], preferred_element_type=F32)
           + jnp.dot(b_ref[...], w_ref[half:, :], preferred_element_type=F32))
    y = _layer_norm(ALPHA * x_ref[...] + mix, gain_ref[...], bias_ref[...])
    xo_ref[...] = y

    logits = jnp.dot(y.astype(BF16), wr_ref[...], preferred_element_type=F32) + br_ref[...]
    lane = lax.broadcasted_iota(I32, (tm, LANES), 1)
    lanef = lane.astype(F32)
    ninf = -jnp.inf
    big = 1e9

    def rmax(v):
        return jnp.max(v, axis=1, keepdims=True)

    def first_lane(hit):
        return jnp.min(jnp.where(hit, lanef, big), axis=1, keepdims=True)

    cmask = (lane >= N_EXPERTS) & (lane < N_EXPERTS + N_GROUPS)
    cl = jnp.where(cmask, logits, ninf)
    cmax = rmax(cl)
    group = first_lane(cl == cmax) - float(N_EXPERTS)
    p_group = 1.0 / jnp.sum(jnp.where(cmask, jnp.exp(cl - cmax), 0.0), axis=1, keepdims=True)
    lo = group * float(EXPERTS_PER_GROUP)
    fmask = (lanef >= lo) & (lanef < lo + float(EXPERTS_PER_GROUP))
    fl = jnp.where(fmask, logits, ninf)
    v1 = rmax(fl)
    e1 = first_lane(fl == v1)
    fl2 = jnp.where(lanef == e1, ninf, fl)
    v2 = rmax(fl2)
    e2 = first_lane(fl2 == v2)
    t = jnp.exp(v2 - v1)
    g1 = p_group / (1.0 + t)
    g2 = p_group * t / (1.0 + t)

    hit1, hit2 = lanef == e1, lanef == e2
    onehot = jnp.where(hit1 | hit2, 1.0, 0.0)
    r = lax.broadcasted_iota(I32, (tm, tm), 0)
    cidx = lax.broadcasted_iota(I32, (tm, tm), 1)
    lower = jnp.where(r > cidx, 1.0, 0.0).astype(BF16)
    prefix = jnp.dot(lower, onehot.astype(BF16), preferred_element_type=F32) + carry_scr[...]
    rank1 = jnp.sum(jnp.where(hit1, prefix, 0.0), axis=1, keepdims=True)
    rank2 = jnp.sum(jnp.where(hit2, prefix, 0.0), axis=1, keepdims=True)
    carry_scr[...] += jnp.sum(onehot, axis=0, keepdims=True)
    cnt_ref[...] = carry_scr[...]
    ri_ref[...] = jnp.where(lane == 0, e1, jnp.where(lane == 1, e2,
                            jnp.where(lane == 2, rank1, rank2))).astype(I32)
    rg_ref[...] = jnp.where(lane == 0, g1, g2)


def _proj_ln_route(a, b, x2, w_bf16, gain, bias, w_router, b_router, tm):
    T = x2.shape[0]
    half = D_MODEL // 2
    row = lambda i: (i, 0)
    const = lambda i: (0, 0)
    a_spec = pl.BlockSpec((tm, half), row)
    b_spec = pl.BlockSpec((tm, half), row if b is not a else (lambda i: (i, 1)))
    return pl.pallas_call(
        _proj_ln_route_kernel,
        grid=(T // tm,),
        in_specs=[a_spec, b_spec, pl.BlockSpec((tm, D_MODEL), row),
                  pl.BlockSpec((D_MODEL, D_MODEL), const),
                  pl.BlockSpec((1, D_MODEL), const), pl.BlockSpec((1, D_MODEL), const),
                  pl.BlockSpec((D_MODEL, LANES), const), pl.BlockSpec((1, LANES), const)],
        out_specs=[pl.BlockSpec((tm, D_MODEL), row), pl.BlockSpec((tm, LANES), row),
                   pl.BlockSpec((tm, LANES), row), pl.BlockSpec((1, LANES), const)],
        out_shape=[jax.ShapeDtypeStruct((T, D_MODEL), F32), jax.ShapeDtypeStruct((T, LANES), I32),
                   jax.ShapeDtypeStruct((T, LANES), F32), jax.ShapeDtypeStruct((1, LANES), F32)],
        scratch_shapes=[pltpu.VMEM((1, LANES), F32)],
        compiler_params=_params("arbitrary"),
        name="proj_ln_router",
    )(a, b, x2, w_bf16, gain, bias, w_router, b_router)


def _expert_kernel(te_ref, nv_ref, src_cur, src_nxt, dst_prv, dst_cur, x_hbm, wg_ref, wu_ref, wd_ref,
                   y_hbm, wg_scr, wu_scr, wd_scr, xbuf0, xbuf1, obuf0, obuf1, gsem, ssem):
    i = pl.program_id(0)
    nv = nv_ref[0]
    tile = xbuf0.shape[0]
    xbufs, obufs = (xbuf0, xbuf1), (obuf0, obuf1)

    def gather_row(idx_ref, r, p):
        return pltpu.make_async_copy(x_hbm.at[pl.ds(idx_ref[0, 0, r], 1), :],
                                     xbufs[p].at[pl.ds(r, 1), :], gsem.at[p])

    def scatter_row(idx_ref, r, p):
        return pltpu.make_async_copy(obufs[p].at[pl.ds(r, 1), :],
                                     y_hbm.at[pl.ds(idx_ref[0, 0, r], 1), :], ssem.at[p])

    def wait_gather(p):
        pltpu.make_async_copy(x_hbm.at[pl.ds(0, tile), :], xbufs[p], gsem.at[p]).wait()

    def wait_scatter(p):
        pltpu.make_async_copy(obufs[p], y_hbm.at[pl.ds(0, tile), :], ssem.at[p]).wait()

    @pl.when(i == 0)
    def _():
        obuf1[...] = jnp.zeros_like(obuf1)

        def first(r, carry):
            gather_row(src_cur, r, 0).start()
            return carry

        lax.fori_loop(0, tile, first, 0, unroll=ROW_DMA_UNROLL)

    @pl.when(i < nv)
    def _():
        @pl.when((i == 0) | (te_ref[i] != te_ref[jnp.maximum(i - 1, 0)]))
        def _():
            wg_scr[...] = wg_ref[0, 0].astype(BF16)
            wu_scr[...] = wu_ref[0, 0].astype(BF16)
            wd_scr[...] = wd_ref[0, 0].astype(BF16)

        for p in range(2):
            @pl.when(i % 2 == p)
            def _():
                wait_gather(p)

                @pl.when(i >= 1)
                def _():
                    wait_scatter(p)

                for r in range(tile):
                    gather_row(src_nxt, r, 1 - p).start(priority=0)
                    scatter_row(dst_prv, r, 1 - p).start(priority=1)
                xb = xbufs[p][...].astype(BF16)
                hg = jnp.dot(xb, wg_scr[...], preferred_element_type=F32)
                hu = jnp.dot(xb, wu_scr[...], preferred_element_type=F32)
                hid = (_silu(hg) * hu).astype(BF16)
                obufs[p][...] = jnp.dot(hid, wd_scr[...], preferred_element_type=F32)

                @pl.when(i == nv - 1)
                def _():
                    wait_gather(1 - p)
                    wait_scatter(1 - p)

                    def last(r, carry):
                        scatter_row(dst_cur, r, p).start()
                        return carry

                    lax.fori_loop(0, tile, last, 0, unroll=ROW_DMA_UNROLL)
                    wait_scatter(p)


def _experts(x1, src_rows, dst_rows, tile_expert, n_valid, w_gate, w_up, w_down, layer, tile, n_out):
    n_tiles = src_rows.shape[0] // tile
    src3 = src_rows.reshape(n_tiles, 1, tile)
    spare = n_out - tile + jnp.arange(tile, dtype=I32)
    dst3 = jnp.concatenate([spare, dst_rows]).reshape(n_tiles + 1, 1, tile)
    wsel = lambda i, te, nv: (layer, te[i], 0, 0)
    smem = lambda fn: pl.BlockSpec((1, 1, tile), fn, memory_space=pltpu.SMEM)
    return pl.pallas_call(
        _expert_kernel,
        grid_spec=pltpu.PrefetchScalarGridSpec(
            num_scalar_prefetch=2,
            grid=(n_tiles,),
            in_specs=[smem(lambda i, te, nv: (i, 0, 0)),
                      smem(lambda i, te, nv: (jnp.minimum(i + 1, n_tiles - 1), 0, 0)),
                      smem(lambda i, te, nv: (i, 0, 0)),
                      smem(lambda i, te, nv: (i + 1, 0, 0)),
                      pl.BlockSpec(memory_space=pl.ANY),
                      pl.BlockSpec((1, 1, D_MODEL, EXPERT_HIDDEN), wsel),
                      pl.BlockSpec((1, 1, D_MODEL, EXPERT_HIDDEN), wsel),
                      pl.BlockSpec((1, 1, EXPERT_HIDDEN, D_MODEL), wsel)],
            out_specs=pl.BlockSpec(memory_space=pl.ANY),
            scratch_shapes=[pltpu.VMEM((D_MODEL, EXPERT_HIDDEN), BF16),
                            pltpu.VMEM((D_MODEL, EXPERT_HIDDEN), BF16),
                            pltpu.VMEM((EXPERT_HIDDEN, D_MODEL), BF16),
                            pltpu.VMEM((tile, D_MODEL), F32), pltpu.VMEM((tile, D_MODEL), F32),
                            pltpu.VMEM((tile, D_MODEL), F32), pltpu.VMEM((tile, D_MODEL), F32),
                            pltpu.SemaphoreType.DMA((2,)), pltpu.SemaphoreType.DMA((2,))],
        ),
        out_shape=jax.ShapeDtypeStruct((n_out, D_MODEL), F32),
        compiler_params=_params("arbitrary"),
        name="moe_experts",
    )(tile_expert, n_valid, src3, src3, dst3, dst3, x1, w_gate, w_up, w_down)


def _combine_ln_kernel(x_ref, y0_ref, y1_ref, g_ref, gain_ref, bias_ref, o_ref):
    g = g_ref[...]
    ffn = y0_ref[...] * g[:, 0:1] + y1_ref[...] * g[:, 1:2]
    o_ref[...] = _layer_norm(ALPHA * x_ref[...] + ffn, gain_ref[...], bias_ref[...])


def _combine_ln(x1, y2, gates, gain, bias, tm):
    T = x1.shape[0]
    row = lambda i: (i, 0)
    const = lambda i: (0, 0)
    return pl.pallas_call(
        _combine_ln_kernel,
        grid=(T // tm,),
        in_specs=[pl.BlockSpec((tm, D_MODEL), row), pl.BlockSpec((tm, D_MODEL), row),
                  pl.BlockSpec((tm, D_MODEL), lambda i: (T // tm + i, 0)),
                  pl.BlockSpec((tm, LANES), row),
                  pl.BlockSpec((1, D_MODEL), const), pl.BlockSpec((1, D_MODEL), const)],
        out_specs=pl.BlockSpec((tm, D_MODEL), row),
        out_shape=jax.ShapeDtypeStruct((T, D_MODEL), F32),
        compiler_params=_params("arbitrary"),
        name="moe_combine_ln",
    )(x1, y2, y2, gates, gain, bias)


def _moe(x1, route_i, route_g, counts_f, w_gate, w_up, w_down, layer, gain, bias, tile, tm):
    T = x1.shape[0]
    expert = route_i[:, 0:2]
    rank = route_i[:, 2:4]
    counts = counts_f[0, :N_EXPERTS].astype(I32)
    padded = (counts + tile - 1) // tile * tile
    ends = jnp.cumsum(padded)
    starts = ends - padded
    eids = jnp.arange(N_EXPERTS, dtype=I32)
    pos = rank + jnp.sum(jnp.where(expert[..., None] == eids, starts, 0), axis=-1)
    n_tiles = (2 * T) // tile + N_EXPERTS
    n_out = 2 * T + tile
    tok = jnp.arange(T, dtype=I32)[:, None]
    slot = jnp.arange(2, dtype=I32)[None, :]
    flat = pos.reshape(-1)
    src_rows = jnp.zeros((n_tiles * tile,), I32).at[flat].set(
        jnp.broadcast_to(tok, (T, 2)).reshape(-1), unique_indices=True)
    spare = 2 * T + jnp.arange(n_tiles * tile, dtype=I32) % tile
    dst_rows = spare.at[flat].set((slot * T + tok).reshape(-1), unique_indices=True)
    last_expert = jnp.max(jnp.where(counts > 0, eids, 0))
    tile_ids = jnp.arange(n_tiles, dtype=I32)
    tile_expert = jnp.sum((tile_ids[:, None] >= (ends // tile)[None, :]).astype(I32), axis=1)
    tile_expert = jnp.minimum(tile_expert, last_expert).astype(I32)
    n_valid = (ends[-1:] // tile).astype(I32)
    y2 = _experts(x1, src_rows, dst_rows, tile_expert, n_valid, w_gate, w_up, w_down, layer, tile, n_out)
    return _combine_ln(x1, y2, route_g, gain, bias, tm)


def _pick(n, pref):
    t = min(n, pref)
    while n % t:
        t //= 2
    return t


def kernel(x, w_in_even, ret_decay_logit, ret_gn_gain, sink_logit, w_out_even, w_out_fourier,
           ln1_gain, ln1_bias, ln2_gain, ln2_bias, router_coarse_w, router_coarse_b,
           router_fine_w, router_fine_b, expert_w_gate, expert_w_up, expert_w_down):
    B, S, D = x.shape
    assert D == D_MODEL and S % (FFT_RADIX2 * 8) == 0 and S % WINDOW == 0 and S % RET_BLOCK == 0
    T = B * S
    tm = _pick(S, 512)
    tile = _pick(T, 512)
    rope_tabs = _rope_tables(S)
    fourier_tabs = _fourier_tables(S)
    row = lambda v: v.reshape(1, -1).astype(F32)

    x2 = x.reshape(T, D).astype(F32)
    for layer in range(DEPTH):
        if layer % 2 == 0:
            e = layer // 2
            qa, ka_t, va, ga, qb, kb2, vb2 = _inproj(x2, w_in_even[e].astype(BF16), rope_tabs, S, tm)
            shp = lambda v: v.reshape(B, S, v.shape[-1])
            ya = _retention(shp(qa), ka_t, shp(va), shp(ga), ret_decay_logit[e], ret_gn_gain[e])
            yb = _winattn(shp(qb), shp(kb2), shp(vb2), sink_logit[e], _pick(S // WINDOW, 4))
            a, b = ya.reshape(T, RET_WIDTH), yb.reshape(T, WIN_WIDTH)
            w_out = w_out_even[e]
        else:
            a = b = _fourier(x2.reshape(B, S, D), fourier_tabs)
            w_out = w_out_fourier[layer // 2]
        w_router = jnp.zeros((D, LANES), F32)
        w_router = w_router.at[:, :N_EXPERTS].set(router_fine_w[layer])
        w_router = w_router.at[:, N_EXPERTS:N_EXPERTS + N_GROUPS].set(router_coarse_w[layer])
        b_router = jnp.zeros((1, LANES), F32)
        b_router = b_router.at[0, :N_EXPERTS].set(router_fine_b[layer])
        b_router = b_router.at[0, N_EXPERTS:N_EXPERTS + N_GROUPS].set(router_coarse_b[layer])
        x1, route_i, route_g, counts = _proj_ln_route(
            a, b, x2, w_out.astype(BF16), row(ln1_gain[layer]), row(ln1_bias[layer]),
            w_router.astype(BF16), b_router, tm)
        x2 = _moe(x1, route_i, route_g, counts, expert_w_gate, expert_w_up, expert_w_down, layer,
                  row(ln2_gain[layer]), row(ln2_bias[layer]), tile, tm)
    return x2.reshape(B, S, D).astype(x.dtype)
```

```python
import math

import jax
import jax.numpy as jnp
from jax import lax
from jax.experimental import pallas as pl
from jax.experimental.pallas import tpu as pltpu

F32 = jnp.float32
BF16 = jnp.bfloat16
I32 = jnp.int32

D_MODEL = 1024
DEPTH = 4
RET_HEADS = 4
RET_DIM = 128
RET_BLOCK = 256
RET_WIDTH = RET_HEADS * RET_DIM
WIN_Q_HEADS = 8
WIN_KV_HEADS = 2
WIN_DIM = 64
WINDOW = 128
WIN_WIDTH = WIN_Q_HEADS * WIN_DIM
WIN_KV = WIN_KV_HEADS * WIN_DIM
FOURIER_GROUPS = 4
GROUP_CH = D_MODEL // FOURIER_GROUPS
FFT_RADIX2 = 16
ROPE_THETA = 10000.0
N_GROUPS = 4
EXPERTS_PER_GROUP = 8
N_EXPERTS = N_GROUPS * EXPERTS_PER_GROUP
EXPERT_HIDDEN = D_MODEL // 2
LN_EPS = 1e-5
GN_EPS = 1e-6
ALPHA = (2.0 * DEPTH) ** 0.25
IN_EVEN = 2 * RET_WIDTH + 2 * RET_WIDTH + WIN_WIDTH + 2 * WIN_KV

LANES = 128
TOKEN_ROWS = D_MODEL // LANES
VMEM_LIMIT_BYTES = 48 * 1024 * 1024
NEG_MASK = -1e30
ROW_DMA_UNROLL = 8

_NT = (((1,), (1,)), ((), ()))


def _params(*sem):
    return pltpu.CompilerParams(dimension_semantics=sem, vmem_limit_bytes=VMEM_LIMIT_BYTES)


def _silu(v):
    return v / (1.0 + jnp.exp(-v))


def _layer_norm(z, gain, bias):
    mu = jnp.mean(z, axis=-1, keepdims=True)
    zc = z - mu
    var = jnp.mean(zc * zc, axis=-1, keepdims=True)
    return zc * lax.rsqrt(var + LN_EPS) * gain + bias


def _store_token_tiles(ref, v):
    n = v.shape[0]
    for c in range(TOKEN_ROWS):
        ref[pl.ds(c, n, stride=TOKEN_ROWS), :] = v[:, c * LANES:(c + 1) * LANES]


def _load_token_tiles(ref):
    n = ref.shape[0] // TOKEN_ROWS
    return jnp.concatenate([ref[pl.ds(c, n, stride=TOKEN_ROWS), :] for c in range(TOKEN_ROWS)], axis=1)


def _rope128(h, cos, sin_signed):
    return h * cos + pltpu.roll(h, 64, 1) * sin_signed


def _rope64x2(h, cos, sin_lo, sin_hi):
    return h * cos + pltpu.roll(h, 96, 1) * sin_lo + pltpu.roll(h, 32, 1) * sin_hi


def _inproj_kernel(x_ref, w_ref, cr_ref, sr_ref, cw_ref, slo_ref, shi_ref,
                   qa_ref, ka_ref, va_ref, ga_ref, qb_ref, kb_ref, vb_ref):
    xb = x_ref[...].astype(BF16)

    def seg(lo, hi):
        return jnp.dot(xb, w_ref[:, lo:hi], preferred_element_type=F32)

    cr, sr = cr_ref[...], sr_ref[...]
    q = seg(0, RET_WIDTH)
    k = seg(RET_WIDTH, 2 * RET_WIDTH)
    for h in range(RET_HEADS):
        sl = slice(LANES * h, LANES * (h + 1))
        qa_ref[:, sl] = _rope128(q[:, sl], cr, sr).astype(BF16)
        ka_ref[sl, :] = (_rope128(k[:, sl], cr, sr) * RET_DIM ** -0.5).T.astype(BF16)
    va_ref[...] = seg(2 * RET_WIDTH, 3 * RET_WIDTH).astype(BF16)
    ga_ref[...] = _silu(seg(3 * RET_WIDTH, 4 * RET_WIDTH)).astype(BF16)
    cw, slo, shi = cw_ref[...], slo_ref[...], shi_ref[...]
    base = 4 * RET_WIDTH
    q = seg(base, base + WIN_WIDTH)
    for p in range(WIN_WIDTH // LANES):
        sl = slice(LANES * p, LANES * (p + 1))
        qb_ref[:, sl] = (_rope64x2(q[:, sl], cw, slo, shi) * WIN_DIM ** -0.5).astype(BF16)
    kv = seg(base + WIN_WIDTH, base + WIN_WIDTH + 2 * WIN_KV)
    kb = _rope64x2(kv[:, :WIN_KV], cw, slo, shi)
    vb = kv[:, WIN_KV:]
    kb_ref[:, :WIN_KV] = kb.astype(BF16)
    kb_ref[:, WIN_KV:] = pltpu.roll(kb, WIN_DIM, 1).astype(BF16)
    vb_ref[:, :WIN_KV] = vb.astype(BF16)
    vb_ref[:, WIN_KV:] = pltpu.roll(vb, WIN_DIM, 1).astype(BF16)


def _rope_tables(seq):
    pos = jnp.arange(seq, dtype=F32)[:, None]
    half = RET_DIM // 2
    inv = ROPE_THETA ** (-jnp.arange(half, dtype=F32) / half)
    ang = pos * inv[None, :]
    cr = jnp.concatenate([jnp.cos(ang), jnp.cos(ang)], axis=1)
    sr = jnp.concatenate([-jnp.sin(ang), jnp.sin(ang)], axis=1)
    half = WIN_DIM // 2
    inv = ROPE_THETA ** (-jnp.arange(half, dtype=F32) / half)
    ang = pos * inv[None, :]
    c, s, z = jnp.cos(ang), jnp.sin(ang), jnp.zeros_like(ang)
    cw = jnp.concatenate([c, c, c, c], axis=1)
    slo = jnp.concatenate([-s, z, -s, z], axis=1)
    shi = jnp.concatenate([z, s, z, s], axis=1)
    return cr, sr, cw, slo, shi


def _inproj(x2, w_bf16, tables, seq, tm):
    T = x2.shape[0]
    nseq = seq // tm
    row = lambda i: (i, 0)
    tab = lambda i: (i % nseq, 0)
    widths = (RET_WIDTH, None, RET_WIDTH, RET_WIDTH, WIN_WIDTH, 2 * WIN_KV, 2 * WIN_KV)
    return pl.pallas_call(
        _inproj_kernel,
        grid=(T // tm,),
        in_specs=[pl.BlockSpec((tm, D_MODEL), row),
                  pl.BlockSpec((D_MODEL, IN_EVEN), lambda i: (0, 0))]
                 + [pl.BlockSpec((tm, LANES), tab)] * 5,
        out_specs=[pl.BlockSpec((tm, w), row) if w else pl.BlockSpec((RET_WIDTH, tm), lambda i: (0, i))
                   for w in widths],
        out_shape=[jax.ShapeDtypeStruct((T, w) if w else (RET_WIDTH, T), BF16) for w in widths],
        compiler_params=_params("arbitrary"),
        name="inproj_rope",
    )(x2, w_bf16, *tables)


def _retention_kernel(logit_ref, q_ref, kt_ref, v_ref, g_ref, gain_ref, o_ref,
                      dmat_scr, kv_scr, state_scr):
    h = pl.program_id(1)
    C = RET_BLOCK
    dk = RET_DIM
    n_chunks = q_ref.shape[1] // C

    def log_gamma(d):
        v = jnp.full((1, 1), logit_ref[d, h], F32)
        return -(jnp.maximum(-v, 0.0) + jnp.log(1.0 + jnp.exp(-jnp.abs(v))))

    lgf, lgb = log_gamma(0), log_gamma(1)
    i = lax.broadcasted_iota(I32, (C, C), 0)
    j = lax.broadcasted_iota(I32, (C, C), 1)
    diff = (i - j).astype(F32)
    dmat_scr[...] = jnp.where(diff >= 0, jnp.exp(lgf * jnp.maximum(diff, 0.0)),
                              jnp.exp(lgb * jnp.maximum(-diff, 0.0)))
    col = lax.broadcasted_iota(I32, (C, 1), 0).astype(F32)
    lane = lax.broadcasted_iota(I32, (1, C), 1).astype(F32)
    xi_f, xi_b = jnp.exp(lgf * (col + 1.0)), jnp.exp(lgb * (C - col))
    zeta_f, zeta_b = jnp.exp(lgf * (C - 1.0 - lane)), jnp.exp(lgb * lane)
    dec_f, dec_b = jnp.exp(lgf * C), jnp.exp(lgb * C)
    gain = gain_ref[...]

    def span(n):
        return pl.ds(pl.multiple_of(n * C, C), C)

    def kv_pass(n, carry):
        kt = kt_ref[:, span(n)].astype(F32)
        lhs = jnp.concatenate([(kt * zeta_f).astype(BF16), (kt * zeta_b).astype(BF16)], axis=0)
        kv_scr[n] = jnp.dot(lhs, v_ref[0, span(n), :], preferred_element_type=F32)
        return carry

    lax.fori_loop(0, n_chunks, kv_pass, 0, unroll=4)

    def scan_f(n, state):
        state_scr[n, :dk, :] = state.astype(BF16)
        return state * dec_f + kv_scr[n, :dk, :]

    def scan_b(t, state):
        n = n_chunks - 1 - t
        state_scr[n, dk:, :] = state.astype(BF16)
        return state * dec_b + kv_scr[n, dk:, :]

    zero = jnp.zeros((dk, dk), F32)
    lax.fori_loop(0, n_chunks, scan_f, zero)
    lax.fori_loop(0, n_chunks, scan_b, zero)

    def out_pass(n, carry):
        q = q_ref[0, span(n), :]
        s = jnp.dot(q, kt_ref[:, span(n)], preferred_element_type=F32)
        p = (s * dmat_scr[...]).astype(BF16)
        qf = q.astype(F32)
        qx = jnp.concatenate([(qf * xi_f).astype(BF16), (qf * xi_b).astype(BF16)], axis=1)
        y = (jnp.dot(p, v_ref[0, span(n), :], preferred_element_type=F32)
             + jnp.dot(qx, state_scr[n], preferred_element_type=F32))
        mu = jnp.mean(y, axis=-1, keepdims=True)
        yc = y - mu
        var = jnp.mean(yc * yc, axis=-1, keepdims=True)
        yn = yc * lax.rsqrt(var + GN_EPS) * gain
        o_ref[0, span(n), :] = (g_ref[0, span(n), :].astype(F32) * yn).astype(BF16)
        return carry

    lax.fori_loop(0, n_chunks, out_pass, 0, unroll=4)


def _retention(qa, ka_t, va, ga, decay_logit, gn_gain):
    B, S, _ = qa.shape
    n_chunks = S // RET_BLOCK
    head = pl.BlockSpec((1, S, RET_DIM), lambda b, h: (b, 0, h))
    return pl.pallas_call(
        _retention_kernel,
        grid=(B, RET_HEADS),
        in_specs=[pl.BlockSpec(memory_space=pltpu.SMEM), head,
                  pl.BlockSpec((RET_DIM, S), lambda b, h: (h, b)), head, head,
                  pl.BlockSpec((1, RET_DIM), lambda b, h: (0, h))],
        out_specs=head,
        out_shape=jax.ShapeDtypeStruct((B, S, RET_WIDTH), BF16),
        scratch_shapes=[pltpu.VMEM((RET_BLOCK, RET_BLOCK), F32),
                        pltpu.VMEM((n_chunks, 2 * RET_DIM, RET_DIM), F32),
                        pltpu.VMEM((n_chunks, 2 * RET_DIM, RET_DIM), BF16)],
        compiler_params=_params("arbitrary", "arbitrary"),
        name="retention_gn_gate",
    )(decay_logit.astype(F32), qa, ka_t, va, ga, gn_gain.reshape(1, RET_WIDTH).astype(F32))


def _winattn_kernel(sink_ref, q_ref, kp_ref, kc_ref, kn_ref, vp_ref, vc_ref, vn_ref, o_ref):
    step = pl.program_id(1)
    n_steps = pl.num_programs(1)
    W = WINDOW
    nq = q_ref.shape[1] // W
    group = WIN_Q_HEADS // WIN_KV_HEADS
    k_all = jnp.concatenate([kp_ref[0], kc_ref[0], kn_ref[0]], axis=0)
    v_all = jnp.concatenate([vp_ref[0], vc_ref[0], vn_ref[0]], axis=0)
    lo_half = lax.broadcasted_iota(I32, (1, LANES), 1) < WIN_DIM

    def placements(slab, g):
        first, second = slab[:, :LANES], slab[:, LANES:]
        zero = jnp.zeros_like(first)
        if g == 0:
            return jnp.where(lo_half, first, zero), jnp.where(lo_half, zero, second)
        return jnp.where(lo_half, second, zero), jnp.where(lo_half, zero, first)

    rows2 = lax.broadcasted_iota(I32, (2 * W, 1), 0)
    qi = lax.broadcasted_iota(I32, (2 * W, 3 * W), 0) & (W - 1)
    cj = lax.broadcasted_iota(I32, (2 * W, 3 * W), 1)
    rel = cj - qi
    band = (rel >= 0) & (rel <= 2 * W)

    for g in range(WIN_KV_HEADS):
        k_even, k_odd = placements(k_all, g)
        v_even, v_odd = placements(v_all, g)
        c0 = group * WIN_DIM * g
        sink_even = jnp.where(rows2 < W, sink_ref[0, group * g], sink_ref[0, group * g + 2])
        sink_odd = jnp.where(rows2 < W, sink_ref[0, group * g + 1], sink_ref[0, group * g + 3])
        for jq in range(nq):
            qrows = slice(jq * W, (jq + 1) * W)
            krows = slice(jq * W, (jq + 3) * W)
            mask = band
            if jq == 0:
                mask = mask & (cj >= jnp.where(step == 0, W, 0))
            if jq == nq - 1:
                mask = mask & (cj < jnp.where(step == n_steps - 1, 2 * W, 3 * W))
            q2 = jnp.concatenate([q_ref[0, qrows, c0:c0 + LANES],
                                  q_ref[0, qrows, c0 + LANES:c0 + 2 * LANES]], axis=0)

            def softmax_parts(k_placed, sink):
                s = lax.dot_general(q2, k_placed[krows], _NT, preferred_element_type=F32)
                s = jnp.where(mask, s, NEG_MASK)
                m = jnp.maximum(jnp.max(s, axis=1, keepdims=True), sink)
                e = jnp.exp(s - m)
                return e.astype(BF16), jnp.sum(e, axis=1, keepdims=True) + jnp.exp(sink - m)

            p_even, den_even = softmax_parts(k_even, sink_even)
            p_odd, den_odd = softmax_parts(k_odd, sink_odd)
            o = (jnp.dot(p_even, v_even[krows], preferred_element_type=F32)
                 + jnp.dot(p_odd, v_odd[krows], preferred_element_type=F32))
            o = (o / jnp.where(lo_half, den_even, den_odd)).astype(BF16)
            o_ref[0, qrows, c0:c0 + LANES] = o[:W]
            o_ref[0, qrows, c0 + LANES:c0 + 2 * LANES] = o[W:]


def _winattn(qb, kb2, vb2, sink_logit, blocks_per_step):
    B, S, _ = qb.shape
    nq = blocks_per_step
    nb = S // WINDOW
    prev = pl.BlockSpec((1, WINDOW, 2 * WIN_KV), lambda b, s: (b, jnp.maximum(s * nq - 1, 0), 0))
    cur = pl.BlockSpec((1, nq * WINDOW, 2 * WIN_KV), lambda b, s: (b, s, 0))
    nxt = pl.BlockSpec((1, WINDOW, 2 * WIN_KV), lambda b, s: (b, jnp.minimum((s + 1) * nq, nb - 1), 0))
    qspec = pl.BlockSpec((1, nq * WINDOW, WIN_WIDTH), lambda b, s: (b, s, 0))
    return pl.pallas_call(
        _winattn_kernel,
        grid=(B, nb // nq),
        in_specs=[pl.BlockSpec(memory_space=pltpu.SMEM), qspec, prev, cur, nxt, prev, cur, nxt],
        out_specs=qspec,
        out_shape=jax.ShapeDtypeStruct((B, S, WIN_WIDTH), BF16),
        compiler_params=_params("arbitrary", "arbitrary"),
        name="window_attention",
    )(sink_logit.reshape(1, WIN_Q_HEADS).astype(F32), qb, kb2, kb2, kb2, vb2, vb2, vb2)


def _chan_dft_kernel(xa_ref, xb_ref, m_ref, wr_ref, wi_ref):
    n1 = xa_ref.shape[1] // FFT_RADIX2
    for s in range(FFT_RADIX2):
        rows = pl.ds(s, n1, stride=FFT_RADIX2)
        w = (jnp.dot(xa_ref[0, rows, :].astype(BF16), m_ref[:LANES, :], preferred_element_type=F32)
             + jnp.dot(xb_ref[0, rows, :].astype(BF16), m_ref[LANES:, :], preferred_element_type=F32))
        wr_ref[0, s] = w[:, :GROUP_CH].astype(BF16)
        wi_ref[0, s] = w[:, GROUP_CH:].astype(BF16)


def _real_dft16(ar, ai):
    n = FFT_RADIX2
    cs = [math.cos(2 * math.pi * m / n) for m in range(n)]
    sn = [math.sin(2 * math.pi * m / n) for m in range(n)]

    def axpy(acc, coef, v):
        if abs(coef) < 1e-12:
            return acc
        if abs(abs(coef) - 1.0) < 1e-12:
            if acc is None:
                return v if coef > 0 else -v
            return acc + v if coef > 0 else acc - v
        t = coef * v
        return t if acc is None else acc + t

    p = {s: ar[s] + ar[n - s] for s in range(1, n // 2)}
    d = {s: ai[s] - ai[n - s] for s in range(1, n // 2)}
    base = (ar[0] + ar[n // 2], ar[0] - ar[n // 2])
    y = [None] * n
    for k in range(n // 2 + 1):
        e = base[k % 2]
        for s in range(1, n // 2):
            e = axpy(e, cs[(s * k) % n], p[s])
        if k in (0, n // 2):
            y[k] = e
            continue
        o = None
        for s in range(1, n // 2):
            o = axpy(o, sn[(s * k) % n], d[s])
        y[k] = e + o
        y[n - k] = e - o
    return y


def _seq_dft_kernel(wr_ref, wi_ref, cf_ref, sf_ref, y_ref, ar_scr, ai_scr):
    n1 = wr_ref.shape[2]
    for s in range(FFT_RADIX2):
        cf, sf, wr, wi = cf_ref[s], sf_ref[s], wr_ref[0, s], wi_ref[0, s]
        ar_scr[s] = (jnp.dot(cf, wr, preferred_element_type=F32)
                     + jnp.dot(sf, wi, preferred_element_type=F32))
        ai_scr[s] = (jnp.dot(cf, wi, preferred_element_type=F32)
                     - jnp.dot(sf, wr, preferred_element_type=F32))

    def rows8(r, carry):
        rows = pl.ds(pl.multiple_of(r * 8, 8), 8)
        y = _real_dft16([ar_scr[s, rows, :] for s in range(FFT_RADIX2)],
                        [ai_scr[s, rows, :] for s in range(FFT_RADIX2)])
        for k in range(FFT_RADIX2):
            y_ref[0, k, rows, :] = y[k].astype(BF16)
        return carry

    lax.fori_loop(0, n1 // 8, rows8, 0)


def _fourier_tables(seq):
    n1 = seq // FFT_RADIX2
    c = jnp.arange(GROUP_CH, dtype=I32)
    m = (c[:, None] * c[None, :]) % GROUP_CH
    ang = m.astype(F32) * (2.0 * math.pi / GROUP_CH)
    scale = (seq * GROUP_CH) ** -0.5
    chan = (jnp.concatenate([jnp.cos(ang), -jnp.sin(ang)], axis=1) * scale).astype(BF16)
    k1 = jnp.arange(n1, dtype=I32)[None, :, None]
    s1 = jnp.arange(n1, dtype=I32)[None, None, :]
    s2 = jnp.arange(FFT_RADIX2, dtype=I32)[:, None, None]
    m = (FFT_RADIX2 * s1 * k1 + s2 * k1) % seq
    ang = m.astype(F32) * (2.0 * math.pi / seq)
    return chan, jnp.cos(ang).astype(BF16), jnp.sin(ang).astype(BF16)


def _fourier(x, tables, tn=256):
    B, S, _ = x.shape
    n1 = S // FFT_RADIX2
    chan, cf, sf = tables
    wshape = jax.ShapeDtypeStruct((B, FFT_RADIX2, n1, D_MODEL), BF16)
    wblk = pl.BlockSpec((1, FFT_RADIX2, n1, GROUP_CH), lambda b, g: (b, 0, 0, g))
    wr, wi = pl.pallas_call(
        _chan_dft_kernel,
        grid=(B, FOURIER_GROUPS),
        in_specs=[pl.BlockSpec((1, S, LANES), lambda b, g: (b, 0, 2 * g)),
                  pl.BlockSpec((1, S, LANES), lambda b, g: (b, 0, 2 * g + 1)),
                  pl.BlockSpec((GROUP_CH, 2 * GROUP_CH), lambda b, g: (0, 0))],
        out_specs=[wblk, wblk],
        out_shape=[wshape, wshape],
        compiler_params=_params("arbitrary", "arbitrary"),
        name="fourier_channel_dft",
    )(x, x, chan)
    cols = pl.BlockSpec((1, FFT_RADIX2, n1, tn), lambda b, j: (b, 0, 0, j))
    full = pl.BlockSpec((FFT_RADIX2, n1, n1), lambda b, j: (0, 0, 0))
    y = pl.pallas_call(
        _seq_dft_kernel,
        grid=(B, D_MODEL // tn),
        in_specs=[cols, cols, full, full],
        out_specs=cols,
        out_shape=wshape,
        scratch_shapes=[pltpu.VMEM((FFT_RADIX2, n1, tn), F32), pltpu.VMEM((FFT_RADIX2, n1, tn), F32)],
        compiler_params=_params("arbitrary", "arbitrary"),
        name="fourier_sequence_dft",
    )(wr, wi, cf, sf)
    return y.reshape(B * S, D_MODEL)


def _proj_ln_route_kernel(a_ref, b_ref, x_ref, w_ref, gain_ref, bias_ref, wr_ref, br_ref,
                          xo_ref, xt_ref, ri_ref, rg_ref, cnt_ref, carry_scr):
    tm = x_ref.shape[0]
    half = a_ref.shape[1]

    @pl.when(pl.program_id(0) == 0)
    def _():
        carry_scr[...] = jnp.zeros_like(carry_scr)

    mix = (jnp.dot(a_ref[...], w_ref[:half, :], preferred_element_type=F32)
           + jnp.dot(b_ref[...], w_ref[half:, :], preferred_element_type=F32))
    y = _layer_norm(ALPHA * x_ref[...] + mix, gain_ref[...], bias_ref[...])
    xo_ref[...] = y
    _store_token_tiles(xt_ref, y)

    logits = jnp.dot(y.astype(BF16), wr_ref[...], preferred_element_type=F32) + br_ref[...]
    lane = lax.broadcasted_iota(I32, (tm, LANES), 1)
    lanef = lane.astype(F32)
    ninf = -jnp.inf
    big = 1e9

    def rmax(v):
        return jnp.max(v, axis=1, keepdims=True)

    def first_lane(hit):
        return jnp.min(jnp.where(hit, lanef, big), axis=1, keepdims=True)

    cmask = (lane >= N_EXPERTS) & (lane < N_EXPERTS + N_GROUPS)
    cl = jnp.where(cmask, logits, ninf)
    cmax = rmax(cl)
    group = first_lane(cl == cmax) - float(N_EXPERTS)
    p_group = 1.0 / jnp.sum(jnp.where(cmask, jnp.exp(cl - cmax), 0.0), axis=1, keepdims=True)
    lo = group * float(EXPERTS_PER_GROUP)
    fmask = (lanef >= lo) & (lanef < lo + float(EXPERTS_PER_GROUP))
    fl = jnp.where(fmask, logits, ninf)
    v1 = rmax(fl)
    e1 = first_lane(fl == v1)
    fl2 = jnp.where(lanef == e1, ninf, fl)
    v2 = rmax(fl2)
    e2 = first_lane(fl2 == v2)
    t = jnp.exp(v2 - v1)
    g1 = p_group / (1.0 + t)
    g2 = p_group * t / (1.0 + t)

    hit1, hit2 = lanef == e1, lanef == e2
    onehot = jnp.where(hit1 | hit2, 1.0, 0.0)
    r = lax.broadcasted_iota(I32, (tm, tm), 0)
    cidx = lax.broadcasted_iota(I32, (tm, tm), 1)
    lower = jnp.where(r > cidx, 1.0, 0.0).astype(BF16)
    prefix = jnp.dot(lower, onehot.astype(BF16), preferred_element_type=F32) + carry_scr[...]
    rank1 = jnp.sum(jnp.where(hit1, prefix, 0.0), axis=1, keepdims=True)
    rank2 = jnp.sum(jnp.where(hit2, prefix, 0.0), axis=1, keepdims=True)
    carry_scr[...] += jnp.sum(onehot, axis=0, keepdims=True)
    cnt_ref[...] = carry_scr[...]
    ri_ref[...] = jnp.where(lane == 0, e1, jnp.where(lane == 1, e2,
                            jnp.where(lane == 2, rank1, rank2))).astype(I32)
    rg_ref[...] = jnp.where(lane == 0, g1, g2)


def _proj_ln_route(a, b, x2, w_bf16, gain, bias, w_router, b_router, tm):
    T = x2.shape[0]
    half = D_MODEL // 2
    row = lambda i: (i, 0)
    const = lambda i: (0, 0)
    a_spec = pl.BlockSpec((tm, half), row)
    b_spec = pl.BlockSpec((tm, half), row if b is not a else (lambda i: (i, 1)))
    return pl.pallas_call(
        _proj_ln_route_kernel,
        grid=(T // tm,),
        in_specs=[a_spec, b_spec, pl.BlockSpec((tm, D_MODEL), row),
                  pl.BlockSpec((D_MODEL, D_MODEL), const),
                  pl.BlockSpec((1, D_MODEL), const), pl.BlockSpec((1, D_MODEL), const),
                  pl.BlockSpec((D_MODEL, LANES), const), pl.BlockSpec((1, LANES), const)],
        out_specs=[pl.BlockSpec((tm, D_MODEL), row), pl.BlockSpec((tm * TOKEN_ROWS, LANES), row),
                   pl.BlockSpec((tm, LANES), row),
                   pl.BlockSpec((tm, LANES), row), pl.BlockSpec((1, LANES), const)],
        out_shape=[jax.ShapeDtypeStruct((T, D_MODEL), F32),
                   jax.ShapeDtypeStruct((T * TOKEN_ROWS, LANES), F32), jax.ShapeDtypeStruct((T, LANES), I32),
                   jax.ShapeDtypeStruct((T, LANES), F32), jax.ShapeDtypeStruct((1, LANES), F32)],
        scratch_shapes=[pltpu.VMEM((1, LANES), F32)],
        compiler_params=_params("arbitrary"),
        name="proj_ln_router",
    )(a, b, x2, w_bf16, gain, bias, w_router, b_router)


def _expert_kernel(te_ref, base_ref, nv_ref, src_ref, dst_ref, x_hbm, wg_ref, wu_ref, wd_ref,
                   y_hbm, wg_scr, wu_scr, wd_scr, xbuf0, xbuf1, obuf0, obuf1, gsem, ssem):
    i = pl.program_id(0)
    nv = nv_ref[0]
    tile = xbuf0.shape[0] // TOKEN_ROWS
    xbufs, obufs = (xbuf0, xbuf1), (obuf0, obuf1)

    def token_rows(t):
        return pl.ds(pl.multiple_of(t * TOKEN_ROWS, TOKEN_ROWS), TOKEN_ROWS)

    def gather_row(base, r, p):
        return pltpu.make_async_copy(x_hbm.at[token_rows(src_ref[base + r]), :],
                                     xbufs[p].at[token_rows(r), :], gsem.at[p])

    def scatter_row(base, r, p):
        return pltpu.make_async_copy(obufs[p].at[token_rows(r), :],
                                     y_hbm.at[token_rows(dst_ref[base + r]), :], ssem.at[p])

    def wait_gather(p):
        pltpu.make_async_copy(x_hbm.at[pl.ds(0, tile * TOKEN_ROWS), :], xbufs[p], gsem.at[p]).wait()

    def wait_scatter(p):
        pltpu.make_async_copy(obufs[p], y_hbm.at[pl.ds(0, tile * TOKEN_ROWS), :], ssem.at[p]).wait()

    @pl.when(i == 0)
    def _():
        obuf1[...] = jnp.zeros_like(obuf1)

        def first(r, carry):
            gather_row(base_ref[0], r, 0).start()
            return carry

        lax.fori_loop(0, tile, first, 0, unroll=ROW_DMA_UNROLL)

    @pl.when(i < nv)
    def _():
        @pl.when((i == 0) | (te_ref[i] != te_ref[jnp.maximum(i - 1, 0)]))
        def _():
            wg_scr[...] = wg_ref[0, 0].astype(BF16)
            wu_scr[...] = wu_ref[0, 0].astype(BF16)
            wd_scr[...] = wd_ref[0, 0].astype(BF16)

        base_next = base_ref[i + 1]
        pad_base = dst_ref.shape[0] - tile
        base_prev = jnp.where(i >= 1, base_ref[jnp.maximum(i - 1, 0)], pad_base)
        for p in range(2):
            @pl.when(i % 2 == p)
            def _():
                wait_gather(p)

                @pl.when(i >= 1)
                def _():
                    wait_scatter(p)

                for r in range(tile):
                    gather_row(base_next, r, 1 - p).start(priority=0)
                    scatter_row(base_prev, r, 1 - p).start(priority=1)
                xb = _load_token_tiles(xbufs[p]).astype(BF16)
                hg = jnp.dot(xb, wg_scr[...], preferred_element_type=F32)
                hu = jnp.dot(xb, wu_scr[...], preferred_element_type=F32)
                hid = (_silu(hg) * hu).astype(BF16)
                _store_token_tiles(obufs[p], jnp.dot(hid, wd_scr[...], preferred_element_type=F32))

                @pl.when(i == nv - 1)
                def _():
                    wait_gather(1 - p)
                    wait_scatter(1 - p)

                    def last(r, carry):
                        scatter_row(base_ref[i], r, p).start()
                        return carry

                    lax.fori_loop(0, tile, last, 0, unroll=ROW_DMA_UNROLL)
                    wait_scatter(p)


def _experts(x1t, src, dst, tile_expert, tile_base, n_valid, w_gate, w_up, w_down, layer, tile, n_out):
    n_tiles = tile_expert.shape[0]
    wsel = lambda i, te, *_: (layer, te[i], 0, 0)
    buf = pltpu.VMEM((tile * TOKEN_ROWS, LANES), F32)
    return pl.pallas_call(
        _expert_kernel,
        grid_spec=pltpu.PrefetchScalarGridSpec(
            num_scalar_prefetch=5,
            grid=(n_tiles,),
            in_specs=[pl.BlockSpec(memory_space=pl.ANY),
                      pl.BlockSpec((1, 1, D_MODEL, EXPERT_HIDDEN), wsel),
                      pl.BlockSpec((1, 1, D_MODEL, EXPERT_HIDDEN), wsel),
                      pl.BlockSpec((1, 1, EXPERT_HIDDEN, D_MODEL), wsel)],
            out_specs=pl.BlockSpec(memory_space=pl.ANY),
            scratch_shapes=[pltpu.VMEM((D_MODEL, EXPERT_HIDDEN), BF16),
                            pltpu.VMEM((D_MODEL, EXPERT_HIDDEN), BF16),
                            pltpu.VMEM((EXPERT_HIDDEN, D_MODEL), BF16),
                            buf, buf, buf, buf,
                            pltpu.SemaphoreType.DMA((2,)), pltpu.SemaphoreType.DMA((2,))],
        ),
        out_shape=jax.ShapeDtypeStruct((n_out * TOKEN_ROWS, LANES), F32),
        compiler_params=_params("arbitrary"),
        name="moe_experts",
    )(tile_expert, tile_base, n_valid, src, dst, x1t, w_gate, w_up, w_down)


def _combine_ln_kernel(x_ref, y0_ref, y1_ref, g_ref, gain_ref, bias_ref, o_ref):
    g = g_ref[...]
    ffn = _load_token_tiles(y0_ref) * g[:, 0:1] + _load_token_tiles(y1_ref) * g[:, 1:2]
    o_ref[...] = _layer_norm(ALPHA * x_ref[...] + ffn, gain_ref[...], bias_ref[...])


def _combine_ln(x1, y2t, gates, gain, bias, tm):
    T = x1.shape[0]
    row = lambda i: (i, 0)
    const = lambda i: (0, 0)
    return pl.pallas_call(
        _combine_ln_kernel,
        grid=(T // tm,),
        in_specs=[pl.BlockSpec((tm, D_MODEL), row), pl.BlockSpec((tm * TOKEN_ROWS, LANES), row),
                  pl.BlockSpec((tm * TOKEN_ROWS, LANES), lambda i: (T // tm + i, 0)),
                  pl.BlockSpec((tm, LANES), row),
                  pl.BlockSpec((1, D_MODEL), const), pl.BlockSpec((1, D_MODEL), const)],
        out_specs=pl.BlockSpec((tm, D_MODEL), row),
        out_shape=jax.ShapeDtypeStruct((T, D_MODEL), F32),
        compiler_params=_params("arbitrary"),
        name="moe_combine_ln",
    )(x1, y2t, y2t, gates, gain, bias)


def _moe(x1, x1t, route_i, route_g, counts_f, w_gate, w_up, w_down, layer, gain, bias, tile, tm):
    T = x1.shape[0]
    n_pairs = 2 * T
    counts = counts_f[0, :N_EXPERTS].astype(I32)
    eids = jnp.arange(N_EXPERTS, dtype=I32)
    key = route_i[:, 0:2].reshape(-1) * n_pairs + jnp.arange(n_pairs, dtype=I32)
    pair = jnp.sort(key) % n_pairs
    pad = jnp.arange(tile, dtype=I32)
    src = jnp.concatenate([pair // 2, jnp.zeros((tile,), I32)])
    dst = jnp.concatenate([(pair % 2) * T + pair // 2, n_pairs + pad])
    tiles_per = (counts + tile - 1) // tile
    tile_ends = jnp.cumsum(tiles_per)
    first_row = jnp.cumsum(counts) - counts
    n_tiles = n_pairs // tile + N_EXPERTS
    tile_ids = jnp.arange(n_tiles + 1, dtype=I32)
    owner = jnp.sum((tile_ids[:, None] >= tile_ends[None, :]).astype(I32), axis=1)
    valid = owner < N_EXPERTS
    last_expert = jnp.max(jnp.where(counts > 0, eids, 0))
    tile_expert = jnp.minimum(owner, last_expert).astype(I32)
    onehot = owner[:, None] == eids[None, :]
    pick = lambda v: jnp.sum(jnp.where(onehot, v[None, :], 0), axis=1)
    tile_base = pick(first_row) + (tile_ids - pick(tile_ends - tiles_per)) * tile
    tile_base = jnp.where(valid, tile_base, n_pairs).astype(I32)
    n_valid = tile_ends[-1:].astype(I32)
    y2t = _experts(x1t, src, dst, tile_expert[:n_tiles], tile_base, n_valid, w_gate, w_up, w_down,
                   layer, tile, n_pairs + tile)
    return _combine_ln(x1, y2t, route_g, gain, bias, tm)


def _pick(n, pref):
    t = min(n, pref)
    while n % t:
        t //= 2
    return t


def kernel(x, w_in_even, ret_decay_logit, ret_gn_gain, sink_logit, w_out_even, w_out_fourier,
           ln1_gain, ln1_bias, ln2_gain, ln2_bias, router_coarse_w, router_coarse_b,
           router_fine_w, router_fine_b, expert_w_gate, expert_w_up, expert_w_down):
    B, S, D = x.shape
    assert D == D_MODEL and S % (FFT_RADIX2 * 8) == 0 and S % WINDOW == 0 and S % RET_BLOCK == 0
    T = B * S
    tm = _pick(S, 512)
    tile = _pick(T, 512)
    rope_tabs = _rope_tables(S)
    fourier_tabs = _fourier_tables(S)
    row = lambda v: v.reshape(1, -1).astype(F32)

    x2 = x.reshape(T, D).astype(F32)
    for layer in range(DEPTH):
        if layer % 2 == 0:
            e = layer // 2
            qa, ka_t, va, ga, qb, kb2, vb2 = _inproj(x2, w_in_even[e].astype(BF16), rope_tabs, S, tm)
            shp = lambda v: v.reshape(B, S, v.shape[-1])
            ya = _retention(shp(qa), ka_t, shp(va), shp(ga), ret_decay_logit[e], ret_gn_gain[e])
            yb = _winattn(shp(qb), shp(kb2), shp(vb2), sink_logit[e], _pick(S // WINDOW, 4))
            a, b = ya.reshape(T, RET_WIDTH), yb.reshape(T, WIN_WIDTH)
            w_out = w_out_even[e]
        else:
            a = b = _fourier(x2.reshape(B, S, D), fourier_tabs)
            w_out = w_out_fourier[layer // 2]
        w_router = jnp.zeros((D, LANES), F32)
        w_router = w_router.at[:, :N_EXPERTS].set(router_fine_w[layer])
        w_router = w_router.at[:, N_EXPERTS:N_EXPERTS + N_GROUPS].set(router_coarse_w[layer])
        b_router = jnp.zeros((1, LANES), F32)
        b_router = b_router.at[0, :N_EXPERTS].set(router_fine_b[layer])
        b_router = b_router.at[0, N_EXPERTS:N_EXPERTS + N_GROUPS].set(router_coarse_b[layer])
        x1, x1t, route_i, route_g, counts = _proj_ln_route(
            a, b, x2, w_out.astype(BF16), row(ln1_gain[layer]), row(ln1_bias[layer]),
            w_router.astype(BF16), b_router, tm)
        x2 = _moe(x1, x1t, route_i, route_g, counts, expert_w_gate, expert_w_up, expert_w_down, layer,
                  row(ln2_gain[layer]), row(ln2_bias[layer]), tile, tm)
    return x2.reshape(B, S, D).astype(x.dtype)
```

```python
import math

import jax
import jax.numpy as jnp
from jax import lax
from jax.experimental import pallas as pl
from jax.experimental.pallas import tpu as pltpu

F32 = jnp.float32
BF16 = jnp.bfloat16
I32 = jnp.int32

D_MODEL = 1024
DEPTH = 4
RET_HEADS = 4
RET_DIM = 128
RET_BLOCK = 256
RET_WIDTH = RET_HEADS * RET_DIM
WIN_Q_HEADS = 8
WIN_KV_HEADS = 2
WIN_DIM = 64
WINDOW = 128
WIN_WIDTH = WIN_Q_HEADS * WIN_DIM
WIN_KV = WIN_KV_HEADS * WIN_DIM
FOURIER_GROUPS = 4
GROUP_CH = D_MODEL // FOURIER_GROUPS
FFT_RADIX2 = 16
ROPE_THETA = 10000.0
N_GROUPS = 4
EXPERTS_PER_GROUP = 8
N_EXPERTS = N_GROUPS * EXPERTS_PER_GROUP
EXPERT_HIDDEN = D_MODEL // 2
LN_EPS = 1e-5
GN_EPS = 1e-6
ALPHA = (2.0 * DEPTH) ** 0.25
IN_EVEN = 2 * RET_WIDTH + 2 * RET_WIDTH + WIN_WIDTH + 2 * WIN_KV

LANES = 128
TOKEN_ROWS = D_MODEL // LANES
VMEM_LIMIT_BYTES = 48 * 1024 * 1024
NEG_MASK = -1e30
ROW_DMA_UNROLL = 8

_NT = (((1,), (1,)), ((), ()))


def _params(*sem):
    return pltpu.CompilerParams(dimension_semantics=sem, vmem_limit_bytes=VMEM_LIMIT_BYTES)


def _silu(v):
    return v / (1.0 + jnp.exp(-v))


def _layer_norm(z, gain, bias):
    mu = jnp.mean(z, axis=-1, keepdims=True)
    zc = z - mu
    var = jnp.mean(zc * zc, axis=-1, keepdims=True)
    return zc * lax.rsqrt(var + LN_EPS) * gain + bias


def _store_token_tiles(ref, v):
    n = v.shape[0]
    for c in range(TOKEN_ROWS):
        ref[pl.ds(c, n, stride=TOKEN_ROWS), :] = v[:, c * LANES:(c + 1) * LANES]


def _load_token_tiles(ref):
    n = ref.shape[0] // TOKEN_ROWS
    return jnp.concatenate([ref[pl.ds(c, n, stride=TOKEN_ROWS), :] for c in range(TOKEN_ROWS)], axis=1)


def _rope128(h, cos, sin_signed):
    return h * cos + pltpu.roll(h, 64, 1) * sin_signed


def _rope64x2(h, cos, sin_lo, sin_hi):
    return h * cos + pltpu.roll(h, 96, 1) * sin_lo + pltpu.roll(h, 32, 1) * sin_hi


def _inproj_kernel(x_ref, w_ref, cr_ref, sr_ref, cw_ref, slo_ref, shi_ref,
                   qa_ref, ka_ref, va_ref, ga_ref, qb_ref, kb_ref, vb_ref):
    xb = x_ref[...].astype(BF16)

    def seg(lo, hi):
        return jnp.dot(xb, w_ref[:, lo:hi], preferred_element_type=F32)

    cr, sr = cr_ref[...], sr_ref[...]
    q = seg(0, RET_WIDTH)
    k = seg(RET_WIDTH, 2 * RET_WIDTH)
    for h in range(RET_HEADS):
        sl = slice(LANES * h, LANES * (h + 1))
        qa_ref[:, sl] = _rope128(q[:, sl], cr, sr).astype(BF16)
        ka_ref[sl, :] = (_rope128(k[:, sl], cr, sr) * RET_DIM ** -0.5).T.astype(BF16)
    va_ref[...] = seg(2 * RET_WIDTH, 3 * RET_WIDTH).astype(BF16)
    ga_ref[...] = _silu(seg(3 * RET_WIDTH, 4 * RET_WIDTH)).astype(BF16)
    cw, slo, shi = cw_ref[...], slo_ref[...], shi_ref[...]
    base = 4 * RET_WIDTH
    q = seg(base, base + WIN_WIDTH)
    for p in range(WIN_WIDTH // LANES):
        sl = slice(LANES * p, LANES * (p + 1))
        qb_ref[:, sl] = (_rope64x2(q[:, sl], cw, slo, shi) * WIN_DIM ** -0.5).astype(BF16)
    kv = seg(base + WIN_WIDTH, base + WIN_WIDTH + 2 * WIN_KV)
    kb = _rope64x2(kv[:, :WIN_KV], cw, slo, shi)
    vb = kv[:, WIN_KV:]
    kb_ref[:, :WIN_KV] = kb.astype(BF16)
    kb_ref[:, WIN_KV:] = pltpu.roll(kb, WIN_DIM, 1).astype(BF16)
    vb_ref[:, :WIN_KV] = vb.astype(BF16)
    vb_ref[:, WIN_KV:] = pltpu.roll(vb, WIN_DIM, 1).astype(BF16)


def _rope_tables(seq):
    pos = jnp.arange(seq, dtype=F32)[:, None]
    half = RET_DIM // 2
    inv = ROPE_THETA ** (-jnp.arange(half, dtype=F32) / half)
    ang = pos * inv[None, :]
    cr = jnp.concatenate([jnp.cos(ang), jnp.cos(ang)], axis=1)
    sr = jnp.concatenate([-jnp.sin(ang), jnp.sin(ang)], axis=1)
    half = WIN_DIM // 2
    inv = ROPE_THETA ** (-jnp.arange(half, dtype=F32) / half)
    ang = pos * inv[None, :]
    c, s, z = jnp.cos(ang), jnp.sin(ang), jnp.zeros_like(ang)
    cw = jnp.concatenate([c, c, c, c], axis=1)
    slo = jnp.concatenate([-s, z, -s, z], axis=1)
    shi = jnp.concatenate([z, s, z, s], axis=1)
    return cr, sr, cw, slo, shi


def _inproj(x2, w_bf16, tables, seq, tm):
    T = x2.shape[0]
    nseq = seq // tm
    row = lambda i: (i, 0)
    tab = lambda i: (i % nseq, 0)
    widths = (RET_WIDTH, None, RET_WIDTH, RET_WIDTH, WIN_WIDTH, 2 * WIN_KV, 2 * WIN_KV)
    return pl.pallas_call(
        _inproj_kernel,
        grid=(T // tm,),
        in_specs=[pl.BlockSpec((tm, D_MODEL), row),
                  pl.BlockSpec((D_MODEL, IN_EVEN), lambda i: (0, 0))]
                 + [pl.BlockSpec((tm, LANES), tab)] * 5,
        out_specs=[pl.BlockSpec((tm, w), row) if w else pl.BlockSpec((RET_WIDTH, tm), lambda i: (0, i))
                   for w in widths],
        out_shape=[jax.ShapeDtypeStruct((T, w) if w else (RET_WIDTH, T), BF16) for w in widths],
        compiler_params=_params("arbitrary"),
        name="inproj_rope",
    )(x2, w_bf16, *tables)


def _retention_kernel(logit_ref, q_ref, kt_ref, v_ref, g_ref, gain_ref, o_ref,
                      dmat_scr, kv_scr, state_scr):
    h = pl.program_id(1)
    C = RET_BLOCK
    dk = RET_DIM
    n_chunks = q_ref.shape[1] // C

    def log_gamma(d):
        v = jnp.full((1, 1), logit_ref[d, h], F32)
        return -(jnp.maximum(-v, 0.0) + jnp.log(1.0 + jnp.exp(-jnp.abs(v))))

    lgf, lgb = log_gamma(0), log_gamma(1)
    i = lax.broadcasted_iota(I32, (C, C), 0)
    j = lax.broadcasted_iota(I32, (C, C), 1)
    diff = (i - j).astype(F32)
    dmat_scr[...] = jnp.where(diff >= 0, jnp.exp(lgf * jnp.maximum(diff, 0.0)),
                              jnp.exp(lgb * jnp.maximum(-diff, 0.0)))
    col = lax.broadcasted_iota(I32, (C, 1), 0).astype(F32)
    lane = lax.broadcasted_iota(I32, (1, C), 1).astype(F32)
    xi_f, xi_b = jnp.exp(lgf * (col + 1.0)), jnp.exp(lgb * (C - col))
    zeta_f, zeta_b = jnp.exp(lgf * (C - 1.0 - lane)), jnp.exp(lgb * lane)
    dec_f, dec_b = jnp.exp(lgf * C), jnp.exp(lgb * C)
    gain = gain_ref[...]

    def span(n):
        return pl.ds(pl.multiple_of(n * C, C), C)

    def kv_pass(n, carry):
        kt = kt_ref[:, span(n)].astype(F32)
        lhs = jnp.concatenate([(kt * zeta_f).astype(BF16), (kt * zeta_b).astype(BF16)], axis=0)
        kv_scr[n] = jnp.dot(lhs, v_ref[0, span(n), :], preferred_element_type=F32)
        return carry

    lax.fori_loop(0, n_chunks, kv_pass, 0, unroll=4)

    def scan_f(n, state):
        state_scr[n, :dk, :] = state.astype(BF16)
        return state * dec_f + kv_scr[n, :dk, :]

    def scan_b(t, state):
        n = n_chunks - 1 - t
        state_scr[n, dk:, :] = state.astype(BF16)
        return state * dec_b + kv_scr[n, dk:, :]

    zero = jnp.zeros((dk, dk), F32)
    lax.fori_loop(0, n_chunks, scan_f, zero)
    lax.fori_loop(0, n_chunks, scan_b, zero)

    def out_pass(n, carry):
        q = q_ref[0, span(n), :]
        s = jnp.dot(q, kt_ref[:, span(n)], preferred_element_type=F32)
        p = (s * dmat_scr[...]).astype(BF16)
        qf = q.astype(F32)
        qx = jnp.concatenate([(qf * xi_f).astype(BF16), (qf * xi_b).astype(BF16)], axis=1)
        y = (jnp.dot(p, v_ref[0, span(n), :], preferred_element_type=F32)
             + jnp.dot(qx, state_scr[n], preferred_element_type=F32))
        mu = jnp.mean(y, axis=-1, keepdims=True)
        yc = y - mu
        var = jnp.mean(yc * yc, axis=-1, keepdims=True)
        yn = yc * lax.rsqrt(var + GN_EPS) * gain
        o_ref[0, span(n), :] = (g_ref[0, span(n), :].astype(F32) * yn).astype(BF16)
        return carry

    lax.fori_loop(0, n_chunks, out_pass, 0, unroll=4)


def _retention(qa, ka_t, va, ga, decay_logit, gn_gain):
    B, S, _ = qa.shape
    n_chunks = S // RET_BLOCK
    head = pl.BlockSpec((1, S, RET_DIM), lambda b, h: (b, 0, h))
    return pl.pallas_call(
        _retention_kernel,
        grid=(B, RET_HEADS),
        in_specs=[pl.BlockSpec(memory_space=pltpu.SMEM), head,
                  pl.BlockSpec((RET_DIM, S), lambda b, h: (h, b)), head, head,
                  pl.BlockSpec((1, RET_DIM), lambda b, h: (0, h))],
        out_specs=head,
        out_shape=jax.ShapeDtypeStruct((B, S, RET_WIDTH), BF16),
        scratch_shapes=[pltpu.VMEM((RET_BLOCK, RET_BLOCK), F32),
                        pltpu.VMEM((n_chunks, 2 * RET_DIM, RET_DIM), F32),
                        pltpu.VMEM((n_chunks, 2 * RET_DIM, RET_DIM), BF16)],
        compiler_params=_params("arbitrary", "arbitrary"),
        name="retention_gn_gate",
    )(decay_logit.astype(F32), qa, ka_t, va, ga, gn_gain.reshape(1, RET_WIDTH).astype(F32))


def _winattn_kernel(sink_ref, q_ref, kp_ref, kc_ref, kn_ref, vp_ref, vc_ref, vn_ref, o_ref):
    step = pl.program_id(1)
    n_steps = pl.num_programs(1)
    W = WINDOW
    nq = q_ref.shape[1] // W
    group = WIN_Q_HEADS // WIN_KV_HEADS
    k_all = jnp.concatenate([kp_ref[0], kc_ref[0], kn_ref[0]], axis=0)
    v_all = jnp.concatenate([vp_ref[0], vc_ref[0], vn_ref[0]], axis=0)
    lo_half = lax.broadcasted_iota(I32, (1, LANES), 1) < WIN_DIM

    def placements(slab, g):
        first, second = slab[:, :LANES], slab[:, LANES:]
        zero = jnp.zeros_like(first)
        if g == 0:
            return jnp.where(lo_half, first, zero), jnp.where(lo_half, zero, second)
        return jnp.where(lo_half, second, zero), jnp.where(lo_half, zero, first)

    rows2 = lax.broadcasted_iota(I32, (2 * W, 1), 0)
    qi = lax.broadcasted_iota(I32, (2 * W, 3 * W), 0) & (W - 1)
    cj = lax.broadcasted_iota(I32, (2 * W, 3 * W), 1)
    rel = cj - qi
    band = (rel >= 0) & (rel <= 2 * W)

    for g in range(WIN_KV_HEADS):
        k_even, k_odd = placements(k_all, g)
        v_even, v_odd = placements(v_all, g)
        c0 = group * WIN_DIM * g
        sink_even = jnp.where(rows2 < W, sink_ref[0, group * g], sink_ref[0, group * g + 2])
        sink_odd = jnp.where(rows2 < W, sink_ref[0, group * g + 1], sink_ref[0, group * g + 3])
        for jq in range(nq):
            qrows = slice(jq * W, (jq + 1) * W)
            krows = slice(jq * W, (jq + 3) * W)
            mask = band
            if jq == 0:
                mask = mask & (cj >= jnp.where(step == 0, W, 0))
            if jq == nq - 1:
                mask = mask & (cj < jnp.where(step == n_steps - 1, 2 * W, 3 * W))
            q2 = jnp.concatenate([q_ref[0, qrows, c0:c0 + LANES],
                                  q_ref[0, qrows, c0 + LANES:c0 + 2 * LANES]], axis=0)

            def softmax_parts(k_placed, sink):
                s = lax.dot_general(q2, k_placed[krows], _NT, preferred_element_type=F32)
                s = jnp.where(mask, s, NEG_MASK)
                m = jnp.maximum(jnp.max(s, axis=1, keepdims=True), sink)
                e = jnp.exp(s - m)
                return e.astype(BF16), jnp.sum(e, axis=1, keepdims=True) + jnp.exp(sink - m)

            p_even, den_even = softmax_parts(k_even, sink_even)
            p_odd, den_odd = softmax_parts(k_odd, sink_odd)
            o = (jnp.dot(p_even, v_even[krows], preferred_element_type=F32)
                 + jnp.dot(p_odd, v_odd[krows], preferred_element_type=F32))
            o = (o / jnp.where(lo_half, den_even, den_odd)).astype(BF16)
            o_ref[0, qrows, c0:c0 + LANES] = o[:W]
            o_ref[0, qrows, c0 + LANES:c0 + 2 * LANES] = o[W:]


def _winattn(qb, kb2, vb2, sink_logit, blocks_per_step):
    B, S, _ = qb.shape
    nq = blocks_per_step
    nb = S // WINDOW
    prev = pl.BlockSpec((1, WINDOW, 2 * WIN_KV), lambda b, s: (b, jnp.maximum(s * nq - 1, 0), 0))
    cur = pl.BlockSpec((1, nq * WINDOW, 2 * WIN_KV), lambda b, s: (b, s, 0))
    nxt = pl.BlockSpec((1, WINDOW, 2 * WIN_KV), lambda b, s: (b, jnp.minimum((s + 1) * nq, nb - 1), 0))
    qspec = pl.BlockSpec((1, nq * WINDOW, WIN_WIDTH), lambda b, s: (b, s, 0))
    return pl.pallas_call(
        _winattn_kernel,
        grid=(B, nb // nq),
        in_specs=[pl.BlockSpec(memory_space=pltpu.SMEM), qspec, prev, cur, nxt, prev, cur, nxt],
        out_specs=qspec,
        out_shape=jax.ShapeDtypeStruct((B, S, WIN_WIDTH), BF16),
        compiler_params=_params("arbitrary", "arbitrary"),
        name="window_attention",
    )(sink_logit.reshape(1, WIN_Q_HEADS).astype(F32), qb, kb2, kb2, kb2, vb2, vb2, vb2)


def _chan_dft_kernel(xa_ref, xb_ref, m_ref, wr_ref, wi_ref):
    n1 = xa_ref.shape[1] // FFT_RADIX2
    for s in range(FFT_RADIX2):
        rows = pl.ds(s, n1, stride=FFT_RADIX2)
        w = (jnp.dot(xa_ref[0, rows, :].astype(BF16), m_ref[:LANES, :], preferred_element_type=F32)
             + jnp.dot(xb_ref[0, rows, :].astype(BF16), m_ref[LANES:, :], preferred_element_type=F32))
        wr_ref[0, s] = w[:, :GROUP_CH].astype(BF16)
        wi_ref[0, s] = w[:, GROUP_CH:].astype(BF16)


def _real_dft16(ar, ai):
    n = FFT_RADIX2
    cs = [math.cos(2 * math.pi * m / n) for m in range(n)]
    sn = [math.sin(2 * math.pi * m / n) for m in range(n)]

    def axpy(acc, coef, v):
        if abs(coef) < 1e-12:
            return acc
        if abs(abs(coef) - 1.0) < 1e-12:
            if acc is None:
                return v if coef > 0 else -v
            return acc + v if coef > 0 else acc - v
        t = coef * v
        return t if acc is None else acc + t

    p = {s: ar[s] + ar[n - s] for s in range(1, n // 2)}
    d = {s: ai[s] - ai[n - s] for s in range(1, n // 2)}
    base = (ar[0] + ar[n // 2], ar[0] - ar[n // 2])
    y = [None] * n
    for k in range(n // 2 + 1):
        e = base[k % 2]
        for s in range(1, n // 2):
            e = axpy(e, cs[(s * k) % n], p[s])
        if k in (0, n // 2):
            y[k] = e
            continue
        o = None
        for s in range(1, n // 2):
            o = axpy(o, sn[(s * k) % n], d[s])
        y[k] = e + o
        y[n - k] = e - o
    return y


def _seq_dft_kernel(wr_ref, wi_ref, cf_ref, sf_ref, y_ref, ar_scr, ai_scr):
    n1 = wr_ref.shape[2]
    for s in range(FFT_RADIX2):
        cf, sf, wr, wi = cf_ref[s], sf_ref[s], wr_ref[0, s], wi_ref[0, s]
        ar_scr[s] = (jnp.dot(cf, wr, preferred_element_type=F32)
                     + jnp.dot(sf, wi, preferred_element_type=F32))
        ai_scr[s] = (jnp.dot(cf, wi, preferred_element_type=F32)
                     - jnp.dot(sf, wr, preferred_element_type=F32))

    def rows8(r, carry):
        rows = pl.ds(pl.multiple_of(r * 8, 8), 8)
        y = _real_dft16([ar_scr[s, rows, :] for s in range(FFT_RADIX2)],
                        [ai_scr[s, rows, :] for s in range(FFT_RADIX2)])
        for k in range(FFT_RADIX2):
            y_ref[0, k, rows, :] = y[k].astype(BF16)
        return carry

    lax.fori_loop(0, n1 // 8, rows8, 0)


def _fourier_tables(seq):
    n1 = seq // FFT_RADIX2
    c = jnp.arange(GROUP_CH, dtype=I32)
    m = (c[:, None] * c[None, :]) % GROUP_CH
    ang = m.astype(F32) * (2.0 * math.pi / GROUP_CH)
    scale = (seq * GROUP_CH) ** -0.5
    chan = (jnp.concatenate([jnp.cos(ang), -jnp.sin(ang)], axis=1) * scale).astype(BF16)
    k1 = jnp.arange(n1, dtype=I32)[None, :, None]
    s1 = jnp.arange(n1, dtype=I32)[None, None, :]
    s2 = jnp.arange(FFT_RADIX2, dtype=I32)[:, None, None]
    m = (FFT_RADIX2 * s1 * k1 + s2 * k1) % seq
    ang = m.astype(F32) * (2.0 * math.pi / seq)
    return chan, jnp.cos(ang).astype(BF16), jnp.sin(ang).astype(BF16)


def _fourier(x, tables, tn=256):
    B, S, _ = x.shape
    n1 = S // FFT_RADIX2
    chan, cf, sf = tables
    wshape = jax.ShapeDtypeStruct((B, FFT_RADIX2, n1, D_MODEL), BF16)
    wblk = pl.BlockSpec((1, FFT_RADIX2, n1, GROUP_CH), lambda b, g: (b, 0, 0, g))
    wr, wi = pl.pallas_call(
        _chan_dft_kernel,
        grid=(B, FOURIER_GROUPS),
        in_specs=[pl.BlockSpec((1, S, LANES), lambda b, g: (b, 0, 2 * g)),
                  pl.BlockSpec((1, S, LANES), lambda b, g: (b, 0, 2 * g + 1)),
                  pl.BlockSpec((GROUP_CH, 2 * GROUP_CH), lambda b, g: (0, 0))],
        out_specs=[wblk, wblk],
        out_shape=[wshape, wshape],
        compiler_params=_params("arbitrary", "arbitrary"),
        name="fourier_channel_dft",
    )(x, x, chan)
    cols = pl.BlockSpec((1, FFT_RADIX2, n1, tn), lambda b, j: (b, 0, 0, j))
    full = pl.BlockSpec((FFT_RADIX2, n1, n1), lambda b, j: (0, 0, 0))
    y = pl.pallas_call(
        _seq_dft_kernel,
        grid=(B, D_MODEL // tn),
        in_specs=[cols, cols, full, full],
        out_specs=cols,
        out_shape=wshape,
        scratch_shapes=[pltpu.VMEM((FFT_RADIX2, n1, tn), F32), pltpu.VMEM((FFT_RADIX2, n1, tn), F32)],
        compiler_params=_params("arbitrary", "arbitrary"),
        name="fourier_sequence_dft",
    )(wr, wi, cf, sf)
    return y.reshape(B * S, D_MODEL)


def _proj_ln_route_kernel(a_ref, b_ref, x_ref, w_ref, gain_ref, bias_ref, wr_ref, br_ref,
                          xo_ref, xt_ref, ri_ref, rg_ref, cnt_ref, carry_scr):
    tm = x_ref.shape[0]
    half = a_ref.shape[1]

    @pl.when(pl.program_id(0) == 0)
    def _():
        carry_scr[...] = jnp.zeros_like(carry_scr)

    mix = (jnp.dot(a_ref[...], w_ref[:half, :], preferred_element_type=F32)
           + jnp.dot(b_ref[...], w_ref[half:, :], preferred_element_type=F32))
    y = _layer_norm(ALPHA * x_ref[...] + mix, gain_ref[...], bias_ref[...])
    xo_ref[...] = y
    _store_token_tiles(xt_ref, y)

    logits = jnp.dot(y.astype(BF16), wr_ref[...], preferred_element_type=F32) + br_ref[...]
    lane = lax.broadcasted_iota(I32, (tm, LANES), 1)
    lanef = lane.astype(F32)
    ninf = -jnp.inf
    big = 1e9

    def rmax(v):
        return jnp.max(v, axis=1, keepdims=True)

    def first_lane(hit):
        return jnp.min(jnp.where(hit, lanef, big), axis=1, keepdims=True)

    cmask = (lane >= N_EXPERTS) & (lane < N_EXPERTS + N_GROUPS)
    cl = jnp.where(cmask, logits, ninf)
    cmax = rmax(cl)
    group = first_lane(cl == cmax) - float(N_EXPERTS)
    p_group = 1.0 / jnp.sum(jnp.where(cmask, jnp.exp(cl - cmax), 0.0), axis=1, keepdims=True)
    lo = group * float(EXPERTS_PER_GROUP)
    fmask = (lanef >= lo) & (lanef < lo + float(EXPERTS_PER_GROUP))
    fl = jnp.where(fmask, logits, ninf)
    v1 = rmax(fl)
    e1 = first_lane(fl == v1)
    fl2 = jnp.where(lanef == e1, ninf, fl)
    v2 = rmax(fl2)
    e2 = first_lane(fl2 == v2)
    t = jnp.exp(v2 - v1)
    g1 = p_group / (1.0 + t)
    g2 = p_group * t / (1.0 + t)

    onehot = jnp.where((lanef == e1) | (lanef == e2), 1.0, 0.0)
    carry_scr[...] += jnp.sum(onehot, axis=0, keepdims=True)
    cnt_ref[...] = carry_scr[...]
    ri_ref[...] = jnp.where(lane == 0, e1, e2).astype(I32)
    rg_ref[...] = jnp.where(lane == 0, g1, g2)


def _proj_ln_route(a, b, x2, w_bf16, gain, bias, w_router, b_router, tm):
    T = x2.shape[0]
    half = D_MODEL // 2
    row = lambda i: (i, 0)
    const = lambda i: (0, 0)
    a_spec = pl.BlockSpec((tm, half), row)
    b_spec = pl.BlockSpec((tm, half), row if b is not a else (lambda i: (i, 1)))
    return pl.pallas_call(
        _proj_ln_route_kernel,
        grid=(T // tm,),
        in_specs=[a_spec, b_spec, pl.BlockSpec((tm, D_MODEL), row),
                  pl.BlockSpec((D_MODEL, D_MODEL), const),
                  pl.BlockSpec((1, D_MODEL), const), pl.BlockSpec((1, D_MODEL), const),
                  pl.BlockSpec((D_MODEL, LANES), const), pl.BlockSpec((1, LANES), const)],
        out_specs=[pl.BlockSpec((tm, D_MODEL), row), pl.BlockSpec((tm * TOKEN_ROWS, LANES), row),
                   pl.BlockSpec((tm, LANES), row),
                   pl.BlockSpec((tm, LANES), row), pl.BlockSpec((1, LANES), const)],
        out_shape=[jax.ShapeDtypeStruct((T, D_MODEL), F32),
                   jax.ShapeDtypeStruct((T * TOKEN_ROWS, LANES), F32), jax.ShapeDtypeStruct((T, LANES), I32),
                   jax.ShapeDtypeStruct((T, LANES), F32), jax.ShapeDtypeStruct((1, LANES), F32)],
        scratch_shapes=[pltpu.VMEM((1, LANES), F32)],
        compiler_params=_params("arbitrary"),
        name="proj_ln_router",
    )(a, b, x2, w_bf16, gain, bias, w_router, b_router)


def _expert_kernel(te_ref, base_ref, nv_ref, src_ref, dst_ref, x_hbm, wg_ref, wu_ref, wd_ref,
                   y_hbm, wg_scr, wu_scr, wd_scr, xbuf0, xbuf1, obuf0, obuf1, gsem, ssem):
    i = pl.program_id(0)
    nv = nv_ref[0]
    tile = xbuf0.shape[0] // TOKEN_ROWS
    xbufs, obufs = (xbuf0, xbuf1), (obuf0, obuf1)

    def token_rows(t):
        return pl.ds(pl.multiple_of(t * TOKEN_ROWS, TOKEN_ROWS), TOKEN_ROWS)

    def gather_row(base, r, p):
        return pltpu.make_async_copy(x_hbm.at[token_rows(src_ref[base + r]), :],
                                     xbufs[p].at[token_rows(r), :], gsem.at[p])

    def scatter_row(base, r, p):
        return pltpu.make_async_copy(obufs[p].at[token_rows(r), :],
                                     y_hbm.at[token_rows(dst_ref[base + r]), :], ssem.at[p])

    def wait_gather(p):
        pltpu.make_async_copy(x_hbm.at[pl.ds(0, tile * TOKEN_ROWS), :], xbufs[p], gsem.at[p]).wait()

    def wait_scatter(p):
        pltpu.make_async_copy(obufs[p], y_hbm.at[pl.ds(0, tile * TOKEN_ROWS), :], ssem.at[p]).wait()

    @pl.when(i == 0)
    def _():
        obuf1[...] = jnp.zeros_like(obuf1)

        def first(r, carry):
            gather_row(base_ref[0], r, 0).start()
            return carry

        lax.fori_loop(0, tile, first, 0, unroll=ROW_DMA_UNROLL)

    @pl.when(i < nv)
    def _():
        @pl.when((i == 0) | (te_ref[i] != te_ref[jnp.maximum(i - 1, 0)]))
        def _():
            wg_scr[...] = wg_ref[0, 0].astype(BF16)
            wu_scr[...] = wu_ref[0, 0].astype(BF16)
            wd_scr[...] = wd_ref[0, 0].astype(BF16)

        base_next = base_ref[i + 1]
        pad_base = dst_ref.shape[0] - tile
        base_prev = jnp.where(i >= 1, base_ref[jnp.maximum(i - 1, 0)], pad_base)
        for p in range(2):
            @pl.when(i % 2 == p)
            def _():
                wait_gather(p)

                @pl.when(i >= 1)
                def _():
                    wait_scatter(p)

                for r in range(tile):
                    gather_row(base_next, r, 1 - p).start(priority=r % 2)
                    scatter_row(base_prev, r, 1 - p).start(priority=(r + 1) % 2)
                xb = _load_token_tiles(xbufs[p]).astype(BF16)
                hg = jnp.dot(xb, wg_scr[...], preferred_element_type=F32)
                hu = jnp.dot(xb, wu_scr[...], preferred_element_type=F32)
                hid = (_silu(hg) * hu).astype(BF16)
                _store_token_tiles(obufs[p], jnp.dot(hid, wd_scr[...], preferred_element_type=F32))

                @pl.when(i == nv - 1)
                def _():
                    wait_gather(1 - p)
                    wait_scatter(1 - p)

                    def last(r, carry):
                        scatter_row(base_ref[i], r, p).start()
                        return carry

                    lax.fori_loop(0, tile, last, 0, unroll=ROW_DMA_UNROLL)
                    wait_scatter(p)


def _experts(x1t, src, dst, tile_expert, tile_base, n_valid, w_gate, w_up, w_down, layer, tile, n_out):
    n_tiles = tile_expert.shape[0]
    wsel = lambda i, te, *_: (layer, te[i], 0, 0)
    buf = pltpu.VMEM((tile * TOKEN_ROWS, LANES), F32)
    return pl.pallas_call(
        _expert_kernel,
        grid_spec=pltpu.PrefetchScalarGridSpec(
            num_scalar_prefetch=5,
            grid=(n_tiles,),
            in_specs=[pl.BlockSpec(memory_space=pl.ANY),
                      pl.BlockSpec((1, 1, D_MODEL, EXPERT_HIDDEN), wsel),
                      pl.BlockSpec((1, 1, D_MODEL, EXPERT_HIDDEN), wsel),
                      pl.BlockSpec((1, 1, EXPERT_HIDDEN, D_MODEL), wsel)],
            out_specs=pl.BlockSpec(memory_space=pl.ANY),
            scratch_shapes=[pltpu.VMEM((D_MODEL, EXPERT_HIDDEN), BF16),
                            pltpu.VMEM((D_MODEL, EXPERT_HIDDEN), BF16),
                            pltpu.VMEM((EXPERT_HIDDEN, D_MODEL), BF16),
                            buf, buf, buf, buf,
                            pltpu.SemaphoreType.DMA((2,)), pltpu.SemaphoreType.DMA((2,))],
        ),
        out_shape=jax.ShapeDtypeStruct((n_out * TOKEN_ROWS, LANES), F32),
        compiler_params=_params("arbitrary"),
        name="moe_experts",
    )(tile_expert, tile_base, n_valid, src, dst, x1t, w_gate, w_up, w_down)


def _combine_ln_kernel(x_ref, y0_ref, y1_ref, g_ref, gain_ref, bias_ref, o_ref):
    g = g_ref[...]
    ffn = _load_token_tiles(y0_ref) * g[:, 0:1] + _load_token_tiles(y1_ref) * g[:, 1:2]
    o_ref[...] = _layer_norm(ALPHA * x_ref[...] + ffn, gain_ref[...], bias_ref[...])


def _combine_ln(x1, y2t, gates, gain, bias, tm):
    T = x1.shape[0]
    row = lambda i: (i, 0)
    const = lambda i: (0, 0)
    return pl.pallas_call(
        _combine_ln_kernel,
        grid=(T // tm,),
        in_specs=[pl.BlockSpec((tm, D_MODEL), row), pl.BlockSpec((tm * TOKEN_ROWS, LANES), row),
                  pl.BlockSpec((tm * TOKEN_ROWS, LANES), lambda i: (T // tm + i, 0)),
                  pl.BlockSpec((tm, LANES), row),
                  pl.BlockSpec((1, D_MODEL), const), pl.BlockSpec((1, D_MODEL), const)],
        out_specs=pl.BlockSpec((tm, D_MODEL), row),
        out_shape=jax.ShapeDtypeStruct((T, D_MODEL), F32),
        compiler_params=_params("arbitrary"),
        name="moe_combine_ln",
    )(x1, y2t, y2t, gates, gain, bias)


def _moe(x1, x1t, route_i, route_g, counts_f, w_gate, w_up, w_down, layer, gain, bias, tile, tm):
    T = x1.shape[0]
    n_pairs = 2 * T
    counts = counts_f[0, :N_EXPERTS].astype(I32)
    eids = jnp.arange(N_EXPERTS, dtype=I32)
    key = route_i[:, 0:2].reshape(-1) * n_pairs + jnp.arange(n_pairs, dtype=I32)
    pair = jnp.sort(key) % n_pairs
    pad = jnp.arange(tile, dtype=I32)
    src = jnp.concatenate([pair // 2, jnp.zeros((tile,), I32)])
    dst = jnp.concatenate([(pair % 2) * T + pair // 2, n_pairs + pad])
    tiles_per = (counts + tile - 1) // tile
    tile_ends = jnp.cumsum(tiles_per)
    first_row = jnp.cumsum(counts) - counts
    n_tiles = n_pairs // tile + N_EXPERTS
    tile_ids = jnp.arange(n_tiles + 1, dtype=I32)
    owner = jnp.sum((tile_ids[:, None] >= tile_ends[None, :]).astype(I32), axis=1)
    valid = owner < N_EXPERTS
    last_expert = jnp.max(jnp.where(counts > 0, eids, 0))
    tile_expert = jnp.minimum(owner, last_expert).astype(I32)
    onehot = owner[:, None] == eids[None, :]
    pick = lambda v: jnp.sum(jnp.where(onehot, v[None, :], 0), axis=1)
    tile_base = pick(first_row) + (tile_ids - pick(tile_ends - tiles_per)) * tile
    tile_base = jnp.where(valid, tile_base, n_pairs).astype(I32)
    n_valid = tile_ends[-1:].astype(I32)
    y2t = _experts(x1t, src, dst, tile_expert[:n_tiles], tile_base, n_valid, w_gate, w_up, w_down,
                   layer, tile, n_pairs + tile)
    return _combine_ln(x1, y2t, route_g, gain, bias, tm)


def _pick(n, pref):
    t = min(n, pref)
    while n % t:
        t //= 2
    return t


def kernel(x, w_in_even, ret_decay_logit, ret_gn_gain, sink_logit, w_out_even, w_out_fourier,
           ln1_gain, ln1_bias, ln2_gain, ln2_bias, router_coarse_w, router_coarse_b,
           router_fine_w, router_fine_b, expert_w_gate, expert_w_up, expert_w_down):
    B, S, D = x.shape
    assert D == D_MODEL and S % (FFT_RADIX2 * 8) == 0 and S % WINDOW == 0 and S % RET_BLOCK == 0
    T = B * S
    tm = _pick(S, 512)
    tile = _pick(T, 512)
    rope_tabs = _rope_tables(S)
    fourier_tabs = _fourier_tables(S)
    row = lambda v: v.reshape(1, -1).astype(F32)

    x2 = x.reshape(T, D).astype(F32)
    for layer in range(DEPTH):
        if layer % 2 == 0:
            e = layer // 2
            qa, ka_t, va, ga, qb, kb2, vb2 = _inproj(x2, w_in_even[e].astype(BF16), rope_tabs, S, tm)
            shp = lambda v: v.reshape(B, S, v.shape[-1])
            ya = _retention(shp(qa), ka_t, shp(va), shp(ga), ret_decay_logit[e], ret_gn_gain[e])
            yb = _winattn(shp(qb), shp(kb2), shp(vb2), sink_logit[e], _pick(S // WINDOW, 4))
            a, b = ya.reshape(T, RET_WIDTH), yb.reshape(T, WIN_WIDTH)
            w_out = w_out_even[e]
        else:
            a = b = _fourier(x2.reshape(B, S, D), fourier_tabs)
            w_out = w_out_fourier[layer // 2]
        w_router = jnp.zeros((D, LANES), F32)
        w_router = w_router.at[:, :N_EXPERTS].set(router_fine_w[layer])
        w_router = w_router.at[:, N_EXPERTS:N_EXPERTS + N_GROUPS].set(router_coarse_w[layer])
        b_router = jnp.zeros((1, LANES), F32)
        b_router = b_router.at[0, :N_EXPERTS].set(router_fine_b[layer])
        b_router = b_router.at[0, N_EXPERTS:N_EXPERTS + N_GROUPS].set(router_coarse_b[layer])
        x1, x1t, route_i, route_g, counts = _proj_ln_route(
            a, b, x2, w_out.astype(BF16), row(ln1_gain[layer]), row(ln1_bias[layer]),
            w_router.astype(BF16), b_router, tm)
        x2 = _moe(x1, x1t, route_i, route_g, counts, expert_w_gate, expert_w_up, expert_w_down, layer,
                  row(ln2_gain[layer]), row(ln2_bias[layer]), tile, tm)
    return x2.reshape(B, S, D).astype(x.dtype)
```

```python
import math

import jax
import jax.numpy as jnp
from jax import lax
from jax.experimental import pallas as pl
from jax.experimental.pallas import tpu as pltpu

F32 = jnp.float32
BF16 = jnp.bfloat16
I32 = jnp.int32

D_MODEL = 1024
DEPTH = 4
RET_HEADS = 4
RET_DIM = 128
RET_BLOCK = 256
RET_WIDTH = RET_HEADS * RET_DIM
WIN_Q_HEADS = 8
WIN_KV_HEADS = 2
WIN_DIM = 64
WINDOW = 128
WIN_WIDTH = WIN_Q_HEADS * WIN_DIM
WIN_KV = WIN_KV_HEADS * WIN_DIM
FOURIER_GROUPS = 4
GROUP_CH = D_MODEL // FOURIER_GROUPS
FFT_RADIX2 = 16
ROPE_THETA = 10000.0
N_GROUPS = 4
EXPERTS_PER_GROUP = 8
N_EXPERTS = N_GROUPS * EXPERTS_PER_GROUP
EXPERT_HIDDEN = D_MODEL // 2
LN_EPS = 1e-5
GN_EPS = 1e-6
ALPHA = (2.0 * DEPTH) ** 0.25
IN_EVEN = 2 * RET_WIDTH + 2 * RET_WIDTH + WIN_WIDTH + 2 * WIN_KV

LANES = 128
TOKEN_ROWS = D_MODEL // LANES
VMEM_LIMIT_BYTES = 48 * 1024 * 1024
NEG_MASK = -1e30
ROW_DMA_UNROLL = 8

_NT = (((1,), (1,)), ((), ()))


def _params(*sem):
    return pltpu.CompilerParams(dimension_semantics=sem, vmem_limit_bytes=VMEM_LIMIT_BYTES)


def _silu(v):
    return v / (1.0 + jnp.exp(-v))


def _layer_norm(z, gain, bias):
    mu = jnp.mean(z, axis=-1, keepdims=True)
    zc = z - mu
    var = jnp.mean(zc * zc, axis=-1, keepdims=True)
    return zc * lax.rsqrt(var + LN_EPS) * gain + bias


def _store_token_tiles(ref, v):
    n = v.shape[0]
    for c in range(TOKEN_ROWS):
        ref[pl.ds(c, n, stride=TOKEN_ROWS), :] = v[:, c * LANES:(c + 1) * LANES]


def _load_token_tiles(ref):
    n = ref.shape[0] // TOKEN_ROWS
    return jnp.concatenate([ref[pl.ds(c, n, stride=TOKEN_ROWS), :] for c in range(TOKEN_ROWS)], axis=1)


def _rope128(h, cos, sin_signed):
    return h * cos + pltpu.roll(h, 64, 1) * sin_signed


def _rope64x2(h, cos, sin_lo, sin_hi):
    return h * cos + pltpu.roll(h, 96, 1) * sin_lo + pltpu.roll(h, 32, 1) * sin_hi


def _inproj_kernel(x_ref, w_ref, cr_ref, sr_ref, cw_ref, slo_ref, shi_ref,
                   qa_ref, ka_ref, va_ref, ga_ref, qb_ref, kb_ref, vb_ref):
    xb = x_ref[...].astype(BF16)

    def seg(lo, hi):
        return jnp.dot(xb, w_ref[:, lo:hi], preferred_element_type=F32)

    cr, sr = cr_ref[...], sr_ref[...]
    q = seg(0, RET_WIDTH)
    k = seg(RET_WIDTH, 2 * RET_WIDTH)
    for h in range(RET_HEADS):
        sl = slice(LANES * h, LANES * (h + 1))
        qa_ref[:, sl] = _rope128(q[:, sl], cr, sr).astype(BF16)
        ka_ref[sl, :] = (_rope128(k[:, sl], cr, sr) * RET_DIM ** -0.5).T.astype(BF16)
    va_ref[...] = seg(2 * RET_WIDTH, 3 * RET_WIDTH).astype(BF16)
    ga_ref[...] = _silu(seg(3 * RET_WIDTH, 4 * RET_WIDTH)).astype(BF16)
    cw, slo, shi = cw_ref[...], slo_ref[...], shi_ref[...]
    base = 4 * RET_WIDTH
    q = seg(base, base + WIN_WIDTH)
    for p in range(WIN_WIDTH // LANES):
        sl = slice(LANES * p, LANES * (p + 1))
        qb_ref[:, sl] = (_rope64x2(q[:, sl], cw, slo, shi) * WIN_DIM ** -0.5).astype(BF16)
    kv = seg(base + WIN_WIDTH, base + WIN_WIDTH + 2 * WIN_KV)
    kb = _rope64x2(kv[:, :WIN_KV], cw, slo, shi)
    vb = kv[:, WIN_KV:]
    kb_ref[:, :WIN_KV] = kb.astype(BF16)
    kb_ref[:, WIN_KV:] = pltpu.roll(kb, WIN_DIM, 1).astype(BF16)
    vb_ref[:, :WIN_KV] = vb.astype(BF16)
    vb_ref[:, WIN_KV:] = pltpu.roll(vb, WIN_DIM, 1).astype(BF16)


def _rope_tables(seq):
    pos = jnp.arange(seq, dtype=F32)[:, None]
    half = RET_DIM // 2
    inv = ROPE_THETA ** (-jnp.arange(half, dtype=F32) / half)
    ang = pos * inv[None, :]
    cr = jnp.concatenate([jnp.cos(ang), jnp.cos(ang)], axis=1)
    sr = jnp.concatenate([-jnp.sin(ang), jnp.sin(ang)], axis=1)
    half = WIN_DIM // 2
    inv = ROPE_THETA ** (-jnp.arange(half, dtype=F32) / half)
    ang = pos * inv[None, :]
    c, s, z = jnp.cos(ang), jnp.sin(ang), jnp.zeros_like(ang)
    cw = jnp.concatenate([c, c, c, c], axis=1)
    slo = jnp.concatenate([-s, z, -s, z], axis=1)
    shi = jnp.concatenate([z, s, z, s], axis=1)
    return cr, sr, cw, slo, shi


def _inproj(x2, w_bf16, tables, seq, tm):
    T = x2.shape[0]
    nseq = seq // tm
    row = lambda i: (i, 0)
    tab = lambda i: (i % nseq, 0)
    widths = (RET_WIDTH, None, RET_WIDTH, RET_WIDTH, WIN_WIDTH, 2 * WIN_KV, 2 * WIN_KV)
    return pl.pallas_call(
        _inproj_kernel,
        grid=(T // tm,),
        in_specs=[pl.BlockSpec((tm, D_MODEL), row),
                  pl.BlockSpec((D_MODEL, IN_EVEN), lambda i: (0, 0))]
                 + [pl.BlockSpec((tm, LANES), tab)] * 5,
        out_specs=[pl.BlockSpec((tm, w), row) if w else pl.BlockSpec((RET_WIDTH, tm), lambda i: (0, i))
                   for w in widths],
        out_shape=[jax.ShapeDtypeStruct((T, w) if w else (RET_WIDTH, T), BF16) for w in widths],
        compiler_params=_params("arbitrary"),
        name="inproj_rope",
    )(x2, w_bf16, *tables)


def _retention_kernel(logit_ref, q_ref, kt_ref, v_ref, g_ref, gain_ref, o_ref,
                      dmat_scr, kv_scr, state_scr):
    h = pl.program_id(1)
    C = RET_BLOCK
    dk = RET_DIM
    n_chunks = q_ref.shape[1] // C

    def log_gamma(d):
        v = jnp.full((1, 1), logit_ref[d, h], F32)
        return -(jnp.maximum(-v, 0.0) + jnp.log(1.0 + jnp.exp(-jnp.abs(v))))

    lgf, lgb = log_gamma(0), log_gamma(1)
    i = lax.broadcasted_iota(I32, (C, C), 0)
    j = lax.broadcasted_iota(I32, (C, C), 1)
    diff = (i - j).astype(F32)
    dmat_scr[...] = jnp.where(diff >= 0, jnp.exp(lgf * jnp.maximum(diff, 0.0)),
                              jnp.exp(lgb * jnp.maximum(-diff, 0.0)))
    col = lax.broadcasted_iota(I32, (C, 1), 0).astype(F32)
    lane = lax.broadcasted_iota(I32, (1, C), 1).astype(F32)
    xi_f, xi_b = jnp.exp(lgf * (col + 1.0)), jnp.exp(lgb * (C - col))
    zeta_f, zeta_b = jnp.exp(lgf * (C - 1.0 - lane)), jnp.exp(lgb * lane)
    dec_f, dec_b = jnp.exp(lgf * C), jnp.exp(lgb * C)
    gain = gain_ref[...]

    def span(n):
        return pl.ds(pl.multiple_of(n * C, C), C)

    def kv_pass(n, carry):
        kt = kt_ref[:, span(n)].astype(F32)
        lhs = jnp.concatenate([(kt * zeta_f).astype(BF16), (kt * zeta_b).astype(BF16)], axis=0)
        kv_scr[n] = jnp.dot(lhs, v_ref[0, span(n), :], preferred_element_type=F32)
        return carry

    lax.fori_loop(0, n_chunks, kv_pass, 0, unroll=4)

    def scan_f(n, state):
        state_scr[n, :dk, :] = state.astype(BF16)
        return state * dec_f + kv_scr[n, :dk, :]

    def scan_b(t, state):
        n = n_chunks - 1 - t
        state_scr[n, dk:, :] = state.astype(BF16)
        return state * dec_b + kv_scr[n, dk:, :]

    zero = jnp.zeros((dk, dk), F32)
    lax.fori_loop(0, n_chunks, scan_f, zero)
    lax.fori_loop(0, n_chunks, scan_b, zero)

    def out_pass(n, carry):
        q = q_ref[0, span(n), :]
        s = jnp.dot(q, kt_ref[:, span(n)], preferred_element_type=F32)
        p = (s * dmat_scr[...]).astype(BF16)
        qf = q.astype(F32)
        qx = jnp.concatenate([(qf * xi_f).astype(BF16), (qf * xi_b).astype(BF16)], axis=1)
        y = (jnp.dot(p, v_ref[0, span(n), :], preferred_element_type=F32)
             + jnp.dot(qx, state_scr[n], preferred_element_type=F32))
        mu = jnp.mean(y, axis=-1, keepdims=True)
        yc = y - mu
        var = jnp.mean(yc * yc, axis=-1, keepdims=True)
        yn = yc * lax.rsqrt(var + GN_EPS) * gain
        o_ref[0, span(n), :] = (g_ref[0, span(n), :].astype(F32) * yn).astype(BF16)
        return carry

    lax.fori_loop(0, n_chunks, out_pass, 0, unroll=4)


def _retention(qa, ka_t, va, ga, decay_logit, gn_gain):
    B, S, _ = qa.shape
    n_chunks = S // RET_BLOCK
    head = pl.BlockSpec((1, S, RET_DIM), lambda b, h: (b, 0, h))
    return pl.pallas_call(
        _retention_kernel,
        grid=(B, RET_HEADS),
        in_specs=[pl.BlockSpec(memory_space=pltpu.SMEM), head,
                  pl.BlockSpec((RET_DIM, S), lambda b, h: (h, b)), head, head,
                  pl.BlockSpec((1, RET_DIM), lambda b, h: (0, h))],
        out_specs=head,
        out_shape=jax.ShapeDtypeStruct((B, S, RET_WIDTH), BF16),
        scratch_shapes=[pltpu.VMEM((RET_BLOCK, RET_BLOCK), F32),
                        pltpu.VMEM((n_chunks, 2 * RET_DIM, RET_DIM), F32),
                        pltpu.VMEM((n_chunks, 2 * RET_DIM, RET_DIM), BF16)],
        compiler_params=_params("arbitrary", "arbitrary"),
        name="retention_gn_gate",
    )(decay_logit.astype(F32), qa, ka_t, va, ga, gn_gain.reshape(1, RET_WIDTH).astype(F32))


def _winattn_kernel(sink_ref, q_ref, kp_ref, kc_ref, kn_ref, vp_ref, vc_ref, vn_ref, o_ref):
    step = pl.program_id(1)
    n_steps = pl.num_programs(1)
    W = WINDOW
    nq = q_ref.shape[1] // W
    group = WIN_Q_HEADS // WIN_KV_HEADS
    k_all = jnp.concatenate([kp_ref[0], kc_ref[0], kn_ref[0]], axis=0)
    v_all = jnp.concatenate([vp_ref[0], vc_ref[0], vn_ref[0]], axis=0)
    lo_half = lax.broadcasted_iota(I32, (1, LANES), 1) < WIN_DIM

    def placements(slab, g):
        first, second = slab[:, :LANES], slab[:, LANES:]
        zero = jnp.zeros_like(first)
        if g == 0:
            return jnp.where(lo_half, first, zero), jnp.where(lo_half, zero, second)
        return jnp.where(lo_half, second, zero), jnp.where(lo_half, zero, first)

    rows2 = lax.broadcasted_iota(I32, (2 * W, 1), 0)
    qi = lax.broadcasted_iota(I32, (2 * W, 3 * W), 0) & (W - 1)
    cj = lax.broadcasted_iota(I32, (2 * W, 3 * W), 1)
    rel = cj - qi
    band = (rel >= 0) & (rel <= 2 * W)

    for g in range(WIN_KV_HEADS):
        k_even, k_odd = placements(k_all, g)
        v_even, v_odd = placements(v_all, g)
        c0 = group * WIN_DIM * g
        sink_even = jnp.where(rows2 < W, sink_ref[0, group * g], sink_ref[0, group * g + 2])
        sink_odd = jnp.where(rows2 < W, sink_ref[0, group * g + 1], sink_ref[0, group * g + 3])
        for jq in range(nq):
            qrows = slice(jq * W, (jq + 1) * W)
            krows = slice(jq * W, (jq + 3) * W)
            mask = band
            if jq == 0:
                mask = mask & (cj >= jnp.where(step == 0, W, 0))
            if jq == nq - 1:
                mask = mask & (cj < jnp.where(step == n_steps - 1, 2 * W, 3 * W))
            q2 = jnp.concatenate([q_ref[0, qrows, c0:c0 + LANES],
                                  q_ref[0, qrows, c0 + LANES:c0 + 2 * LANES]], axis=0)

            def softmax_parts(k_placed, sink):
                s = lax.dot_general(q2, k_placed[krows], _NT, preferred_element_type=F32)
                s = jnp.where(mask, s, NEG_MASK)
                m = jnp.maximum(jnp.max(s, axis=1, keepdims=True), sink)
                e = jnp.exp(s - m)
                return e.astype(BF16), jnp.sum(e, axis=1, keepdims=True) + jnp.exp(sink - m)

            p_even, den_even = softmax_parts(k_even, sink_even)
            p_odd, den_odd = softmax_parts(k_odd, sink_odd)
            o = (jnp.dot(p_even, v_even[krows], preferred_element_type=F32)
                 + jnp.dot(p_odd, v_odd[krows], preferred_element_type=F32))
            o = (o / jnp.where(lo_half, den_even, den_odd)).astype(BF16)
            o_ref[0, qrows, c0:c0 + LANES] = o[:W]
            o_ref[0, qrows, c0 + LANES:c0 + 2 * LANES] = o[W:]


def _winattn(qb, kb2, vb2, sink_logit, blocks_per_step):
    B, S, _ = qb.shape
    nq = blocks_per_step
    nb = S // WINDOW
    prev = pl.BlockSpec((1, WINDOW, 2 * WIN_KV), lambda b, s: (b, jnp.maximum(s * nq - 1, 0), 0))
    cur = pl.BlockSpec((1, nq * WINDOW, 2 * WIN_KV), lambda b, s: (b, s, 0))
    nxt = pl.BlockSpec((1, WINDOW, 2 * WIN_KV), lambda b, s: (b, jnp.minimum((s + 1) * nq, nb - 1), 0))
    qspec = pl.BlockSpec((1, nq * WINDOW, WIN_WIDTH), lambda b, s: (b, s, 0))
    return pl.pallas_call(
        _winattn_kernel,
        grid=(B, nb // nq),
        in_specs=[pl.BlockSpec(memory_space=pltpu.SMEM), qspec, prev, cur, nxt, prev, cur, nxt],
        out_specs=qspec,
        out_shape=jax.ShapeDtypeStruct((B, S, WIN_WIDTH), BF16),
        compiler_params=_params("arbitrary", "arbitrary"),
        name="window_attention",
    )(sink_logit.reshape(1, WIN_Q_HEADS).astype(F32), qb, kb2, kb2, kb2, vb2, vb2, vb2)


def _chan_dft_kernel(xa_ref, xb_ref, m_ref, wr_ref, wi_ref):
    n1 = xa_ref.shape[1] // FFT_RADIX2
    for s in range(FFT_RADIX2):
        rows = pl.ds(s, n1, stride=FFT_RADIX2)
        w = (jnp.dot(xa_ref[0, rows, :].astype(BF16), m_ref[:LANES, :], preferred_element_type=F32)
             + jnp.dot(xb_ref[0, rows, :].astype(BF16), m_ref[LANES:, :], preferred_element_type=F32))
        wr_ref[0, s] = w[:, :GROUP_CH].astype(BF16)
        wi_ref[0, s] = w[:, GROUP_CH:].astype(BF16)


def _real_dft16(ar, ai):
    n = FFT_RADIX2
    cs = [math.cos(2 * math.pi * m / n) for m in range(n)]
    sn = [math.sin(2 * math.pi * m / n) for m in range(n)]

    def axpy(acc, coef, v):
        if abs(coef) < 1e-12:
            return acc
        if abs(abs(coef) - 1.0) < 1e-12:
            if acc is None:
                return v if coef > 0 else -v
            return acc + v if coef > 0 else acc - v
        t = coef * v
        return t if acc is None else acc + t

    p = {s: ar[s] + ar[n - s] for s in range(1, n // 2)}
    d = {s: ai[s] - ai[n - s] for s in range(1, n // 2)}
    base = (ar[0] + ar[n // 2], ar[0] - ar[n // 2])
    y = [None] * n
    for k in range(n // 2 + 1):
        e = base[k % 2]
        for s in range(1, n // 2):
            e = axpy(e, cs[(s * k) % n], p[s])
        if k in (0, n // 2):
            y[k] = e
            continue
        o = None
        for s in range(1, n // 2):
            o = axpy(o, sn[(s * k) % n], d[s])
        y[k] = e + o
        y[n - k] = e - o
    return y


def _seq_dft_kernel(wr_ref, wi_ref, cf_ref, sf_ref, y_ref, ar_scr, ai_scr):
    n1 = wr_ref.shape[2]
    for s in range(FFT_RADIX2):
        cf, sf, wr, wi = cf_ref[s], sf_ref[s], wr_ref[0, s], wi_ref[0, s]
        ar_scr[s] = (jnp.dot(cf, wr, preferred_element_type=F32)
                     + jnp.dot(sf, wi, preferred_element_type=F32))
        ai_scr[s] = (jnp.dot(cf, wi, preferred_element_type=F32)
                     - jnp.dot(sf, wr, preferred_element_type=F32))

    def rows8(r, carry):
        rows = pl.ds(pl.multiple_of(r * 8, 8), 8)
        y = _real_dft16([ar_scr[s, rows, :] for s in range(FFT_RADIX2)],
                        [ai_scr[s, rows, :] for s in range(FFT_RADIX2)])
        for k in range(FFT_RADIX2):
            y_ref[0, k, rows, :] = y[k].astype(BF16)
        return carry

    lax.fori_loop(0, n1 // 8, rows8, 0)


def _fourier_tables(seq):
    n1 = seq // FFT_RADIX2
    c = jnp.arange(GROUP_CH, dtype=I32)
    m = (c[:, None] * c[None, :]) % GROUP_CH
    ang = m.astype(F32) * (2.0 * math.pi / GROUP_CH)
    scale = (seq * GROUP_CH) ** -0.5
    chan = (jnp.concatenate([jnp.cos(ang), -jnp.sin(ang)], axis=1) * scale).astype(BF16)
    k1 = jnp.arange(n1, dtype=I32)[None, :, None]
    s1 = jnp.arange(n1, dtype=I32)[None, None, :]
    s2 = jnp.arange(FFT_RADIX2, dtype=I32)[:, None, None]
    m = (FFT_RADIX2 * s1 * k1 + s2 * k1) % seq
    ang = m.astype(F32) * (2.0 * math.pi / seq)
    return chan, jnp.cos(ang).astype(BF16), jnp.sin(ang).astype(BF16)


def _fourier(x, tables, tn=256):
    B, S, _ = x.shape
    n1 = S // FFT_RADIX2
    chan, cf, sf = tables
    wshape = jax.ShapeDtypeStruct((B, FFT_RADIX2, n1, D_MODEL), BF16)
    wblk = pl.BlockSpec((1, FFT_RADIX2, n1, GROUP_CH), lambda b, g: (b, 0, 0, g))
    wr, wi = pl.pallas_call(
        _chan_dft_kernel,
        grid=(B, FOURIER_GROUPS),
        in_specs=[pl.BlockSpec((1, S, LANES), lambda b, g: (b, 0, 2 * g)),
                  pl.BlockSpec((1, S, LANES), lambda b, g: (b, 0, 2 * g + 1)),
                  pl.BlockSpec((GROUP_CH, 2 * GROUP_CH), lambda b, g: (0, 0))],
        out_specs=[wblk, wblk],
        out_shape=[wshape, wshape],
        compiler_params=_params("arbitrary", "arbitrary"),
        name="fourier_channel_dft",
    )(x, x, chan)
    cols = pl.BlockSpec((1, FFT_RADIX2, n1, tn), lambda b, j: (b, 0, 0, j))
    full = pl.BlockSpec((FFT_RADIX2, n1, n1), lambda b, j: (0, 0, 0))
    y = pl.pallas_call(
        _seq_dft_kernel,
        grid=(B, D_MODEL // tn),
        in_specs=[cols, cols, full, full],
        out_specs=cols,
        out_shape=wshape,
        scratch_shapes=[pltpu.VMEM((FFT_RADIX2, n1, tn), F32), pltpu.VMEM((FFT_RADIX2, n1, tn), F32)],
        compiler_params=_params("arbitrary", "arbitrary"),
        name="fourier_sequence_dft",
    )(wr, wi, cf, sf)
    return y.reshape(B * S, D_MODEL)


def _proj_ln_route_kernel(a_ref, b_ref, x_ref, w_ref, gain_ref, bias_ref, wr_ref, br_ref,
                          xo_ref, xt_ref, ri_ref, rg_ref, cnt_ref, carry_scr):
    tm = x_ref.shape[0]
    half = a_ref.shape[1]

    @pl.when(pl.program_id(0) == 0)
    def _():
        carry_scr[...] = jnp.zeros_like(carry_scr)

    mix = (jnp.dot(a_ref[...], w_ref[:half, :], preferred_element_type=F32)
           + jnp.dot(b_ref[...], w_ref[half:, :], preferred_element_type=F32))
    y = _layer_norm(ALPHA * x_ref[...] + mix, gain_ref[...], bias_ref[...])
    xo_ref[...] = y
    _store_token_tiles(xt_ref, y)

    logits = jnp.dot(y.astype(BF16), wr_ref[...], preferred_element_type=F32) + br_ref[...]
    lane = lax.broadcasted_iota(I32, (tm, LANES), 1)
    lanef = lane.astype(F32)
    ninf = -jnp.inf
    big = 1e9

    def rmax(v):
        return jnp.max(v, axis=1, keepdims=True)

    def first_lane(hit):
        return jnp.min(jnp.where(hit, lanef, big), axis=1, keepdims=True)

    cmask = (lane >= N_EXPERTS) & (lane < N_EXPERTS + N_GROUPS)
    cl = jnp.where(cmask, logits, ninf)
    cmax = rmax(cl)
    group = first_lane(cl == cmax) - float(N_EXPERTS)
    p_group = 1.0 / jnp.sum(jnp.where(cmask, jnp.exp(cl - cmax), 0.0), axis=1, keepdims=True)
    lo = group * float(EXPERTS_PER_GROUP)
    fmask = (lanef >= lo) & (lanef < lo + float(EXPERTS_PER_GROUP))
    fl = jnp.where(fmask, logits, ninf)
    v1 = rmax(fl)
    e1 = first_lane(fl == v1)
    fl2 = jnp.where(lanef == e1, ninf, fl)
    v2 = rmax(fl2)
    e2 = first_lane(fl2 == v2)
    t = jnp.exp(v2 - v1)
    g1 = p_group / (1.0 + t)
    g2 = p_group * t / (1.0 + t)

    onehot = jnp.where((lanef == e1) | (lanef == e2), 1.0, 0.0)
    carry_scr[...] += jnp.sum(onehot, axis=0, keepdims=True)
    cnt_ref[...] = carry_scr[...]
    ri_ref[...] = jnp.where(lane == 0, e1, e2).astype(I32)
    rg_ref[...] = jnp.where(lane == 0, g1, g2)


def _proj_ln_route(a, b, x2, w_bf16, gain, bias, w_router, b_router, tm):
    T = x2.shape[0]
    half = D_MODEL // 2
    row = lambda i: (i, 0)
    const = lambda i: (0, 0)
    a_spec = pl.BlockSpec((tm, half), row)
    b_spec = pl.BlockSpec((tm, half), row if b is not a else (lambda i: (i, 1)))
    return pl.pallas_call(
        _proj_ln_route_kernel,
        grid=(T // tm,),
        in_specs=[a_spec, b_spec, pl.BlockSpec((tm, D_MODEL), row),
                  pl.BlockSpec((D_MODEL, D_MODEL), const),
                  pl.BlockSpec((1, D_MODEL), const), pl.BlockSpec((1, D_MODEL), const),
                  pl.BlockSpec((D_MODEL, LANES), const), pl.BlockSpec((1, LANES), const)],
        out_specs=[pl.BlockSpec((tm, D_MODEL), row), pl.BlockSpec((tm * TOKEN_ROWS, LANES), row),
                   pl.BlockSpec((tm, LANES), row),
                   pl.BlockSpec((tm, LANES), row), pl.BlockSpec((1, LANES), const)],
        out_shape=[jax.ShapeDtypeStruct((T, D_MODEL), F32),
                   jax.ShapeDtypeStruct((T * TOKEN_ROWS, LANES), F32), jax.ShapeDtypeStruct((T, LANES), I32),
                   jax.ShapeDtypeStruct((T, LANES), F32), jax.ShapeDtypeStruct((1, LANES), F32)],
        scratch_shapes=[pltpu.VMEM((1, LANES), F32)],
        compiler_params=_params("arbitrary"),
        name="proj_ln_router",
    )(a, b, x2, w_bf16, gain, bias, w_router, b_router)


def _expert_kernel(te_ref, base_ref, rows_ref, nv_ref, src_ref, dst_ref, x_hbm, wg_ref, wu_ref, wd_ref,
                   y_hbm, wg_scr, wu_scr, wd_scr, xbuf0, xbuf1, xbuf2, obuf0, obuf1, obuf2, gsem, ssem):
    i = pl.program_id(0)
    nv = nv_ref[0]
    tile = xbuf0.shape[0] // TOKEN_ROWS
    xbufs, obufs = (xbuf0, xbuf1, xbuf2), (obuf0, obuf1, obuf2)
    n_buf = len(xbufs)
    spare0 = dst_ref.shape[0] - tile

    def token_rows(t):
        return pl.ds(pl.multiple_of(t * TOKEN_ROWS, TOKEN_ROWS), TOKEN_ROWS)

    def gather_row(base, r, q):
        return pltpu.make_async_copy(x_hbm.at[token_rows(src_ref[base + r]), :],
                                     xbufs[q].at[token_rows(r), :], gsem.at[q])

    def scatter_row(base, n_rows, r, q):
        row = jnp.where(r < n_rows, dst_ref[base + r], spare0 + q * tile + r)
        return pltpu.make_async_copy(obufs[q].at[token_rows(r), :], y_hbm.at[token_rows(row), :], ssem.at[q])

    def wait_gather(q):
        pltpu.make_async_copy(x_hbm.at[pl.ds(0, tile * TOKEN_ROWS), :], xbufs[q], gsem.at[q]).wait()

    def wait_scatter(q):
        pltpu.make_async_copy(obufs[q], y_hbm.at[pl.ds(0, tile * TOKEN_ROWS), :], ssem.at[q]).wait()

    @pl.when(i == 0)
    def _():
        obufs[n_buf - 1][...] = jnp.zeros_like(obufs[n_buf - 1])

        def first(r, carry):
            gather_row(base_ref[0], r, 0).start()
            gather_row(base_ref[1], r, 1).start()
            return carry

        lax.fori_loop(0, tile, first, 0, unroll=ROW_DMA_UNROLL)

    @pl.when(i < nv)
    def _():
        @pl.when((i == 0) | (te_ref[i] != te_ref[jnp.maximum(i - 1, 0)]))
        def _():
            wg_scr[...] = wg_ref[0, 0].astype(BF16)
            wu_scr[...] = wu_ref[0, 0].astype(BF16)
            wd_scr[...] = wd_ref[0, 0].astype(BF16)

        base_ahead = base_ref[i + 2]
        prev = jnp.maximum(i - 1, 0)
        base_prev = base_ref[prev]
        rows_prev = jnp.where(i >= 1, rows_ref[prev], 0)
        for q in range(n_buf):
            @pl.when(i % n_buf == q)
            def _():
                ahead, behind = (q + 2) % n_buf, (q + n_buf - 1) % n_buf
                wait_gather(q)

                @pl.when(i >= 2)
                def _():
                    wait_scatter(q)

                for r in range(tile):
                    gather_row(base_ahead, r, ahead).start(priority=r % 2)
                    scatter_row(base_prev, rows_prev, r, behind).start(priority=(r + 1) % 2)
                xb = _load_token_tiles(xbufs[q]).astype(BF16)
                hg = jnp.dot(xb, wg_scr[...], preferred_element_type=F32)
                hu = jnp.dot(xb, wu_scr[...], preferred_element_type=F32)
                hid = (_silu(hg) * hu).astype(BF16)
                _store_token_tiles(obufs[q], jnp.dot(hid, wd_scr[...], preferred_element_type=F32))

                @pl.when(i == nv - 1)
                def _():
                    wait_gather((q + 1) % n_buf)
                    wait_gather(ahead)
                    wait_scatter(behind)

                    @pl.when(i >= 1)
                    def _():
                        wait_scatter((q + 1) % n_buf)

                    def last(r, carry):
                        scatter_row(base_ref[i], rows_ref[i], r, q).start()
                        return carry

                    lax.fori_loop(0, tile, last, 0, unroll=ROW_DMA_UNROLL)
                    wait_scatter(q)


def _experts(x1t, src, dst, tile_expert, tile_base, tile_rows, n_valid, w_gate, w_up, w_down, layer, tile,
             n_out):
    n_tiles = tile_expert.shape[0]
    wsel = lambda i, te, *_: (layer, te[i], 0, 0)
    buf = pltpu.VMEM((tile * TOKEN_ROWS, LANES), F32)
    return pl.pallas_call(
        _expert_kernel,
        grid_spec=pltpu.PrefetchScalarGridSpec(
            num_scalar_prefetch=6,
            grid=(n_tiles,),
            in_specs=[pl.BlockSpec(memory_space=pl.ANY),
                      pl.BlockSpec((1, 1, D_MODEL, EXPERT_HIDDEN), wsel),
                      pl.BlockSpec((1, 1, D_MODEL, EXPERT_HIDDEN), wsel),
                      pl.BlockSpec((1, 1, EXPERT_HIDDEN, D_MODEL), wsel)],
            out_specs=pl.BlockSpec(memory_space=pl.ANY),
            scratch_shapes=[pltpu.VMEM((D_MODEL, EXPERT_HIDDEN), BF16),
                            pltpu.VMEM((D_MODEL, EXPERT_HIDDEN), BF16),
                            pltpu.VMEM((EXPERT_HIDDEN, D_MODEL), BF16),
                            buf, buf, buf, buf, buf, buf,
                            pltpu.SemaphoreType.DMA((3,)), pltpu.SemaphoreType.DMA((3,))],
        ),
        out_shape=jax.ShapeDtypeStruct((n_out * TOKEN_ROWS, LANES), F32),
        compiler_params=_params("arbitrary"),
        name="moe_experts",
    )(tile_expert, tile_base, tile_rows, n_valid, src, dst, x1t, w_gate, w_up, w_down)


def _combine_ln_kernel(x_ref, y0_ref, y1_ref, g_ref, gain_ref, bias_ref, o_ref):
    g = g_ref[...]
    ffn = _load_token_tiles(y0_ref) * g[:, 0:1] + _load_token_tiles(y1_ref) * g[:, 1:2]
    o_ref[...] = _layer_norm(ALPHA * x_ref[...] + ffn, gain_ref[...], bias_ref[...])


def _combine_ln(x1, y2t, gates, gain, bias, tm):
    T = x1.shape[0]
    row = lambda i: (i, 0)
    const = lambda i: (0, 0)
    return pl.pallas_call(
        _combine_ln_kernel,
        grid=(T // tm,),
        in_specs=[pl.BlockSpec((tm, D_MODEL), row), pl.BlockSpec((tm * TOKEN_ROWS, LANES), row),
                  pl.BlockSpec((tm * TOKEN_ROWS, LANES), lambda i: (T // tm + i, 0)),
                  pl.BlockSpec((tm, LANES), row),
                  pl.BlockSpec((1, D_MODEL), const), pl.BlockSpec((1, D_MODEL), const)],
        out_specs=pl.BlockSpec((tm, D_MODEL), row),
        out_shape=jax.ShapeDtypeStruct((T, D_MODEL), F32),
        compiler_params=_params("arbitrary"),
        name="moe_combine_ln",
    )(x1, y2t, y2t, gates, gain, bias)


def _moe(x1, x1t, route_i, route_g, counts_f, w_gate, w_up, w_down, layer, gain, bias, tile, tm):
    T = x1.shape[0]
    n_pairs = 2 * T
    counts = counts_f[0, :N_EXPERTS].astype(I32)
    eids = jnp.arange(N_EXPERTS, dtype=I32)
    key = route_i[:, 0:2].reshape(-1) * n_pairs + jnp.arange(n_pairs, dtype=I32)
    pair = jnp.sort(key) % n_pairs
    pad = jnp.arange(tile, dtype=I32)
    src = jnp.concatenate([pair // 2, jnp.zeros((tile,), I32)])
    dst = jnp.concatenate([(pair % 2) * T + pair // 2, n_pairs + pad])
    tiles_per = (counts + tile - 1) // tile
    tile_ends = jnp.cumsum(tiles_per)
    first_row = jnp.cumsum(counts) - counts
    n_tiles = n_pairs // tile + N_EXPERTS
    tile_ids = jnp.arange(n_tiles + 2, dtype=I32)
    owner = jnp.sum((tile_ids[:, None] >= tile_ends[None, :]).astype(I32), axis=1)
    valid = owner < N_EXPERTS
    last_expert = jnp.max(jnp.where(counts > 0, eids, 0))
    tile_expert = jnp.minimum(owner, last_expert).astype(I32)
    onehot = owner[:, None] == eids[None, :]
    pick = lambda v: jnp.sum(jnp.where(onehot, v[None, :], 0), axis=1)
    tile_base = pick(first_row) + (tile_ids - pick(tile_ends - tiles_per)) * tile
    tile_base = jnp.where(valid, tile_base, n_pairs).astype(I32)
    tile_rows = jnp.clip(pick(first_row + counts) - tile_base, 0, tile).astype(I32)
    n_valid = tile_ends[-1:].astype(I32)
    y2t = _experts(x1t, src, dst, tile_expert[:n_tiles], tile_base, tile_rows, n_valid, w_gate, w_up, w_down,
                   layer, tile, n_pairs + 3 * tile)
    return _combine_ln(x1, y2t, route_g, gain, bias, tm)


def _pick(n, pref):
    t = min(n, pref)
    while n % t:
        t //= 2
    return t


def kernel(x, w_in_even, ret_decay_logit, ret_gn_gain, sink_logit, w_out_even, w_out_fourier,
           ln1_gain, ln1_bias, ln2_gain, ln2_bias, router_coarse_w, router_coarse_b,
           router_fine_w, router_fine_b, expert_w_gate, expert_w_up, expert_w_down):
    B, S, D = x.shape
    assert D == D_MODEL and S % (FFT_RADIX2 * 8) == 0 and S % WINDOW == 0 and S % RET_BLOCK == 0
    T = B * S
    tm = _pick(S, 512)
    tile = _pick(T, 512)
    rope_tabs = _rope_tables(S)
    fourier_tabs = _fourier_tables(S)
    row = lambda v: v.reshape(1, -1).astype(F32)

    x2 = x.reshape(T, D).astype(F32)
    for layer in range(DEPTH):
        if layer % 2 == 0:
            e = layer // 2
            qa, ka_t, va, ga, qb, kb2, vb2 = _inproj(x2, w_in_even[e].astype(BF16), rope_tabs, S, tm)
            shp = lambda v: v.reshape(B, S, v.shape[-1])
            ya = _retention(shp(qa), ka_t, shp(va), shp(ga), ret_decay_logit[e], ret_gn_gain[e])
            yb = _winattn(shp(qb), shp(kb2), shp(vb2), sink_logit[e], _pick(S // WINDOW, 4))
            a, b = ya.reshape(T, RET_WIDTH), yb.reshape(T, WIN_WIDTH)
            w_out = w_out_even[e]
        else:
            a = b = _fourier(x2.reshape(B, S, D), fourier_tabs)
            w_out = w_out_fourier[layer // 2]
        w_router = jnp.zeros((D, LANES), F32)
        w_router = w_router.at[:, :N_EXPERTS].set(router_fine_w[layer])
        w_router = w_router.at[:, N_EXPERTS:N_EXPERTS + N_GROUPS].set(router_coarse_w[layer])
        b_router = jnp.zeros((1, LANES), F32)
        b_router = b_router.at[0, :N_EXPERTS].set(router_fine_b[layer])
        b_router = b_router.at[0, N_EXPERTS:N_EXPERTS + N_GROUPS].set(router_coarse_b[layer])
        x1, x1t, route_i, route_g, counts = _proj_ln_route(
            a, b, x2, w_out.astype(BF16), row(ln1_gain[layer]), row(ln1_bias[layer]),
            w_router.astype(BF16), b_router, tm)
        x2 = _moe(x1, x1t, route_i, route_g, counts, expert_w_gate, expert_w_up, expert_w_down, layer,
                  row(ln2_gain[layer]), row(ln2_bias[layer]), tile, tm)
    return x2.reshape(B, S, D).astype(x.dtype)
```

```python
import math

import jax
import jax.numpy as jnp
from jax import lax
from jax.experimental import pallas as pl
from jax.experimental.pallas import tpu as pltpu

F32 = jnp.float32
BF16 = jnp.bfloat16
I32 = jnp.int32

D_MODEL = 1024
DEPTH = 4
RET_HEADS = 4
RET_DIM = 128
RET_BLOCK = 256
RET_WIDTH = RET_HEADS * RET_DIM
WIN_Q_HEADS = 8
WIN_KV_HEADS = 2
WIN_DIM = 64
WINDOW = 128
WIN_WIDTH = WIN_Q_HEADS * WIN_DIM
WIN_KV = WIN_KV_HEADS * WIN_DIM
FOURIER_GROUPS = 4
GROUP_CH = D_MODEL // FOURIER_GROUPS
FFT_RADIX2 = 16
ROPE_THETA = 10000.0
N_GROUPS = 4
EXPERTS_PER_GROUP = 8
N_EXPERTS = N_GROUPS * EXPERTS_PER_GROUP
EXPERT_HIDDEN = D_MODEL // 2
LN_EPS = 1e-5
GN_EPS = 1e-6
ALPHA = (2.0 * DEPTH) ** 0.25
IN_EVEN = 2 * RET_WIDTH + 2 * RET_WIDTH + WIN_WIDTH + 2 * WIN_KV

LANES = 128
TOKEN_ROWS = D_MODEL // LANES
VMEM_LIMIT_BYTES = 48 * 1024 * 1024
NEG_MASK = -1e30
ROW_DMA_UNROLL = 8

_NT = (((1,), (1,)), ((), ()))


def _params(*sem):
    return pltpu.CompilerParams(dimension_semantics=sem, vmem_limit_bytes=VMEM_LIMIT_BYTES)


def _silu(v):
    return v / (1.0 + jnp.exp(-v))


def _layer_norm(z, gain, bias):
    mu = jnp.mean(z, axis=-1, keepdims=True)
    zc = z - mu
    var = jnp.mean(zc * zc, axis=-1, keepdims=True)
    return zc * lax.rsqrt(var + LN_EPS) * gain + bias


def _store_token_tiles(ref, v):
    n = v.shape[0]
    for c in range(TOKEN_ROWS):
        ref[pl.ds(c, n, stride=TOKEN_ROWS), :] = v[:, c * LANES:(c + 1) * LANES]


def _load_token_tiles(ref):
    n = ref.shape[0] // TOKEN_ROWS
    return jnp.concatenate([ref[pl.ds(c, n, stride=TOKEN_ROWS), :] for c in range(TOKEN_ROWS)], axis=1)


def _rope128(h, cos, sin_signed):
    return h * cos + pltpu.roll(h, 64, 1) * sin_signed


def _rope64x2(h, cos, sin_lo, sin_hi):
    return h * cos + pltpu.roll(h, 96, 1) * sin_lo + pltpu.roll(h, 32, 1) * sin_hi


def _inproj_kernel(x_ref, w_ref, cr_ref, sr_ref, cw_ref, slo_ref, shi_ref,
                   qa_ref, ka_ref, va_ref, ga_ref, qb_ref, kb_ref, vb_ref):
    xb = x_ref[...].astype(BF16)

    def seg(lo, hi):
        return jnp.dot(xb, w_ref[:, lo:hi], preferred_element_type=F32)

    cr, sr = cr_ref[...], sr_ref[...]
    q = seg(0, RET_WIDTH)
    k = seg(RET_WIDTH, 2 * RET_WIDTH)
    for h in range(RET_HEADS):
        sl = slice(LANES * h, LANES * (h + 1))
        qa_ref[:, sl] = _rope128(q[:, sl], cr, sr).astype(BF16)
        ka_ref[sl, :] = (_rope128(k[:, sl], cr, sr) * RET_DIM ** -0.5).T.astype(BF16)
    va_ref[...] = seg(2 * RET_WIDTH, 3 * RET_WIDTH).astype(BF16)
    ga_ref[...] = _silu(seg(3 * RET_WIDTH, 4 * RET_WIDTH)).astype(BF16)
    cw, slo, shi = cw_ref[...], slo_ref[...], shi_ref[...]
    base = 4 * RET_WIDTH
    q = seg(base, base + WIN_WIDTH)
    for p in range(WIN_WIDTH // LANES):
        sl = slice(LANES * p, LANES * (p + 1))
        qb_ref[:, sl] = (_rope64x2(q[:, sl], cw, slo, shi) * WIN_DIM ** -0.5).astype(BF16)
    kv = seg(base + WIN_WIDTH, base + WIN_WIDTH + 2 * WIN_KV)
    kb = _rope64x2(kv[:, :WIN_KV], cw, slo, shi)
    vb = kv[:, WIN_KV:]
    kb_ref[:, :WIN_KV] = kb.astype(BF16)
    kb_ref[:, WIN_KV:] = pltpu.roll(kb, WIN_DIM, 1).astype(BF16)
    vb_ref[:, :WIN_KV] = vb.astype(BF16)
    vb_ref[:, WIN_KV:] = pltpu.roll(vb, WIN_DIM, 1).astype(BF16)


def _rope_tables(seq):
    pos = jnp.arange(seq, dtype=F32)[:, None]
    half = RET_DIM // 2
    inv = ROPE_THETA ** (-jnp.arange(half, dtype=F32) / half)
    ang = pos * inv[None, :]
    cr = jnp.concatenate([jnp.cos(ang), jnp.cos(ang)], axis=1)
    sr = jnp.concatenate([-jnp.sin(ang), jnp.sin(ang)], axis=1)
    half = WIN_DIM // 2
    inv = ROPE_THETA ** (-jnp.arange(half, dtype=F32) / half)
    ang = pos * inv[None, :]
    c, s, z = jnp.cos(ang), jnp.sin(ang), jnp.zeros_like(ang)
    cw = jnp.concatenate([c, c, c, c], axis=1)
    slo = jnp.concatenate([-s, z, -s, z], axis=1)
    shi = jnp.concatenate([z, s, z, s], axis=1)
    return cr, sr, cw, slo, shi


def _inproj(x2, w_bf16, tables, seq, tm):
    T = x2.shape[0]
    nseq = seq // tm
    row = lambda i: (i, 0)
    tab = lambda i: (i % nseq, 0)
    widths = (RET_WIDTH, None, RET_WIDTH, RET_WIDTH, WIN_WIDTH, 2 * WIN_KV, 2 * WIN_KV)
    return pl.pallas_call(
        _inproj_kernel,
        grid=(T // tm,),
        in_specs=[pl.BlockSpec((tm, D_MODEL), row),
                  pl.BlockSpec((D_MODEL, IN_EVEN), lambda i: (0, 0))]
                 + [pl.BlockSpec((tm, LANES), tab)] * 5,
        out_specs=[pl.BlockSpec((tm, w), row) if w else pl.BlockSpec((RET_WIDTH, tm), lambda i: (0, i))
                   for w in widths],
        out_shape=[jax.ShapeDtypeStruct((T, w) if w else (RET_WIDTH, T), BF16) for w in widths],
        compiler_params=_params("arbitrary"),
        name="inproj_rope",
    )(x2, w_bf16, *tables)


def _retention_kernel(logit_ref, q_ref, kt_ref, v_ref, g_ref, gain_ref, o_ref,
                      dmat_scr, kv_scr, state_scr):
    h = pl.program_id(1)
    C = RET_BLOCK
    dk = RET_DIM
    n_chunks = q_ref.shape[1] // C

    def log_gamma(d):
        v = jnp.full((1, 1), logit_ref[d, h], F32)
        return -(jnp.maximum(-v, 0.0) + jnp.log(1.0 + jnp.exp(-jnp.abs(v))))

    lgf, lgb = log_gamma(0), log_gamma(1)
    i = lax.broadcasted_iota(I32, (C, C), 0)
    j = lax.broadcasted_iota(I32, (C, C), 1)
    diff = (i - j).astype(F32)
    dmat_scr[...] = jnp.where(diff >= 0, jnp.exp(lgf * jnp.maximum(diff, 0.0)),
                              jnp.exp(lgb * jnp.maximum(-diff, 0.0)))
    col = lax.broadcasted_iota(I32, (C, 1), 0).astype(F32)
    lane = lax.broadcasted_iota(I32, (1, C), 1).astype(F32)
    xi_f, xi_b = jnp.exp(lgf * (col + 1.0)), jnp.exp(lgb * (C - col))
    zeta_f, zeta_b = jnp.exp(lgf * (C - 1.0 - lane)), jnp.exp(lgb * lane)
    dec_f, dec_b = jnp.exp(lgf * C), jnp.exp(lgb * C)
    gain = gain_ref[...]

    def span(n):
        return pl.ds(pl.multiple_of(n * C, C), C)

    def kv_pass(n, carry):
        kt = kt_ref[:, span(n)].astype(F32)
        lhs = jnp.concatenate([(kt * zeta_f).astype(BF16), (kt * zeta_b).astype(BF16)], axis=0)
        kv_scr[n] = jnp.dot(lhs, v_ref[0, span(n), :], preferred_element_type=F32)
        return carry

    lax.fori_loop(0, n_chunks, kv_pass, 0, unroll=4)

    def scan_f(n, state):
        state_scr[n, :dk, :] = state.astype(BF16)
        return state * dec_f + kv_scr[n, :dk, :]

    def scan_b(t, state):
        n = n_chunks - 1 - t
        state_scr[n, dk:, :] = state.astype(BF16)
        return state * dec_b + kv_scr[n, dk:, :]

    zero = jnp.zeros((dk, dk), F32)
    lax.fori_loop(0, n_chunks, scan_f, zero)
    lax.fori_loop(0, n_chunks, scan_b, zero)

    def out_pass(n, carry):
        q = q_ref[0, span(n), :]
        s = jnp.dot(q, kt_ref[:, span(n)], preferred_element_type=F32)
        p = (s * dmat_scr[...]).astype(BF16)
        qf = q.astype(F32)
        qx = jnp.concatenate([(qf * xi_f).astype(BF16), (qf * xi_b).astype(BF16)], axis=1)
        y = (jnp.dot(p, v_ref[0, span(n), :], preferred_element_type=F32)
             + jnp.dot(qx, state_scr[n], preferred_element_type=F32))
        mu = jnp.mean(y, axis=-1, keepdims=True)
        yc = y - mu
        var = jnp.mean(yc * yc, axis=-1, keepdims=True)
        yn = yc * lax.rsqrt(var + GN_EPS) * gain
        o_ref[0, span(n), :] = (g_ref[0, span(n), :].astype(F32) * yn).astype(BF16)
        return carry

    lax.fori_loop(0, n_chunks, out_pass, 0, unroll=4)


def _retention(qa, ka_t, va, ga, decay_logit, gn_gain):
    B, S, _ = qa.shape
    n_chunks = S // RET_BLOCK
    head = pl.BlockSpec((1, S, RET_DIM), lambda b, h: (b, 0, h))
    return pl.pallas_call(
        _retention_kernel,
        grid=(B, RET_HEADS),
        in_specs=[pl.BlockSpec(memory_space=pltpu.SMEM), head,
                  pl.BlockSpec((RET_DIM, S), lambda b, h: (h, b)), head, head,
                  pl.BlockSpec((1, RET_DIM), lambda b, h: (0, h))],
        out_specs=head,
        out_shape=jax.ShapeDtypeStruct((B, S, RET_WIDTH), BF16),
        scratch_shapes=[pltpu.VMEM((RET_BLOCK, RET_BLOCK), F32),
                        pltpu.VMEM((n_chunks, 2 * RET_DIM, RET_DIM), F32),
                        pltpu.VMEM((n_chunks, 2 * RET_DIM, RET_DIM), BF16)],
        compiler_params=_params("arbitrary", "arbitrary"),
        name="retention_gn_gate",
    )(decay_logit.astype(F32), qa, ka_t, va, ga, gn_gain.reshape(1, RET_WIDTH).astype(F32))


def _winattn_kernel(sink_ref, q_ref, kp_ref, kc_ref, kn_ref, vp_ref, vc_ref, vn_ref, o_ref):
    step = pl.program_id(1)
    n_steps = pl.num_programs(1)
    W = WINDOW
    nq = q_ref.shape[1] // W
    group = WIN_Q_HEADS // WIN_KV_HEADS
    k_all = jnp.concatenate([kp_ref[0], kc_ref[0], kn_ref[0]], axis=0)
    v_all = jnp.concatenate([vp_ref[0], vc_ref[0], vn_ref[0]], axis=0)
    lo_half = lax.broadcasted_iota(I32, (1, LANES), 1) < WIN_DIM

    def placements(slab, g):
        first, second = slab[:, :LANES], slab[:, LANES:]
        zero = jnp.zeros_like(first)
        if g == 0:
            return jnp.where(lo_half, first, zero), jnp.where(lo_half, zero, second)
        return jnp.where(lo_half, second, zero), jnp.where(lo_half, zero, first)

    rows2 = lax.broadcasted_iota(I32, (2 * W, 1), 0)
    qi = lax.broadcasted_iota(I32, (2 * W, 3 * W), 0) & (W - 1)
    cj = lax.broadcasted_iota(I32, (2 * W, 3 * W), 1)
    rel = cj - qi
    band = (rel >= 0) & (rel <= 2 * W)

    for g in range(WIN_KV_HEADS):
        k_even, k_odd = placements(k_all, g)
        v_even, v_odd = placements(v_all, g)
        c0 = group * WIN_DIM * g
        sink_even = jnp.where(rows2 < W, sink_ref[0, group * g], sink_ref[0, group * g + 2])
        sink_odd = jnp.where(rows2 < W, sink_ref[0, group * g + 1], sink_ref[0, group * g + 3])
        for jq in range(nq):
            qrows = slice(jq * W, (jq + 1) * W)
            krows = slice(jq * W, (jq + 3) * W)
            mask = band
            if jq == 0:
                mask = mask & (cj >= jnp.where(step == 0, W, 0))
            if jq == nq - 1:
                mask = mask & (cj < jnp.where(step == n_steps - 1, 2 * W, 3 * W))
            q2 = jnp.concatenate([q_ref[0, qrows, c0:c0 + LANES],
                                  q_ref[0, qrows, c0 + LANES:c0 + 2 * LANES]], axis=0)

            def softmax_parts(k_placed, sink):
                s = lax.dot_general(q2, k_placed[krows], _NT, preferred_element_type=F32)
                s = jnp.where(mask, s, NEG_MASK)
                m = jnp.maximum(jnp.max(s, axis=1, keepdims=True), sink)
                e = jnp.exp(s - m)
                return e.astype(BF16), jnp.sum(e, axis=1, keepdims=True) + jnp.exp(sink - m)

            p_even, den_even = softmax_parts(k_even, sink_even)
            p_odd, den_odd = softmax_parts(k_odd, sink_odd)
            o = (jnp.dot(p_even, v_even[krows], preferred_element_type=F32)
                 + jnp.dot(p_odd, v_odd[krows], preferred_element_type=F32))
            o = (o / jnp.where(lo_half, den_even, den_odd)).astype(BF16)
            o_ref[0, qrows, c0:c0 + LANES] = o[:W]
            o_ref[0, qrows, c0 + LANES:c0 + 2 * LANES] = o[W:]


def _winattn(qb, kb2, vb2, sink_logit, blocks_per_step):
    B, S, _ = qb.shape
    nq = blocks_per_step
    nb = S // WINDOW
    prev = pl.BlockSpec((1, WINDOW, 2 * WIN_KV), lambda b, s: (b, jnp.maximum(s * nq - 1, 0), 0))
    cur = pl.BlockSpec((1, nq * WINDOW, 2 * WIN_KV), lambda b, s: (b, s, 0))
    nxt = pl.BlockSpec((1, WINDOW, 2 * WIN_KV), lambda b, s: (b, jnp.minimum((s + 1) * nq, nb - 1), 0))
    qspec = pl.BlockSpec((1, nq * WINDOW, WIN_WIDTH), lambda b, s: (b, s, 0))
    return pl.pallas_call(
        _winattn_kernel,
        grid=(B, nb // nq),
        in_specs=[pl.BlockSpec(memory_space=pltpu.SMEM), qspec, prev, cur, nxt, prev, cur, nxt],
        out_specs=qspec,
        out_shape=jax.ShapeDtypeStruct((B, S, WIN_WIDTH), BF16),
        compiler_params=_params("arbitrary", "arbitrary"),
        name="window_attention",
    )(sink_logit.reshape(1, WIN_Q_HEADS).astype(F32), qb, kb2, kb2, kb2, vb2, vb2, vb2)


def _chan_dft_kernel(xa_ref, xb_ref, m_ref, wr_ref, wi_ref):
    n1 = xa_ref.shape[1] // FFT_RADIX2
    for s in range(FFT_RADIX2):
        rows = pl.ds(s, n1, stride=FFT_RADIX2)
        w = (jnp.dot(xa_ref[0, rows, :].astype(BF16), m_ref[:LANES, :], preferred_element_type=F32)
             + jnp.dot(xb_ref[0, rows, :].astype(BF16), m_ref[LANES:, :], preferred_element_type=F32))
        wr_ref[0, s] = w[:, :GROUP_CH].astype(BF16)
        wi_ref[0, s] = w[:, GROUP_CH:].astype(BF16)


def _real_dft16(ar, ai):
    n = FFT_RADIX2
    cs = [math.cos(2 * math.pi * m / n) for m in range(n)]
    sn = [math.sin(2 * math.pi * m / n) for m in range(n)]

    def axpy(acc, coef, v):
        if abs(coef) < 1e-12:
            return acc
        if abs(abs(coef) - 1.0) < 1e-12:
            if acc is None:
                return v if coef > 0 else -v
            return acc + v if coef > 0 else acc - v
        t = coef * v
        return t if acc is None else acc + t

    p = {s: ar[s] + ar[n - s] for s in range(1, n // 2)}
    d = {s: ai[s] - ai[n - s] for s in range(1, n // 2)}
    base = (ar[0] + ar[n // 2], ar[0] - ar[n // 2])
    y = [None] * n
    for k in range(n // 2 + 1):
        e = base[k % 2]
        for s in range(1, n // 2):
            e = axpy(e, cs[(s * k) % n], p[s])
        if k in (0, n // 2):
            y[k] = e
            continue
        o = None
        for s in range(1, n // 2):
            o = axpy(o, sn[(s * k) % n], d[s])
        y[k] = e + o
        y[n - k] = e - o
    return y


def _seq_dft_kernel(wr_ref, wi_ref, cf_ref, sf_ref, y_ref, ar_scr, ai_scr):
    n1 = wr_ref.shape[2]
    for s in range(FFT_RADIX2):
        cf, sf, wr, wi = cf_ref[s], sf_ref[s], wr_ref[0, s], wi_ref[0, s]
        ar_scr[s] = (jnp.dot(cf, wr, preferred_element_type=F32)
                     + jnp.dot(sf, wi, preferred_element_type=F32))
        ai_scr[s] = (jnp.dot(cf, wi, preferred_element_type=F32)
                     - jnp.dot(sf, wr, preferred_element_type=F32))

    def rows8(r, carry):
        rows = pl.ds(pl.multiple_of(r * 8, 8), 8)
        y = _real_dft16([ar_scr[s, rows, :] for s in range(FFT_RADIX2)],
                        [ai_scr[s, rows, :] for s in range(FFT_RADIX2)])
        for k in range(FFT_RADIX2):
            y_ref[0, k, rows, :] = y[k].astype(BF16)
        return carry

    lax.fori_loop(0, n1 // 8, rows8, 0)


def _fourier_tables(seq):
    n1 = seq // FFT_RADIX2
    c = jnp.arange(GROUP_CH, dtype=I32)
    m = (c[:, None] * c[None, :]) % GROUP_CH
    ang = m.astype(F32) * (2.0 * math.pi / GROUP_CH)
    scale = (seq * GROUP_CH) ** -0.5
    chan = (jnp.concatenate([jnp.cos(ang), -jnp.sin(ang)], axis=1) * scale).astype(BF16)
    k1 = jnp.arange(n1, dtype=I32)[None, :, None]
    s1 = jnp.arange(n1, dtype=I32)[None, None, :]
    s2 = jnp.arange(FFT_RADIX2, dtype=I32)[:, None, None]
    m = (FFT_RADIX2 * s1 * k1 + s2 * k1) % seq
    ang = m.astype(F32) * (2.0 * math.pi / seq)
    return chan, jnp.cos(ang).astype(BF16), jnp.sin(ang).astype(BF16)


def _fourier(x, tables, tn=256):
    B, S, _ = x.shape
    n1 = S // FFT_RADIX2
    chan, cf, sf = tables
    wshape = jax.ShapeDtypeStruct((B, FFT_RADIX2, n1, D_MODEL), BF16)
    wblk = pl.BlockSpec((1, FFT_RADIX2, n1, GROUP_CH), lambda b, g: (b, 0, 0, g))
    wr, wi = pl.pallas_call(
        _chan_dft_kernel,
        grid=(B, FOURIER_GROUPS),
        in_specs=[pl.BlockSpec((1, S, LANES), lambda b, g: (b, 0, 2 * g)),
                  pl.BlockSpec((1, S, LANES), lambda b, g: (b, 0, 2 * g + 1)),
                  pl.BlockSpec((GROUP_CH, 2 * GROUP_CH), lambda b, g: (0, 0))],
        out_specs=[wblk, wblk],
        out_shape=[wshape, wshape],
        compiler_params=_params("arbitrary", "arbitrary"),
        name="fourier_channel_dft",
    )(x, x, chan)
    cols = pl.BlockSpec((1, FFT_RADIX2, n1, tn), lambda b, j: (b, 0, 0, j))
    full = pl.BlockSpec((FFT_RADIX2, n1, n1), lambda b, j: (0, 0, 0))
    y = pl.pallas_call(
        _seq_dft_kernel,
        grid=(B, D_MODEL // tn),
        in_specs=[cols, cols, full, full],
        out_specs=cols,
        out_shape=wshape,
        scratch_shapes=[pltpu.VMEM((FFT_RADIX2, n1, tn), F32), pltpu.VMEM((FFT_RADIX2, n1, tn), F32)],
        compiler_params=_params("arbitrary", "arbitrary"),
        name="fourier_sequence_dft",
    )(wr, wi, cf, sf)
    return y.reshape(B * S, D_MODEL)


def _proj_ln_route_kernel(a_ref, b_ref, x_ref, w_ref, gain_ref, bias_ref, wr_ref, br_ref,
                          xt_ref, ri_ref, rg_ref, cnt_ref, carry_scr):
    tm = x_ref.shape[0]
    half = a_ref.shape[1]

    @pl.when(pl.program_id(0) == 0)
    def _():
        carry_scr[...] = jnp.zeros_like(carry_scr)

    mix = (jnp.dot(a_ref[...], w_ref[:half, :], preferred_element_type=F32)
           + jnp.dot(b_ref[...], w_ref[half:, :], preferred_element_type=F32))
    y = _layer_norm(ALPHA * x_ref[...] + mix, gain_ref[...], bias_ref[...])
    _store_token_tiles(xt_ref, y)

    logits = jnp.dot(y.astype(BF16), wr_ref[...], preferred_element_type=F32) + br_ref[...]
    lane = lax.broadcasted_iota(I32, (tm, LANES), 1)
    lanef = lane.astype(F32)
    ninf = -jnp.inf
    big = 1e9

    def rmax(v):
        return jnp.max(v, axis=1, keepdims=True)

    def first_lane(hit):
        return jnp.min(jnp.where(hit, lanef, big), axis=1, keepdims=True)

    cmask = (lane >= N_EXPERTS) & (lane < N_EXPERTS + N_GROUPS)
    cl = jnp.where(cmask, logits, ninf)
    cmax = rmax(cl)
    group = first_lane(cl == cmax) - float(N_EXPERTS)
    p_group = 1.0 / jnp.sum(jnp.where(cmask, jnp.exp(cl - cmax), 0.0), axis=1, keepdims=True)
    lo = group * float(EXPERTS_PER_GROUP)
    fmask = (lanef >= lo) & (lanef < lo + float(EXPERTS_PER_GROUP))
    fl = jnp.where(fmask, logits, ninf)
    v1 = rmax(fl)
    e1 = first_lane(fl == v1)
    fl2 = jnp.where(lanef == e1, ninf, fl)
    v2 = rmax(fl2)
    e2 = first_lane(fl2 == v2)
    t = jnp.exp(v2 - v1)
    g1 = p_group / (1.0 + t)
    g2 = p_group * t / (1.0 + t)

    onehot = jnp.where((lanef == e1) | (lanef == e2), 1.0, 0.0)
    carry_scr[...] += jnp.sum(onehot, axis=0, keepdims=True)
    cnt_ref[...] = carry_scr[...]
    ri_ref[...] = jnp.where(lane == 0, e1, e2).astype(I32)
    rg_ref[...] = jnp.where(lane == 0, g1, g2)


def _proj_ln_route(a, b, x2, w_bf16, gain, bias, w_router, b_router, tm):
    T = x2.shape[0]
    half = D_MODEL // 2
    row = lambda i: (i, 0)
    const = lambda i: (0, 0)
    a_spec = pl.BlockSpec((tm, half), row)
    b_spec = pl.BlockSpec((tm, half), row if b is not a else (lambda i: (i, 1)))
    return pl.pallas_call(
        _proj_ln_route_kernel,
        grid=(T // tm,),
        in_specs=[a_spec, b_spec, pl.BlockSpec((tm, D_MODEL), row),
                  pl.BlockSpec((D_MODEL, D_MODEL), const),
                  pl.BlockSpec((1, D_MODEL), const), pl.BlockSpec((1, D_MODEL), const),
                  pl.BlockSpec((D_MODEL, LANES), const), pl.BlockSpec((1, LANES), const)],
        out_specs=[pl.BlockSpec((tm * TOKEN_ROWS, LANES), row), pl.BlockSpec((tm, LANES), row),
                   pl.BlockSpec((tm, LANES), row), pl.BlockSpec((1, LANES), const)],
        out_shape=[jax.ShapeDtypeStruct((T * TOKEN_ROWS, LANES), F32), jax.ShapeDtypeStruct((T, LANES), I32),
                   jax.ShapeDtypeStruct((T, LANES), F32), jax.ShapeDtypeStruct((1, LANES), F32)],
        scratch_shapes=[pltpu.VMEM((1, LANES), F32)],
        compiler_params=_params("arbitrary"),
        name="proj_ln_router",
    )(a, b, x2, w_bf16, gain, bias, w_router, b_router)


def _expert_kernel(te_ref, base_ref, rows_ref, nv_ref, src_ref, dst_ref, x_hbm, wg_ref, wu_ref, wd_ref,
                   y_hbm, wg_scr, wu_scr, wd_scr, xbuf0, xbuf1, xbuf2, obuf0, obuf1, obuf2, gsem, ssem):
    i = pl.program_id(0)
    nv = nv_ref[0]
    tile = xbuf0.shape[0] // TOKEN_ROWS
    xbufs, obufs = (xbuf0, xbuf1, xbuf2), (obuf0, obuf1, obuf2)
    n_buf = len(xbufs)
    spare0 = dst_ref.shape[0] - tile

    def token_rows(t):
        return pl.ds(pl.multiple_of(t * TOKEN_ROWS, TOKEN_ROWS), TOKEN_ROWS)

    def gather_row(base, r, q):
        return pltpu.make_async_copy(x_hbm.at[token_rows(src_ref[base + r]), :],
                                     xbufs[q].at[token_rows(r), :], gsem.at[q])

    def scatter_row(base, n_rows, r, q):
        row = jnp.where(r < n_rows, dst_ref[base + r], spare0 + q * tile + r)
        return pltpu.make_async_copy(obufs[q].at[token_rows(r), :], y_hbm.at[token_rows(row), :], ssem.at[q])

    def wait_gather(q):
        pltpu.make_async_copy(x_hbm.at[pl.ds(0, tile * TOKEN_ROWS), :], xbufs[q], gsem.at[q]).wait()

    def wait_scatter(q):
        pltpu.make_async_copy(obufs[q], y_hbm.at[pl.ds(0, tile * TOKEN_ROWS), :], ssem.at[q]).wait()

    @pl.when(i == 0)
    def _():
        obufs[n_buf - 1][...] = jnp.zeros_like(obufs[n_buf - 1])
        for q in range(n_buf - 1):
            rows = pl.ds((spare0 + q * tile) * TOKEN_ROWS, tile * TOKEN_ROWS)
            fill = pltpu.make_async_copy(obufs[n_buf - 1], y_hbm.at[rows, :], ssem.at[q])
            fill.start()
            fill.wait()

        def first(r, carry):
            gather_row(base_ref[0], r, 0).start()
            gather_row(base_ref[1], r, 1).start()
            return carry

        lax.fori_loop(0, tile, first, 0, unroll=ROW_DMA_UNROLL)

    @pl.when(i < nv)
    def _():
        @pl.when((i == 0) | (te_ref[i] != te_ref[jnp.maximum(i - 1, 0)]))
        def _():
            wg_scr[...] = wg_ref[0, 0].astype(BF16)
            wu_scr[...] = wu_ref[0, 0].astype(BF16)
            wd_scr[...] = wd_ref[0, 0].astype(BF16)

        base_ahead = base_ref[i + 2]
        prev = jnp.maximum(i - 1, 0)
        base_prev = base_ref[prev]
        rows_prev = jnp.where(i >= 1, rows_ref[prev], 0)
        for q in range(n_buf):
            @pl.when(i % n_buf == q)
            def _():
                ahead, behind = (q + 2) % n_buf, (q + n_buf - 1) % n_buf
                wait_gather(q)

                @pl.when(i >= 2)
                def _():
                    wait_scatter(q)

                for r in range(tile):
                    gather_row(base_ahead, r, ahead).start(priority=r % 2)
                    scatter_row(base_prev, rows_prev, r, behind).start(priority=(r + 1) % 2)
                xb = _load_token_tiles(xbufs[q]).astype(BF16)
                hg = jnp.dot(xb, wg_scr[...], preferred_element_type=F32)
                hu = jnp.dot(xb, wu_scr[...], preferred_element_type=F32)
                hid = (_silu(hg) * hu).astype(BF16)
                _store_token_tiles(obufs[q], jnp.dot(hid, wd_scr[...], preferred_element_type=F32))

                @pl.when(i == nv - 1)
                def _():
                    wait_gather((q + 1) % n_buf)
                    wait_gather(ahead)
                    wait_scatter(behind)

                    @pl.when(i >= 1)
                    def _():
                        wait_scatter((q + 1) % n_buf)

                    def last(r, carry):
                        scatter_row(base_ref[i], rows_ref[i], r, q).start()
                        return carry

                    lax.fori_loop(0, tile, last, 0, unroll=ROW_DMA_UNROLL)
                    wait_scatter(q)


def _experts(x1t, src, dst, tile_expert, tile_base, tile_rows, n_valid, w_gate, w_up, w_down, layer, tile,
             n_out):
    n_tiles = tile_expert.shape[0]
    wsel = lambda i, te, *_: (layer, te[i], 0, 0)
    buf = pltpu.VMEM((tile * TOKEN_ROWS, LANES), F32)
    return pl.pallas_call(
        _expert_kernel,
        grid_spec=pltpu.PrefetchScalarGridSpec(
            num_scalar_prefetch=6,
            grid=(n_tiles,),
            in_specs=[pl.BlockSpec(memory_space=pl.ANY),
                      pl.BlockSpec((1, 1, D_MODEL, EXPERT_HIDDEN), wsel),
                      pl.BlockSpec((1, 1, D_MODEL, EXPERT_HIDDEN), wsel),
                      pl.BlockSpec((1, 1, EXPERT_HIDDEN, D_MODEL), wsel)],
            out_specs=pl.BlockSpec(memory_space=pl.ANY),
            scratch_shapes=[pltpu.VMEM((D_MODEL, EXPERT_HIDDEN), BF16),
                            pltpu.VMEM((D_MODEL, EXPERT_HIDDEN), BF16),
                            pltpu.VMEM((EXPERT_HIDDEN, D_MODEL), BF16),
                            buf, buf, buf, buf, buf, buf,
                            pltpu.SemaphoreType.DMA((3,)), pltpu.SemaphoreType.DMA((3,))],
        ),
        out_shape=jax.ShapeDtypeStruct((n_out * TOKEN_ROWS, LANES), F32),
        compiler_params=_params("arbitrary"),
        name="moe_experts",
    )(tile_expert, tile_base, tile_rows, n_valid, src, dst, x1t, w_gate, w_up, w_down)


def _combine_ln_kernel(x_ref, y0_ref, y1_ref, g_ref, gain_ref, bias_ref, o_ref):
    g = g_ref[...]
    ffn = _load_token_tiles(y0_ref) * g[:, 0:1] + _load_token_tiles(y1_ref) * g[:, 1:2]
    o_ref[...] = _layer_norm(ALPHA * _load_token_tiles(x_ref) + ffn, gain_ref[...], bias_ref[...])


def _combine_ln(x1t, y2t, gates, gain, bias, tm):
    T = x1t.shape[0] // TOKEN_ROWS
    row = lambda i: (i, 0)
    const = lambda i: (0, 0)
    return pl.pallas_call(
        _combine_ln_kernel,
        grid=(T // tm,),
        in_specs=[pl.BlockSpec((tm * TOKEN_ROWS, LANES), row), pl.BlockSpec((tm * TOKEN_ROWS, LANES), row),
                  pl.BlockSpec((tm * TOKEN_ROWS, LANES), lambda i: (T // tm + i, 0)),
                  pl.BlockSpec((tm, LANES), row),
                  pl.BlockSpec((1, D_MODEL), const), pl.BlockSpec((1, D_MODEL), const)],
        out_specs=pl.BlockSpec((tm, D_MODEL), row),
        out_shape=jax.ShapeDtypeStruct((T, D_MODEL), F32),
        compiler_params=_params("arbitrary"),
        name="moe_combine_ln",
    )(x1t, y2t, y2t, gates, gain, bias)


def _moe(x1t, route_i, route_g, counts_f, w_gate, w_up, w_down, layer, gain, bias, tile, tm):
    T = x1t.shape[0] // TOKEN_ROWS
    n_pairs = 2 * T
    counts = counts_f[0, :N_EXPERTS].astype(I32)
    eids = jnp.arange(N_EXPERTS, dtype=I32)
    key = route_i[:, 0:2].reshape(-1) * n_pairs + jnp.arange(n_pairs, dtype=I32)
    pair = jnp.sort(key) % n_pairs
    pad = jnp.arange(tile, dtype=I32)
    src = jnp.concatenate([pair // 2, jnp.zeros((tile,), I32)])
    dst = jnp.concatenate([(pair % 2) * T + pair // 2, n_pairs + pad])
    tiles_per = (counts + tile - 1) // tile
    tile_ends = jnp.cumsum(tiles_per)
    first_row = jnp.cumsum(counts) - counts
    n_tiles = n_pairs // tile + N_EXPERTS
    tile_ids = jnp.arange(n_tiles + 2, dtype=I32)
    owner = jnp.sum((tile_ids[:, None] >= tile_ends[None, :]).astype(I32), axis=1)
    valid = owner < N_EXPERTS
    last_expert = jnp.max(jnp.where(counts > 0, eids, 0))
    tile_expert = jnp.minimum(owner, last_expert).astype(I32)
    onehot = owner[:, None] == eids[None, :]
    pick = lambda v: jnp.sum(jnp.where(onehot, v[None, :], 0), axis=1)
    tile_base = pick(first_row) + (tile_ids - pick(tile_ends - tiles_per)) * tile
    tile_base = jnp.where(valid, tile_base, n_pairs).astype(I32)
    tile_rows = jnp.clip(pick(first_row + counts) - tile_base, 0, tile).astype(I32)
    n_valid = tile_ends[-1:].astype(I32)
    y2t = _experts(x1t, src, dst, tile_expert[:n_tiles], tile_base, tile_rows, n_valid, w_gate, w_up, w_down,
                   layer, tile, n_pairs + 3 * tile)
    return _combine_ln(x1t, y2t, route_g, gain, bias, tm)


def _pick(n, pref):
    t = min(n, pref)
    while n % t:
        t //= 2
    return t


def kernel(x, w_in_even, ret_decay_logit, ret_gn_gain, sink_logit, w_out_even, w_out_fourier,
           ln1_gain, ln1_bias, ln2_gain, ln2_bias, router_coarse_w, router_coarse_b,
           router_fine_w, router_fine_b, expert_w_gate, expert_w_up, expert_w_down):
    B, S, D = x.shape
    assert D == D_MODEL and S % (FFT_RADIX2 * 8) == 0 and S % WINDOW == 0 and S % RET_BLOCK == 0
    T = B * S
    tm = _pick(S, 512)
    tile = _pick(T, 512)
    rope_tabs = _rope_tables(S)
    fourier_tabs = _fourier_tables(S)
    row = lambda v: v.reshape(1, -1).astype(F32)

    x2 = x.reshape(T, D).astype(F32)
    for layer in range(DEPTH):
        if layer % 2 == 0:
            e = layer // 2
            qa, ka_t, va, ga, qb, kb2, vb2 = _inproj(x2, w_in_even[e].astype(BF16), rope_tabs, S, tm)
            shp = lambda v: v.reshape(B, S, v.shape[-1])
            ya = _retention(shp(qa), ka_t, shp(va), shp(ga), ret_decay_logit[e], ret_gn_gain[e])
            yb = _winattn(shp(qb), shp(kb2), shp(vb2), sink_logit[e], _pick(S // WINDOW, 4))
            a, b = ya.reshape(T, RET_WIDTH), yb.reshape(T, WIN_WIDTH)
            w_out = w_out_even[e]
        else:
            a = b = _fourier(x2.reshape(B, S, D), fourier_tabs)
            w_out = w_out_fourier[layer // 2]
        w_router = jnp.zeros((D, LANES), F32)
        w_router = w_router.at[:, :N_EXPERTS].set(router_fine_w[layer])
        w_router = w_router.at[:, N_EXPERTS:N_EXPERTS + N_GROUPS].set(router_coarse_w[layer])
        b_router = jnp.zeros((1, LANES), F32)
        b_router = b_router.at[0, :N_EXPERTS].set(router_fine_b[layer])
        b_router = b_router.at[0, N_EXPERTS:N_EXPERTS + N_GROUPS].set(router_coarse_b[layer])
        x1t, route_i, route_g, counts = _proj_ln_route(
            a, b, x2, w_out.astype(BF16), row(ln1_gain[layer]), row(ln1_bias[layer]),
            w_router.astype(BF16), b_router, tm)
        x2 = _moe(x1t, route_i, route_g, counts, expert_w_gate, expert_w_up, expert_w_down, layer,
                  row(ln2_gain[layer]), row(ln2_bias[layer]), tile, tm)
    return x2.reshape(B, S, D).astype(x.dtype)
```

```python
import math

import jax
import jax.numpy as jnp
from jax import lax
from jax.experimental import pallas as pl
from jax.experimental.pallas import tpu as pltpu

F32 = jnp.float32
BF16 = jnp.bfloat16
I32 = jnp.int32

D_MODEL = 1024
DEPTH = 4
RET_HEADS = 4
RET_DIM = 128
RET_BLOCK = 256
RET_WIDTH = RET_HEADS * RET_DIM
WIN_Q_HEADS = 8
WIN_KV_HEADS = 2
WIN_DIM = 64
WINDOW = 128
WIN_WIDTH = WIN_Q_HEADS * WIN_DIM
WIN_KV = WIN_KV_HEADS * WIN_DIM
FOURIER_GROUPS = 4
GROUP_CH = D_MODEL // FOURIER_GROUPS
FFT_RADIX2 = 16
ROPE_THETA = 10000.0
N_GROUPS = 4
EXPERTS_PER_GROUP = 8
N_EXPERTS = N_GROUPS * EXPERTS_PER_GROUP
EXPERT_HIDDEN = D_MODEL // 2
LN_EPS = 1e-5
GN_EPS = 1e-6
ALPHA = (2.0 * DEPTH) ** 0.25
IN_EVEN = 2 * RET_WIDTH + 2 * RET_WIDTH + WIN_WIDTH + 2 * WIN_KV

LANES = 128
TOKEN_ROWS = D_MODEL // LANES
VMEM_LIMIT_BYTES = 48 * 1024 * 1024
NEG_MASK = -1e30
ROW_DMA_UNROLL = 8

_NT = (((1,), (1,)), ((), ()))


def _params(*sem):
    return pltpu.CompilerParams(dimension_semantics=sem, vmem_limit_bytes=VMEM_LIMIT_BYTES)


def _silu(v):
    return v / (1.0 + jnp.exp(-v))


def _layer_norm(z, gain, bias):
    mu = jnp.mean(z, axis=-1, keepdims=True)
    zc = z - mu
    var = jnp.mean(zc * zc, axis=-1, keepdims=True)
    return zc * lax.rsqrt(var + LN_EPS) * gain + bias


def _store_token_tiles(ref, v):
    n = v.shape[0]
    for c in range(TOKEN_ROWS):
        ref[pl.ds(c, n, stride=TOKEN_ROWS), :] = v[:, c * LANES:(c + 1) * LANES]


def _load_token_tiles(ref):
    n = ref.shape[0] // TOKEN_ROWS
    return jnp.concatenate([ref[pl.ds(c, n, stride=TOKEN_ROWS), :] for c in range(TOKEN_ROWS)], axis=1)


def _rope128(h, cos, sin_signed):
    return h * cos + pltpu.roll(h, 64, 1) * sin_signed


def _rope64x2(h, cos, sin_lo, sin_hi):
    return h * cos + pltpu.roll(h, 96, 1) * sin_lo + pltpu.roll(h, 32, 1) * sin_hi


def _inproj_kernel(x_ref, w_ref, cr_ref, sr_ref, cw_ref, slo_ref, shi_ref,
                   qa_ref, ka_ref, va_ref, ga_ref, qb_ref, kb_ref, vb_ref):
    xb = x_ref[...].astype(BF16)

    def seg(lo, hi):
        return jnp.dot(xb, w_ref[:, lo:hi], preferred_element_type=F32)

    cr, sr = cr_ref[...], sr_ref[...]
    q = seg(0, RET_WIDTH)
    k = seg(RET_WIDTH, 2 * RET_WIDTH)
    for h in range(RET_HEADS):
        sl = slice(LANES * h, LANES * (h + 1))
        qa_ref[:, sl] = _rope128(q[:, sl], cr, sr).astype(BF16)
        ka_ref[sl, :] = (_rope128(k[:, sl], cr, sr) * RET_DIM ** -0.5).T.astype(BF16)
    va_ref[...] = seg(2 * RET_WIDTH, 3 * RET_WIDTH).astype(BF16)
    ga_ref[...] = _silu(seg(3 * RET_WIDTH, 4 * RET_WIDTH)).astype(BF16)
    cw, slo, shi = cw_ref[...], slo_ref[...], shi_ref[...]
    base = 4 * RET_WIDTH
    q = seg(base, base + WIN_WIDTH)
    for p in range(WIN_WIDTH // LANES):
        sl = slice(LANES * p, LANES * (p + 1))
        qb_ref[:, sl] = (_rope64x2(q[:, sl], cw, slo, shi) * WIN_DIM ** -0.5).astype(BF16)
    kv = seg(base + WIN_WIDTH, base + WIN_WIDTH + 2 * WIN_KV)
    kb = _rope64x2(kv[:, :WIN_KV], cw, slo, shi)
    vb = kv[:, WIN_KV:]
    kb_ref[:, :WIN_KV] = kb.astype(BF16)
    kb_ref[:, WIN_KV:] = pltpu.roll(kb, WIN_DIM, 1).astype(BF16)
    vb_ref[:, :WIN_KV] = vb.astype(BF16)
    vb_ref[:, WIN_KV:] = pltpu.roll(vb, WIN_DIM, 1).astype(BF16)


def _rope_tables(seq):
    pos = jnp.arange(seq, dtype=F32)[:, None]
    half = RET_DIM // 2
    inv = ROPE_THETA ** (-jnp.arange(half, dtype=F32) / half)
    ang = pos * inv[None, :]
    cr = jnp.concatenate([jnp.cos(ang), jnp.cos(ang)], axis=1)
    sr = jnp.concatenate([-jnp.sin(ang), jnp.sin(ang)], axis=1)
    half = WIN_DIM // 2
    inv = ROPE_THETA ** (-jnp.arange(half, dtype=F32) / half)
    ang = pos * inv[None, :]
    c, s, z = jnp.cos(ang), jnp.sin(ang), jnp.zeros_like(ang)
    cw = jnp.concatenate([c, c, c, c], axis=1)
    slo = jnp.concatenate([-s, z, -s, z], axis=1)
    shi = jnp.concatenate([z, s, z, s], axis=1)
    return cr, sr, cw, slo, shi


def _inproj(x2, w_bf16, tables, seq, tm):
    T = x2.shape[0]
    nseq = seq // tm
    row = lambda i: (i, 0)
    tab = lambda i: (i % nseq, 0)
    widths = (RET_WIDTH, None, RET_WIDTH, RET_WIDTH, WIN_WIDTH, 2 * WIN_KV, 2 * WIN_KV)
    return pl.pallas_call(
        _inproj_kernel,
        grid=(T // tm,),
        in_specs=[pl.BlockSpec((tm, D_MODEL), row),
                  pl.BlockSpec((D_MODEL, IN_EVEN), lambda i: (0, 0))]
                 + [pl.BlockSpec((tm, LANES), tab)] * 5,
        out_specs=[pl.BlockSpec((tm, w), row) if w else pl.BlockSpec((RET_WIDTH, tm), lambda i: (0, i))
                   for w in widths],
        out_shape=[jax.ShapeDtypeStruct((T, w) if w else (RET_WIDTH, T), BF16) for w in widths],
        compiler_params=_params("arbitrary"),
        name="inproj_rope",
    )(x2, w_bf16, *tables)


def _retention_kernel(logit_ref, q_ref, kt_ref, v_ref, g_ref, gain_ref, o_ref,
                      dmat_scr, kv_scr, state_scr):
    h = pl.program_id(1)
    C = RET_BLOCK
    dk = RET_DIM
    n_chunks = q_ref.shape[1] // C

    def log_gamma(d):
        v = jnp.full((1, 1), logit_ref[d, h], F32)
        return -(jnp.maximum(-v, 0.0) + jnp.log(1.0 + jnp.exp(-jnp.abs(v))))

    lgf, lgb = log_gamma(0), log_gamma(1)
    i = lax.broadcasted_iota(I32, (C, C), 0)
    j = lax.broadcasted_iota(I32, (C, C), 1)
    diff = (i - j).astype(F32)
    dmat_scr[...] = jnp.where(diff >= 0, jnp.exp(lgf * jnp.maximum(diff, 0.0)),
                              jnp.exp(lgb * jnp.maximum(-diff, 0.0)))
    col = lax.broadcasted_iota(I32, (C, 1), 0).astype(F32)
    lane = lax.broadcasted_iota(I32, (1, C), 1).astype(F32)
    xi_f, xi_b = jnp.exp(lgf * (col + 1.0)), jnp.exp(lgb * (C - col))
    zeta_f, zeta_b = jnp.exp(lgf * (C - 1.0 - lane)), jnp.exp(lgb * lane)
    dec_f, dec_b = jnp.exp(lgf * C), jnp.exp(lgb * C)
    gain = gain_ref[...]

    def span(n):
        return pl.ds(pl.multiple_of(n * C, C), C)

    def kv_pass(n, carry):
        kt = kt_ref[:, span(n)].astype(F32)
        lhs = jnp.concatenate([(kt * zeta_f).astype(BF16), (kt * zeta_b).astype(BF16)], axis=0)
        kv_scr[n] = jnp.dot(lhs, v_ref[0, span(n), :], preferred_element_type=F32)
        return carry

    lax.fori_loop(0, n_chunks, kv_pass, 0, unroll=4)

    def scan_f(n, state):
        state_scr[n, :dk, :] = state.astype(BF16)
        return state * dec_f + kv_scr[n, :dk, :]

    def scan_b(t, state):
        n = n_chunks - 1 - t
        state_scr[n, dk:, :] = state.astype(BF16)
        return state * dec_b + kv_scr[n, dk:, :]

    zero = jnp.zeros((dk, dk), F32)
    lax.fori_loop(0, n_chunks, scan_f, zero)
    lax.fori_loop(0, n_chunks, scan_b, zero)

    def out_pass(n, carry):
        q = q_ref[0, span(n), :]
        s = jnp.dot(q, kt_ref[:, span(n)], preferred_element_type=F32)
        p = (s * dmat_scr[...]).astype(BF16)
        qf = q.astype(F32)
        qx = jnp.concatenate([(qf * xi_f).astype(BF16), (qf * xi_b).astype(BF16)], axis=1)
        y = (jnp.dot(p, v_ref[0, span(n), :], preferred_element_type=F32)
             + jnp.dot(qx, state_scr[n], preferred_element_type=F32))
        mu = jnp.mean(y, axis=-1, keepdims=True)
        yc = y - mu
        var = jnp.mean(yc * yc, axis=-1, keepdims=True)
        yn = yc * lax.rsqrt(var + GN_EPS) * gain
        o_ref[0, span(n), :] = (g_ref[0, span(n), :].astype(F32) * yn).astype(BF16)
        return carry

    lax.fori_loop(0, n_chunks, out_pass, 0, unroll=4)


def _retention(qa, ka_t, va, ga, decay_logit, gn_gain):
    B, S, _ = qa.shape
    n_chunks = S // RET_BLOCK
    head = pl.BlockSpec((1, S, RET_DIM), lambda b, h: (b, 0, h))
    return pl.pallas_call(
        _retention_kernel,
        grid=(B, RET_HEADS),
        in_specs=[pl.BlockSpec(memory_space=pltpu.SMEM), head,
                  pl.BlockSpec((RET_DIM, S), lambda b, h: (h, b)), head, head,
                  pl.BlockSpec((1, RET_DIM), lambda b, h: (0, h))],
        out_specs=head,
        out_shape=jax.ShapeDtypeStruct((B, S, RET_WIDTH), BF16),
        scratch_shapes=[pltpu.VMEM((RET_BLOCK, RET_BLOCK), F32),
                        pltpu.VMEM((n_chunks, 2 * RET_DIM, RET_DIM), F32),
                        pltpu.VMEM((n_chunks, 2 * RET_DIM, RET_DIM), BF16)],
        compiler_params=_params("arbitrary", "arbitrary"),
        name="retention_gn_gate",
    )(decay_logit.astype(F32), qa, ka_t, va, ga, gn_gain.reshape(1, RET_WIDTH).astype(F32))


def _winattn_kernel(sink_ref, q_ref, kp_ref, kc_ref, kn_ref, vp_ref, vc_ref, vn_ref, o_ref):
    step = pl.program_id(1)
    n_steps = pl.num_programs(1)
    W = WINDOW
    nq = q_ref.shape[1] // W
    group = WIN_Q_HEADS // WIN_KV_HEADS
    k_all = jnp.concatenate([kp_ref[0], kc_ref[0], kn_ref[0]], axis=0)
    v_all = jnp.concatenate([vp_ref[0], vc_ref[0], vn_ref[0]], axis=0)
    lo_half = lax.broadcasted_iota(I32, (1, LANES), 1) < WIN_DIM

    def placements(slab, g):
        first, second = slab[:, :LANES], slab[:, LANES:]
        zero = jnp.zeros_like(first)
        if g == 0:
            return jnp.where(lo_half, first, zero), jnp.where(lo_half, zero, second)
        return jnp.where(lo_half, second, zero), jnp.where(lo_half, zero, first)

    rows2 = lax.broadcasted_iota(I32, (2 * W, 1), 0)
    qi = lax.broadcasted_iota(I32, (2 * W, 3 * W), 0) & (W - 1)
    cj = lax.broadcasted_iota(I32, (2 * W, 3 * W), 1)
    rel = cj - qi
    band = (rel >= 0) & (rel <= 2 * W)

    for g in range(WIN_KV_HEADS):
        k_even, k_odd = placements(k_all, g)
        v_even, v_odd = placements(v_all, g)
        c0 = group * WIN_DIM * g
        sink_even = jnp.where(rows2 < W, sink_ref[0, group * g], sink_ref[0, group * g + 2])
        sink_odd = jnp.where(rows2 < W, sink_ref[0, group * g + 1], sink_ref[0, group * g + 3])
        for jq in range(nq):
            qrows = slice(jq * W, (jq + 1) * W)
            krows = slice(jq * W, (jq + 3) * W)
            mask = band
            if jq == 0:
                mask = mask & (cj >= jnp.where(step == 0, W, 0))
            if jq == nq - 1:
                mask = mask & (cj < jnp.where(step == n_steps - 1, 2 * W, 3 * W))
            q2 = jnp.concatenate([q_ref[0, qrows, c0:c0 + LANES],
                                  q_ref[0, qrows, c0 + LANES:c0 + 2 * LANES]], axis=0)

            def softmax_parts(k_placed, sink):
                s = lax.dot_general(q2, k_placed[krows], _NT, preferred_element_type=F32)
                s = jnp.where(mask, s, NEG_MASK)
                m = jnp.maximum(jnp.max(s, axis=1, keepdims=True), sink)
                e = jnp.exp(s - m)
                return e.astype(BF16), jnp.sum(e, axis=1, keepdims=True) + jnp.exp(sink - m)

            p_even, den_even = softmax_parts(k_even, sink_even)
            p_odd, den_odd = softmax_parts(k_odd, sink_odd)
            o = (jnp.dot(p_even, v_even[krows], preferred_element_type=F32)
                 + jnp.dot(p_odd, v_odd[krows], preferred_element_type=F32))
            o = (o / jnp.where(lo_half, den_even, den_odd)).astype(BF16)
            o_ref[0, qrows, c0:c0 + LANES] = o[:W]
            o_ref[0, qrows, c0 + LANES:c0 + 2 * LANES] = o[W:]


def _winattn(qb, kb2, vb2, sink_logit, blocks_per_step):
    B, S, _ = qb.shape
    nq = blocks_per_step
    nb = S // WINDOW
    prev = pl.BlockSpec((1, WINDOW, 2 * WIN_KV), lambda b, s: (b, jnp.maximum(s * nq - 1, 0), 0))
    cur = pl.BlockSpec((1, nq * WINDOW, 2 * WIN_KV), lambda b, s: (b, s, 0))
    nxt = pl.BlockSpec((1, WINDOW, 2 * WIN_KV), lambda b, s: (b, jnp.minimum((s + 1) * nq, nb - 1), 0))
    qspec = pl.BlockSpec((1, nq * WINDOW, WIN_WIDTH), lambda b, s: (b, s, 0))
    return pl.pallas_call(
        _winattn_kernel,
        grid=(B, nb // nq),
        in_specs=[pl.BlockSpec(memory_space=pltpu.SMEM), qspec, prev, cur, nxt, prev, cur, nxt],
        out_specs=qspec,
        out_shape=jax.ShapeDtypeStruct((B, S, WIN_WIDTH), BF16),
        compiler_params=_params("arbitrary", "arbitrary"),
        name="window_attention",
    )(sink_logit.reshape(1, WIN_Q_HEADS).astype(F32), qb, kb2, kb2, kb2, vb2, vb2, vb2)


def _chan_dft_kernel(xa_ref, xb_ref, m_ref, wr_ref, wi_ref):
    n1 = xa_ref.shape[1] // FFT_RADIX2
    for s in range(FFT_RADIX2):
        rows = pl.ds(s, n1, stride=FFT_RADIX2)
        w = (jnp.dot(xa_ref[0, rows, :].astype(BF16), m_ref[:LANES, :], preferred_element_type=F32)
             + jnp.dot(xb_ref[0, rows, :].astype(BF16), m_ref[LANES:, :], preferred_element_type=F32))
        wr_ref[0, s] = w[:, :GROUP_CH].astype(BF16)
        wi_ref[0, s] = w[:, GROUP_CH:].astype(BF16)


def _real_dft16(ar, ai):
    n = FFT_RADIX2
    cs = [math.cos(2 * math.pi * m / n) for m in range(n)]
    sn = [math.sin(2 * math.pi * m / n) for m in range(n)]

    def axpy(acc, coef, v):
        if abs(coef) < 1e-12:
            return acc
        if abs(abs(coef) - 1.0) < 1e-12:
            if acc is None:
                return v if coef > 0 else -v
            return acc + v if coef > 0 else acc - v
        t = coef * v
        return t if acc is None else acc + t

    p = {s: ar[s] + ar[n - s] for s in range(1, n // 2)}
    d = {s: ai[s] - ai[n - s] for s in range(1, n // 2)}
    base = (ar[0] + ar[n // 2], ar[0] - ar[n // 2])
    y = [None] * n
    for k in range(n // 2 + 1):
        e = base[k % 2]
        for s in range(1, n // 2):
            e = axpy(e, cs[(s * k) % n], p[s])
        if k in (0, n // 2):
            y[k] = e
            continue
        o = None
        for s in range(1, n // 2):
            o = axpy(o, sn[(s * k) % n], d[s])
        y[k] = e + o
        y[n - k] = e - o
    return y


def _seq_dft_kernel(wr_ref, wi_ref, cf_ref, sf_ref, y_ref, ar_scr, ai_scr):
    n1 = wr_ref.shape[2]
    for s in range(FFT_RADIX2):
        cf, sf, wr, wi = cf_ref[s], sf_ref[s], wr_ref[0, s], wi_ref[0, s]
        ar_scr[s] = (jnp.dot(cf, wr, preferred_element_type=F32)
                     + jnp.dot(sf, wi, preferred_element_type=F32))
        ai_scr[s] = (jnp.dot(cf, wi, preferred_element_type=F32)
                     - jnp.dot(sf, wr, preferred_element_type=F32))

    def rows8(r, carry):
        rows = pl.ds(pl.multiple_of(r * 8, 8), 8)
        y = _real_dft16([ar_scr[s, rows, :] for s in range(FFT_RADIX2)],
                        [ai_scr[s, rows, :] for s in range(FFT_RADIX2)])
        for k in range(FFT_RADIX2):
            y_ref[0, k, rows, :] = y[k].astype(BF16)
        return carry

    lax.fori_loop(0, n1 // 8, rows8, 0)


def _fourier_tables(seq):
    n1 = seq // FFT_RADIX2
    c = jnp.arange(GROUP_CH, dtype=I32)
    m = (c[:, None] * c[None, :]) % GROUP_CH
    ang = m.astype(F32) * (2.0 * math.pi / GROUP_CH)
    scale = (seq * GROUP_CH) ** -0.5
    chan = (jnp.concatenate([jnp.cos(ang), -jnp.sin(ang)], axis=1) * scale).astype(BF16)
    k1 = jnp.arange(n1, dtype=I32)[None, :, None]
    s1 = jnp.arange(n1, dtype=I32)[None, None, :]
    s2 = jnp.arange(FFT_RADIX2, dtype=I32)[:, None, None]
    m = (FFT_RADIX2 * s1 * k1 + s2 * k1) % seq
    ang = m.astype(F32) * (2.0 * math.pi / seq)
    return chan, jnp.cos(ang).astype(BF16), jnp.sin(ang).astype(BF16)


def _fourier(x, tables, tn=256):
    B, S, _ = x.shape
    n1 = S // FFT_RADIX2
    chan, cf, sf = tables
    wshape = jax.ShapeDtypeStruct((B, FFT_RADIX2, n1, D_MODEL), BF16)
    wblk = pl.BlockSpec((1, FFT_RADIX2, n1, GROUP_CH), lambda b, g: (b, 0, 0, g))
    wr, wi = pl.pallas_call(
        _chan_dft_kernel,
        grid=(B, FOURIER_GROUPS),
        in_specs=[pl.BlockSpec((1, S, LANES), lambda b, g: (b, 0, 2 * g)),
                  pl.BlockSpec((1, S, LANES), lambda b, g: (b, 0, 2 * g + 1)),
                  pl.BlockSpec((GROUP_CH, 2 * GROUP_CH), lambda b, g: (0, 0))],
        out_specs=[wblk, wblk],
        out_shape=[wshape, wshape],
        compiler_params=_params("arbitrary", "arbitrary"),
        name="fourier_channel_dft",
    )(x, x, chan)
    cols = pl.BlockSpec((1, FFT_RADIX2, n1, tn), lambda b, j: (b, 0, 0, j))
    full = pl.BlockSpec((FFT_RADIX2, n1, n1), lambda b, j: (0, 0, 0))
    y = pl.pallas_call(
        _seq_dft_kernel,
        grid=(B, D_MODEL // tn),
        in_specs=[cols, cols, full, full],
        out_specs=cols,
        out_shape=wshape,
        scratch_shapes=[pltpu.VMEM((FFT_RADIX2, n1, tn), F32), pltpu.VMEM((FFT_RADIX2, n1, tn), F32)],
        compiler_params=_params("arbitrary", "arbitrary"),
        name="fourier_sequence_dft",
    )(wr, wi, cf, sf)
    return y.reshape(B * S, D_MODEL)


def _proj_ln_route_kernel(a_ref, b_ref, x_ref, w_ref, gain_ref, bias_ref, wr_ref, br_ref,
                          xt_ref, ri_ref, rg_ref, cnt_ref, carry_scr):
    tm = x_ref.shape[0]
    half = a_ref.shape[1]

    @pl.when(pl.program_id(0) == 0)
    def _():
        carry_scr[...] = jnp.zeros_like(carry_scr)

    mix = (jnp.dot(a_ref[...], w_ref[:half, :], preferred_element_type=F32)
           + jnp.dot(b_ref[...], w_ref[half:, :], preferred_element_type=F32))
    y = _layer_norm(ALPHA * x_ref[...] + mix, gain_ref[...], bias_ref[...])
    _store_token_tiles(xt_ref, y)

    logits = jnp.dot(y.astype(BF16), wr_ref[...], preferred_element_type=F32) + br_ref[...]
    lane = lax.broadcasted_iota(I32, (tm, LANES), 1)
    lanef = lane.astype(F32)
    ninf = -jnp.inf
    big = 1e9

    def rmax(v):
        return jnp.max(v, axis=1, keepdims=True)

    def first_lane(hit):
        return jnp.min(jnp.where(hit, lanef, big), axis=1, keepdims=True)

    cmask = (lane >= N_EXPERTS) & (lane < N_EXPERTS + N_GROUPS)
    cl = jnp.where(cmask, logits, ninf)
    cmax = rmax(cl)
    group = first_lane(cl == cmax) - float(N_EXPERTS)
    p_group = 1.0 / jnp.sum(jnp.where(cmask, jnp.exp(cl - cmax), 0.0), axis=1, keepdims=True)
    lo = group * float(EXPERTS_PER_GROUP)
    fmask = (lanef >= lo) & (lanef < lo + float(EXPERTS_PER_GROUP))
    fl = jnp.where(fmask, logits, ninf)
    v1 = rmax(fl)
    e1 = first_lane(fl == v1)
    fl2 = jnp.where(lanef == e1, ninf, fl)
    v2 = rmax(fl2)
    e2 = first_lane(fl2 == v2)
    t = jnp.exp(v2 - v1)
    g1 = p_group / (1.0 + t)
    g2 = p_group * t / (1.0 + t)

    hit1, hit2 = lanef == e1, lanef == e2
    onehot = jnp.where(hit1 | hit2, 1.0, 0.0)
    r = lax.broadcasted_iota(I32, (tm, tm), 0)
    cidx = lax.broadcasted_iota(I32, (tm, tm), 1)
    lower = jnp.where(r > cidx, 1.0, 0.0).astype(BF16)
    prefix = jnp.dot(lower, onehot.astype(BF16), preferred_element_type=F32) + carry_scr[...]
    rank1 = jnp.sum(jnp.where(hit1, prefix, 0.0), axis=1, keepdims=True)
    rank2 = jnp.sum(jnp.where(hit2, prefix, 0.0), axis=1, keepdims=True)
    carry_scr[...] += jnp.sum(onehot, axis=0, keepdims=True)
    cnt_ref[...] = carry_scr[...]
    ri_ref[...] = jnp.where(lane == 0, e1, jnp.where(lane == 1, e2,
                            jnp.where(lane == 2, rank1, rank2))).astype(I32)
    rg_ref[...] = jnp.where(lane == 0, g1, g2)


def _proj_ln_route(a, b, x2, w_bf16, gain, bias, w_router, b_router, tm):
    T = x2.shape[0]
    half = D_MODEL // 2
    row = lambda i: (i, 0)
    const = lambda i: (0, 0)
    a_spec = pl.BlockSpec((tm, half), row)
    b_spec = pl.BlockSpec((tm, half), row if b is not a else (lambda i: (i, 1)))
    return pl.pallas_call(
        _proj_ln_route_kernel,
        grid=(T // tm,),
        in_specs=[a_spec, b_spec, pl.BlockSpec((tm, D_MODEL), row),
                  pl.BlockSpec((D_MODEL, D_MODEL), const),
                  pl.BlockSpec((1, D_MODEL), const), pl.BlockSpec((1, D_MODEL), const),
                  pl.BlockSpec((D_MODEL, LANES), const), pl.BlockSpec((1, LANES), const)],
        out_specs=[pl.BlockSpec((tm * TOKEN_ROWS, LANES), row), pl.BlockSpec((tm, LANES), row),
                   pl.BlockSpec((tm, LANES), row), pl.BlockSpec((1, LANES), const)],
        out_shape=[jax.ShapeDtypeStruct((T * TOKEN_ROWS, LANES), F32), jax.ShapeDtypeStruct((T, LANES), I32),
                   jax.ShapeDtypeStruct((T, LANES), F32), jax.ShapeDtypeStruct((1, LANES), F32)],
        scratch_shapes=[pltpu.VMEM((1, LANES), F32)],
        compiler_params=_params("arbitrary"),
        name="proj_ln_router",
    )(a, b, x2, w_bf16, gain, bias, w_router, b_router)


def _expert_kernel(te_ref, base_ref, nv_ref, src_ref, x_hbm, wg_ref, wu_ref, wd_ref, y_ref,
                   wg_scr, wu_scr, wd_scr, xbuf0, xbuf1, xbuf2, gsem):
    i = pl.program_id(0)
    nv = nv_ref[0]
    tile = xbuf0.shape[0] // TOKEN_ROWS
    xbufs = (xbuf0, xbuf1, xbuf2)
    n_buf = len(xbufs)

    def token_rows(first_row):
        return pl.ds(pl.multiple_of(first_row, TOKEN_ROWS), TOKEN_ROWS)

    def gather_row(base, r, q):
        return pltpu.make_async_copy(x_hbm.at[token_rows(src_ref[base + r]), :],
                                     xbufs[q].at[token_rows(r * TOKEN_ROWS), :], gsem.at[q])

    def wait_gather(q):
        pltpu.make_async_copy(x_hbm.at[pl.ds(0, tile * TOKEN_ROWS), :], xbufs[q], gsem.at[q]).wait()

    @pl.when(i == 0)
    def _():
        def first(r, carry):
            gather_row(base_ref[0], r, 0).start()
            gather_row(base_ref[1], r, 1).start()
            return carry

        lax.fori_loop(0, tile, first, 0, unroll=ROW_DMA_UNROLL)

    @pl.when(i < nv)
    def _():
        @pl.when((i == 0) | (te_ref[i] != te_ref[jnp.maximum(i - 1, 0)]))
        def _():
            wg_scr[...] = wg_ref[0, 0].astype(BF16)
            wu_scr[...] = wu_ref[0, 0].astype(BF16)
            wd_scr[...] = wd_ref[0, 0].astype(BF16)

        base_ahead = base_ref[i + 2]
        for q in range(n_buf):
            @pl.when(i % n_buf == q)
            def _():
                ahead = (q + 2) % n_buf
                wait_gather(q)
                for r in range(tile):
                    gather_row(base_ahead, r, ahead).start(priority=r % 2)
                xb = _load_token_tiles(xbufs[q]).astype(BF16)
                hg = jnp.dot(xb, wg_scr[...], preferred_element_type=F32)
                hu = jnp.dot(xb, wu_scr[...], preferred_element_type=F32)
                hid = (_silu(hg) * hu).astype(BF16)
                _store_token_tiles(y_ref, jnp.dot(hid, wd_scr[...], preferred_element_type=F32))

                @pl.when(i == nv - 1)
                def _():
                    wait_gather((q + 1) % n_buf)
                    wait_gather(ahead)

    @pl.when(i >= nv)
    def _():
        y_ref[...] = jnp.zeros_like(y_ref)


def _experts(x1t, src, tile_expert, tile_base, n_valid, w_gate, w_up, w_down, layer, tile):
    n_tiles = tile_expert.shape[0]
    wsel = lambda i, te, *_: (layer, te[i], 0, 0)
    buf = pltpu.VMEM((tile * TOKEN_ROWS, LANES), F32)
    return pl.pallas_call(
        _expert_kernel,
        grid_spec=pltpu.PrefetchScalarGridSpec(
            num_scalar_prefetch=4,
            grid=(n_tiles,),
            in_specs=[pl.BlockSpec(memory_space=pl.ANY),
                      pl.BlockSpec((1, 1, D_MODEL, EXPERT_HIDDEN), wsel),
                      pl.BlockSpec((1, 1, D_MODEL, EXPERT_HIDDEN), wsel),
                      pl.BlockSpec((1, 1, EXPERT_HIDDEN, D_MODEL), wsel)],
            out_specs=pl.BlockSpec((tile * TOKEN_ROWS, LANES), lambda i, *_: (i, 0)),
            scratch_shapes=[pltpu.VMEM((D_MODEL, EXPERT_HIDDEN), BF16),
                            pltpu.VMEM((D_MODEL, EXPERT_HIDDEN), BF16),
                            pltpu.VMEM((EXPERT_HIDDEN, D_MODEL), BF16),
                            buf, buf, buf, pltpu.SemaphoreType.DMA((3,))],
        ),
        out_shape=jax.ShapeDtypeStruct((n_tiles * tile * TOKEN_ROWS, LANES), F32),
        compiler_params=_params("arbitrary"),
        name="moe_experts",
    )(tile_expert, tile_base, n_valid, src, x1t, w_gate, w_up, w_down)


def _combine_ln_kernel(pos_cur, pos_nxt, pos_ahead, x_ref, g_ref, gain_ref, bias_ref, ys_hbm, o_ref,
                       buf0, buf1, buf2, sem):
    i = pl.program_id(0)
    n = pl.num_programs(0)
    tm = g_ref.shape[0]
    bufs = (buf0, buf1, buf2)
    n_buf = len(bufs)

    def gather_row(pos_ref, j, q):
        src = pl.ds(pl.multiple_of(pos_ref[0, 0, j], TOKEN_ROWS), TOKEN_ROWS)
        dst = pl.ds(((j % 2) * tm + j // 2) * TOKEN_ROWS, TOKEN_ROWS)
        return pltpu.make_async_copy(ys_hbm.at[src, :], bufs[q].at[dst, :], sem.at[q])

    def wait_gather(q):
        pltpu.make_async_copy(ys_hbm.at[pl.ds(0, 2 * tm * TOKEN_ROWS), :], bufs[q], sem.at[q]).wait()

    @pl.when(i == 0)
    def _():
        def first(j, carry):
            gather_row(pos_cur, j, 0).start()
            gather_row(pos_nxt, j, 1).start()
            return carry

        lax.fori_loop(0, 2 * tm, first, 0, unroll=ROW_DMA_UNROLL)

    for q in range(n_buf):
        @pl.when(i % n_buf == q)
        def _():
            ahead = (q + 2) % n_buf
            wait_gather(q)
            for j in range(2 * tm):
                gather_row(pos_ahead, j, ahead).start(priority=j % 2)
            g = g_ref[...]
            half = tm * TOKEN_ROWS
            ffn = (_load_token_tiles(bufs[q].at[pl.ds(0, half), :]) * g[:, 0:1]
                   + _load_token_tiles(bufs[q].at[pl.ds(half, half), :]) * g[:, 1:2])
            o_ref[...] = _layer_norm(ALPHA * _load_token_tiles(x_ref) + ffn, gain_ref[...], bias_ref[...])

            @pl.when(i == n - 1)
            def _():
                wait_gather((q + 1) % n_buf)
                wait_gather(ahead)


def _combine_ln(x1t, ys, pos, gates, gain, bias, tm):
    T = x1t.shape[0] // TOKEN_ROWS
    n = T // tm
    pos3 = pos.reshape(n, 1, 2 * tm)
    row = lambda i: (i, 0)
    const = lambda i: (0, 0)
    smem = lambda k: pl.BlockSpec((1, 1, 2 * tm), lambda i: (jnp.minimum(i + k, n - 1), 0, 0),
                                  memory_space=pltpu.SMEM)
    buf = pltpu.VMEM((2 * tm * TOKEN_ROWS, LANES), F32)
    return pl.pallas_call(
        _combine_ln_kernel,
        grid=(n,),
        in_specs=[smem(0), smem(1), smem(2),
                  pl.BlockSpec((tm * TOKEN_ROWS, LANES), row), pl.BlockSpec((tm, LANES), row),
                  pl.BlockSpec((1, D_MODEL), const), pl.BlockSpec((1, D_MODEL), const),
                  pl.BlockSpec(memory_space=pl.ANY)],
        out_specs=pl.BlockSpec((tm, D_MODEL), row),
        out_shape=jax.ShapeDtypeStruct((T, D_MODEL), F32),
        scratch_shapes=[buf, buf, buf, pltpu.SemaphoreType.DMA((3,))],
        compiler_params=_params("arbitrary"),
        name="moe_combine_ln",
    )(pos3, pos3, pos3, x1t, gates, gain, bias, ys)


def _moe(x1t, route_i, route_g, counts_f, w_gate, w_up, w_down, layer, gain, bias, tile, tm):
    T = x1t.shape[0] // TOKEN_ROWS
    n_pairs = 2 * T
    counts = counts_f[0, :N_EXPERTS].astype(I32)
    eids = jnp.arange(N_EXPERTS, dtype=I32)
    expert, rank = route_i[:, 0:2], route_i[:, 2:4]
    key = expert.reshape(-1) * n_pairs + jnp.arange(n_pairs, dtype=I32)
    pair = jnp.sort(key) % n_pairs
    src = jnp.concatenate([pair // 2, jnp.zeros((tile,), I32)]) * TOKEN_ROWS
    tiles_per = (counts + tile - 1) // tile
    tile_ends = jnp.cumsum(tiles_per)
    first_tile = tile_ends - tiles_per
    first_row = jnp.cumsum(counts) - counts
    n_tiles = n_pairs // tile + N_EXPERTS
    tile_ids = jnp.arange(n_tiles + 2, dtype=I32)
    owner = jnp.sum((tile_ids[:, None] >= tile_ends[None, :]).astype(I32), axis=1)
    valid = owner < N_EXPERTS
    last_expert = jnp.max(jnp.where(counts > 0, eids, 0))
    tile_expert = jnp.minimum(owner, last_expert).astype(I32)
    onehot = owner[:, None] == eids[None, :]
    pick = lambda v: jnp.sum(jnp.where(onehot, v[None, :], 0), axis=1)
    tile_base = pick(first_row) + (tile_ids - pick(first_tile)) * tile
    tile_base = jnp.where(valid, tile_base, n_pairs).astype(I32)
    n_valid = tile_ends[-1:].astype(I32)
    ys = _experts(x1t, src, tile_expert[:n_tiles], tile_base, n_valid, w_gate, w_up, w_down, layer, tile)
    pos = rank + jnp.sum(jnp.where(expert[..., None] == eids, first_tile * tile, 0), axis=-1)
    return _combine_ln(x1t, ys, (pos * TOKEN_ROWS).astype(I32), route_g, gain, bias, tm)


def _pick(n, pref):
    t = min(n, pref)
    while n % t:
        t //= 2
    return t


def kernel(x, w_in_even, ret_decay_logit, ret_gn_gain, sink_logit, w_out_even, w_out_fourier,
           ln1_gain, ln1_bias, ln2_gain, ln2_bias, router_coarse_w, router_coarse_b,
           router_fine_w, router_fine_b, expert_w_gate, expert_w_up, expert_w_down):
    B, S, D = x.shape
    assert D == D_MODEL and S % (FFT_RADIX2 * 8) == 0 and S % WINDOW == 0 and S % RET_BLOCK == 0
    T = B * S
    tm = _pick(S, 512)
    tile = _pick(T, 512)
    rope_tabs = _rope_tables(S)
    fourier_tabs = _fourier_tables(S)
    row = lambda v: v.reshape(1, -1).astype(F32)

    x2 = x.reshape(T, D).astype(F32)
    for layer in range(DEPTH):
        if layer % 2 == 0:
            e = layer // 2
            qa, ka_t, va, ga, qb, kb2, vb2 = _inproj(x2, w_in_even[e].astype(BF16), rope_tabs, S, tm)
            shp = lambda v: v.reshape(B, S, v.shape[-1])
            ya = _retention(shp(qa), ka_t, shp(va), shp(ga), ret_decay_logit[e], ret_gn_gain[e])
            yb = _winattn(shp(qb), shp(kb2), shp(vb2), sink_logit[e], _pick(S // WINDOW, 4))
            a, b = ya.reshape(T, RET_WIDTH), yb.reshape(T, WIN_WIDTH)
            w_out = w_out_even[e]
        else:
            a = b = _fourier(x2.reshape(B, S, D), fourier_tabs)
            w_out = w_out_fourier[layer // 2]
        w_router = jnp.zeros((D, LANES), F32)
        w_router = w_router.at[:, :N_EXPERTS].set(router_fine_w[layer])
        w_router = w_router.at[:, N_EXPERTS:N_EXPERTS + N_GROUPS].set(router_coarse_w[layer])
        b_router = jnp.zeros((1, LANES), F32)
        b_router = b_router.at[0, :N_EXPERTS].set(router_fine_b[layer])
        b_router = b_router.at[0, N_EXPERTS:N_EXPERTS + N_GROUPS].set(router_coarse_b[layer])
        x1t, route_i, route_g, counts = _proj_ln_route(
            a, b, x2, w_out.astype(BF16), row(ln1_gain[layer]), row(ln1_bias[layer]),
            w_router.astype(BF16), b_router, tm)
        x2 = _moe(x1t, route_i, route_g, counts, expert_w_gate, expert_w_up, expert_w_down, layer,
                  row(ln2_gain[layer]), row(ln2_bias[layer]), tile, tm)
    return x2.reshape(B, S, D).astype(x.dtype)
```

```python
import math

import jax
import jax.numpy as jnp
from jax import lax
from jax.experimental import pallas as pl
from jax.experimental.pallas import tpu as pltpu

F32 = jnp.float32
BF16 = jnp.bfloat16
I32 = jnp.int32

D_MODEL = 1024
DEPTH = 4
RET_HEADS = 4
RET_DIM = 128
RET_BLOCK = 256
RET_WIDTH = RET_HEADS * RET_DIM
WIN_Q_HEADS = 8
WIN_KV_HEADS = 2
WIN_DIM = 64
WINDOW = 128
WIN_WIDTH = WIN_Q_HEADS * WIN_DIM
WIN_KV = WIN_KV_HEADS * WIN_DIM
FOURIER_GROUPS = 4
GROUP_CH = D_MODEL // FOURIER_GROUPS
FFT_RADIX2 = 16
ROPE_THETA = 10000.0
N_GROUPS = 4
EXPERTS_PER_GROUP = 8
N_EXPERTS = N_GROUPS * EXPERTS_PER_GROUP
EXPERT_HIDDEN = D_MODEL // 2
LN_EPS = 1e-5
GN_EPS = 1e-6
ALPHA = (2.0 * DEPTH) ** 0.25
IN_EVEN = 2 * RET_WIDTH + 2 * RET_WIDTH + WIN_WIDTH + 2 * WIN_KV

LANES = 128
TOKEN_ROWS = D_MODEL // LANES
VMEM_LIMIT_BYTES = 48 * 1024 * 1024
NEG_MASK = -1e30
ROUTE_ROWS = 8
ROW_DMA_UNROLL = 8

_NT = (((1,), (1,)), ((), ()))


def _params(*sem):
    return pltpu.CompilerParams(dimension_semantics=sem, vmem_limit_bytes=VMEM_LIMIT_BYTES)


def _silu(v):
    return v / (1.0 + jnp.exp(-v))


def _layer_norm(z, gain, bias):
    mu = jnp.mean(z, axis=-1, keepdims=True)
    zc = z - mu
    var = jnp.mean(zc * zc, axis=-1, keepdims=True)
    return zc * lax.rsqrt(var + LN_EPS) * gain + bias


def _store_token_tiles(ref, v):
    n = v.shape[0]
    for c in range(TOKEN_ROWS):
        ref[pl.ds(c, n, stride=TOKEN_ROWS), :] = v[:, c * LANES:(c + 1) * LANES]


def _load_token_tiles(ref):
    n = ref.shape[0] // TOKEN_ROWS
    return jnp.concatenate([ref[pl.ds(c, n, stride=TOKEN_ROWS), :] for c in range(TOKEN_ROWS)], axis=1)


def _rope128(h, cos, sin_signed):
    return h * cos + pltpu.roll(h, 64, 1) * sin_signed


def _rope64x2(h, cos, sin_lo, sin_hi):
    return h * cos + pltpu.roll(h, 96, 1) * sin_lo + pltpu.roll(h, 32, 1) * sin_hi


def _inproj_kernel(x_ref, w_ref, cr_ref, sr_ref, cw_ref, slo_ref, shi_ref,
                   qa_ref, ka_ref, va_ref, ga_ref, qb_ref, kb_ref, vb_ref):
    xb = x_ref[...].astype(BF16)

    def seg(lo, hi):
        return jnp.dot(xb, w_ref[:, lo:hi], preferred_element_type=F32)

    cr, sr = cr_ref[...], sr_ref[...]
    q = seg(0, RET_WIDTH)
    k = seg(RET_WIDTH, 2 * RET_WIDTH)
    for h in range(RET_HEADS):
        sl = slice(LANES * h, LANES * (h + 1))
        qa_ref[:, sl] = _rope128(q[:, sl], cr, sr).astype(BF16)
        ka_ref[sl, :] = (_rope128(k[:, sl], cr, sr) * RET_DIM ** -0.5).T.astype(BF16)
    va_ref[...] = seg(2 * RET_WIDTH, 3 * RET_WIDTH).astype(BF16)
    ga_ref[...] = _silu(seg(3 * RET_WIDTH, 4 * RET_WIDTH)).astype(BF16)
    cw, slo, shi = cw_ref[...], slo_ref[...], shi_ref[...]
    base = 4 * RET_WIDTH
    q = seg(base, base + WIN_WIDTH)
    for p in range(WIN_WIDTH // LANES):
        sl = slice(LANES * p, LANES * (p + 1))
        qb_ref[:, sl] = (_rope64x2(q[:, sl], cw, slo, shi) * WIN_DIM ** -0.5).astype(BF16)
    kv = seg(base + WIN_WIDTH, base + WIN_WIDTH + 2 * WIN_KV)
    kb = _rope64x2(kv[:, :WIN_KV], cw, slo, shi)
    vb = kv[:, WIN_KV:]
    kb_ref[:, :WIN_KV] = kb.astype(BF16)
    kb_ref[:, WIN_KV:] = pltpu.roll(kb, WIN_DIM, 1).astype(BF16)
    vb_ref[:, :WIN_KV] = vb.astype(BF16)
    vb_ref[:, WIN_KV:] = pltpu.roll(vb, WIN_DIM, 1).astype(BF16)


def _rope_tables(seq):
    pos = jnp.arange(seq, dtype=F32)[:, None]
    half = RET_DIM // 2
    inv = ROPE_THETA ** (-jnp.arange(half, dtype=F32) / half)
    ang = pos * inv[None, :]
    cr = jnp.concatenate([jnp.cos(ang), jnp.cos(ang)], axis=1)
    sr = jnp.concatenate([-jnp.sin(ang), jnp.sin(ang)], axis=1)
    half = WIN_DIM // 2
    inv = ROPE_THETA ** (-jnp.arange(half, dtype=F32) / half)
    ang = pos * inv[None, :]
    c, s, z = jnp.cos(ang), jnp.sin(ang), jnp.zeros_like(ang)
    cw = jnp.concatenate([c, c, c, c], axis=1)
    slo = jnp.concatenate([-s, z, -s, z], axis=1)
    shi = jnp.concatenate([z, s, z, s], axis=1)
    return cr, sr, cw, slo, shi


def _inproj(x2, w_bf16, tables, seq, tm):
    T = x2.shape[0]
    nseq = seq // tm
    row = lambda i: (i, 0)
    tab = lambda i: (i % nseq, 0)
    widths = (RET_WIDTH, None, RET_WIDTH, RET_WIDTH, WIN_WIDTH, 2 * WIN_KV, 2 * WIN_KV)
    return pl.pallas_call(
        _inproj_kernel,
        grid=(T // tm,),
        in_specs=[pl.BlockSpec((tm, D_MODEL), row),
                  pl.BlockSpec((D_MODEL, IN_EVEN), lambda i: (0, 0))]
                 + [pl.BlockSpec((tm, LANES), tab)] * 5,
        out_specs=[pl.BlockSpec((tm, w), row) if w else pl.BlockSpec((RET_WIDTH, tm), lambda i: (0, i))
                   for w in widths],
        out_shape=[jax.ShapeDtypeStruct((T, w) if w else (RET_WIDTH, T), BF16) for w in widths],
        compiler_params=_params("arbitrary"),
        name="inproj_rope",
    )(x2, w_bf16, *tables)


def _retention_kernel(logit_ref, q_ref, kt_ref, v_ref, g_ref, gain_ref, o_ref,
                      dmat_scr, kv_scr, state_scr):
    h = pl.program_id(1)
    C = RET_BLOCK
    dk = RET_DIM
    n_chunks = q_ref.shape[1] // C

    def log_gamma(d):
        v = jnp.full((1, 1), logit_ref[d, h], F32)
        return -(jnp.maximum(-v, 0.0) + jnp.log(1.0 + jnp.exp(-jnp.abs(v))))

    lgf, lgb = log_gamma(0), log_gamma(1)
    i = lax.broadcasted_iota(I32, (C, C), 0)
    j = lax.broadcasted_iota(I32, (C, C), 1)
    diff = (i - j).astype(F32)
    dmat_scr[...] = jnp.where(diff >= 0, jnp.exp(lgf * jnp.maximum(diff, 0.0)),
                              jnp.exp(lgb * jnp.maximum(-diff, 0.0)))
    col = lax.broadcasted_iota(I32, (C, 1), 0).astype(F32)
    lane = lax.broadcasted_iota(I32, (1, C), 1).astype(F32)
    xi_f, xi_b = jnp.exp(lgf * (col + 1.0)), jnp.exp(lgb * (C - col))
    zeta_f, zeta_b = jnp.exp(lgf * (C - 1.0 - lane)), jnp.exp(lgb * lane)
    dec_f, dec_b = jnp.exp(lgf * C), jnp.exp(lgb * C)
    gain = gain_ref[...]

    def span(n):
        return pl.ds(pl.multiple_of(n * C, C), C)

    def kv_pass(n, carry):
        kt = kt_ref[:, span(n)].astype(F32)
        lhs = jnp.concatenate([(kt * zeta_f).astype(BF16), (kt * zeta_b).astype(BF16)], axis=0)
        kv_scr[n] = jnp.dot(lhs, v_ref[0, span(n), :], preferred_element_type=F32)
        return carry

    lax.fori_loop(0, n_chunks, kv_pass, 0, unroll=4)

    def scan_f(n, state):
        state_scr[n, :dk, :] = state.astype(BF16)
        return state * dec_f + kv_scr[n, :dk, :]

    def scan_b(t, state):
        n = n_chunks - 1 - t
        state_scr[n, dk:, :] = state.astype(BF16)
        return state * dec_b + kv_scr[n, dk:, :]

    zero = jnp.zeros((dk, dk), F32)
    lax.fori_loop(0, n_chunks, scan_f, zero)
    lax.fori_loop(0, n_chunks, scan_b, zero)

    def out_pass(n, carry):
        q = q_ref[0, span(n), :]
        s = jnp.dot(q, kt_ref[:, span(n)], preferred_element_type=F32)
        p = (s * dmat_scr[...]).astype(BF16)
        qf = q.astype(F32)
        qx = jnp.concatenate([(qf * xi_f).astype(BF16), (qf * xi_b).astype(BF16)], axis=1)
        y = (jnp.dot(p, v_ref[0, span(n), :], preferred_element_type=F32)
             + jnp.dot(qx, state_scr[n], preferred_element_type=F32))
        mu = jnp.mean(y, axis=-1, keepdims=True)
        yc = y - mu
        var = jnp.mean(yc * yc, axis=-1, keepdims=True)
        yn = yc * lax.rsqrt(var + GN_EPS) * gain
        o_ref[0, span(n), :] = (g_ref[0, span(n), :].astype(F32) * yn).astype(BF16)
        return carry

    lax.fori_loop(0, n_chunks, out_pass, 0, unroll=4)


def _retention(qa, ka_t, va, ga, decay_logit, gn_gain):
    B, S, _ = qa.shape
    n_chunks = S // RET_BLOCK
    head = pl.BlockSpec((1, S, RET_DIM), lambda b, h: (b, 0, h))
    return pl.pallas_call(
        _retention_kernel,
        grid=(B, RET_HEADS),
        in_specs=[pl.BlockSpec(memory_space=pltpu.SMEM), head,
                  pl.BlockSpec((RET_DIM, S), lambda b, h: (h, b)), head, head,
                  pl.BlockSpec((1, RET_DIM), lambda b, h: (0, h))],
        out_specs=head,
        out_shape=jax.ShapeDtypeStruct((B, S, RET_WIDTH), BF16),
        scratch_shapes=[pltpu.VMEM((RET_BLOCK, RET_BLOCK), F32),
                        pltpu.VMEM((n_chunks, 2 * RET_DIM, RET_DIM), F32),
                        pltpu.VMEM((n_chunks, 2 * RET_DIM, RET_DIM), BF16)],
        compiler_params=_params("arbitrary", "arbitrary"),
        name="retention_gn_gate",
    )(decay_logit.astype(F32), qa, ka_t, va, ga, gn_gain.reshape(1, RET_WIDTH).astype(F32))


def _winattn_kernel(sink_ref, q_ref, kp_ref, kc_ref, kn_ref, vp_ref, vc_ref, vn_ref, o_ref):
    step = pl.program_id(1)
    n_steps = pl.num_programs(1)
    W = WINDOW
    nq = q_ref.shape[1] // W
    group = WIN_Q_HEADS // WIN_KV_HEADS
    k_all = jnp.concatenate([kp_ref[0], kc_ref[0], kn_ref[0]], axis=0)
    v_all = jnp.concatenate([vp_ref[0], vc_ref[0], vn_ref[0]], axis=0)
    lo_half = lax.broadcasted_iota(I32, (1, LANES), 1) < WIN_DIM

    def placements(slab, g):
        first, second = slab[:, :LANES], slab[:, LANES:]
        zero = jnp.zeros_like(first)
        if g == 0:
            return jnp.where(lo_half, first, zero), jnp.where(lo_half, zero, second)
        return jnp.where(lo_half, second, zero), jnp.where(lo_half, zero, first)

    rows2 = lax.broadcasted_iota(I32, (2 * W, 1), 0)
    qi = lax.broadcasted_iota(I32, (2 * W, 3 * W), 0) & (W - 1)
    cj = lax.broadcasted_iota(I32, (2 * W, 3 * W), 1)
    rel = cj - qi
    band = (rel >= 0) & (rel <= 2 * W)

    for g in range(WIN_KV_HEADS):
        k_even, k_odd = placements(k_all, g)
        v_even, v_odd = placements(v_all, g)
        c0 = group * WIN_DIM * g
        sink_even = jnp.where(rows2 < W, sink_ref[0, group * g], sink_ref[0, group * g + 2])
        sink_odd = jnp.where(rows2 < W, sink_ref[0, group * g + 1], sink_ref[0, group * g + 3])
        for jq in range(nq):
            qrows = slice(jq * W, (jq + 1) * W)
            krows = slice(jq * W, (jq + 3) * W)
            mask = band
            if jq == 0:
                mask = mask & (cj >= jnp.where(step == 0, W, 0))
            if jq == nq - 1:
                mask = mask & (cj < jnp.where(step == n_steps - 1, 2 * W, 3 * W))
            q2 = jnp.concatenate([q_ref[0, qrows, c0:c0 + LANES],
                                  q_ref[0, qrows, c0 + LANES:c0 + 2 * LANES]], axis=0)

            def softmax_parts(k_placed, sink):
                s = lax.dot_general(q2, k_placed[krows], _NT, preferred_element_type=F32)
                s = jnp.where(mask, s, NEG_MASK)
                m = jnp.maximum(jnp.max(s, axis=1, keepdims=True), sink)
                e = jnp.exp(s - m)
                return e.astype(BF16), jnp.sum(e, axis=1, keepdims=True) + jnp.exp(sink - m)

            p_even, den_even = softmax_parts(k_even, sink_even)
            p_odd, den_odd = softmax_parts(k_odd, sink_odd)
            o = (jnp.dot(p_even, v_even[krows], preferred_element_type=F32)
                 + jnp.dot(p_odd, v_odd[krows], preferred_element_type=F32))
            o = (o / jnp.where(lo_half, den_even, den_odd)).astype(BF16)
            o_ref[0, qrows, c0:c0 + LANES] = o[:W]
            o_ref[0, qrows, c0 + LANES:c0 + 2 * LANES] = o[W:]


def _winattn(qb, kb2, vb2, sink_logit, blocks_per_step):
    B, S, _ = qb.shape
    nq = blocks_per_step
    nb = S // WINDOW
    prev = pl.BlockSpec((1, WINDOW, 2 * WIN_KV), lambda b, s: (b, jnp.maximum(s * nq - 1, 0), 0))
    cur = pl.BlockSpec((1, nq * WINDOW, 2 * WIN_KV), lambda b, s: (b, s, 0))
    nxt = pl.BlockSpec((1, WINDOW, 2 * WIN_KV), lambda b, s: (b, jnp.minimum((s + 1) * nq, nb - 1), 0))
    qspec = pl.BlockSpec((1, nq * WINDOW, WIN_WIDTH), lambda b, s: (b, s, 0))
    return pl.pallas_call(
        _winattn_kernel,
        grid=(B, nb // nq),
        in_specs=[pl.BlockSpec(memory_space=pltpu.SMEM), qspec, prev, cur, nxt, prev, cur, nxt],
        out_specs=qspec,
        out_shape=jax.ShapeDtypeStruct((B, S, WIN_WIDTH), BF16),
        compiler_params=_params("arbitrary", "arbitrary"),
        name="window_attention",
    )(sink_logit.reshape(1, WIN_Q_HEADS).astype(F32), qb, kb2, kb2, kb2, vb2, vb2, vb2)


def _chan_dft_kernel(xa_ref, xb_ref, m_ref, wr_ref, wi_ref):
    n1 = xa_ref.shape[1] // FFT_RADIX2
    for s in range(FFT_RADIX2):
        rows = pl.ds(s, n1, stride=FFT_RADIX2)
        w = (jnp.dot(xa_ref[0, rows, :].astype(BF16), m_ref[:LANES, :], preferred_element_type=F32)
             + jnp.dot(xb_ref[0, rows, :].astype(BF16), m_ref[LANES:, :], preferred_element_type=F32))
        wr_ref[0, s] = w[:, :GROUP_CH].astype(BF16)
        wi_ref[0, s] = w[:, GROUP_CH:].astype(BF16)


def _real_dft16(ar, ai):
    n = FFT_RADIX2
    cs = [math.cos(2 * math.pi * m / n) for m in range(n)]
    sn = [math.sin(2 * math.pi * m / n) for m in range(n)]

    def axpy(acc, coef, v):
        if abs(coef) < 1e-12:
            return acc
        if abs(abs(coef) - 1.0) < 1e-12:
            if acc is None:
                return v if coef > 0 else -v
            return acc + v if coef > 0 else acc - v
        t = coef * v
        return t if acc is None else acc + t

    p = {s: ar[s] + ar[n - s] for s in range(1, n // 2)}
    d = {s: ai[s] - ai[n - s] for s in range(1, n // 2)}
    base = (ar[0] + ar[n // 2], ar[0] - ar[n // 2])
    y = [None] * n
    for k in range(n // 2 + 1):
        e = base[k % 2]
        for s in range(1, n // 2):
            e = axpy(e, cs[(s * k) % n], p[s])
        if k in (0, n // 2):
            y[k] = e
            continue
        o = None
        for s in range(1, n // 2):
            o = axpy(o, sn[(s * k) % n], d[s])
        y[k] = e + o
        y[n - k] = e - o
    return y


def _seq_dft_kernel(wr_ref, wi_ref, cf_ref, sf_ref, y_ref, ar_scr, ai_scr):
    n1 = wr_ref.shape[2]
    for s in range(FFT_RADIX2):
        cf, sf, wr, wi = cf_ref[s], sf_ref[s], wr_ref[0, s], wi_ref[0, s]
        ar_scr[s] = (jnp.dot(cf, wr, preferred_element_type=F32)
                     + jnp.dot(sf, wi, preferred_element_type=F32))
        ai_scr[s] = (jnp.dot(cf, wi, preferred_element_type=F32)
                     - jnp.dot(sf, wr, preferred_element_type=F32))

    def rows8(r, carry):
        rows = pl.ds(pl.multiple_of(r * 8, 8), 8)
        y = _real_dft16([ar_scr[s, rows, :] for s in range(FFT_RADIX2)],
                        [ai_scr[s, rows, :] for s in range(FFT_RADIX2)])
        for k in range(FFT_RADIX2):
            y_ref[0, k, rows, :] = y[k].astype(BF16)
        return carry

    lax.fori_loop(0, n1 // 8, rows8, 0)


def _fourier_tables(seq):
    n1 = seq // FFT_RADIX2
    c = jnp.arange(GROUP_CH, dtype=I32)
    m = (c[:, None] * c[None, :]) % GROUP_CH
    ang = m.astype(F32) * (2.0 * math.pi / GROUP_CH)
    scale = (seq * GROUP_CH) ** -0.5
    chan = (jnp.concatenate([jnp.cos(ang), -jnp.sin(ang)], axis=1) * scale).astype(BF16)
    k1 = jnp.arange(n1, dtype=I32)[None, :, None]
    s1 = jnp.arange(n1, dtype=I32)[None, None, :]
    s2 = jnp.arange(FFT_RADIX2, dtype=I32)[:, None, None]
    m = (FFT_RADIX2 * s1 * k1 + s2 * k1) % seq
    ang = m.astype(F32) * (2.0 * math.pi / seq)
    return chan, jnp.cos(ang).astype(BF16), jnp.sin(ang).astype(BF16)


def _fourier(x, tables, tn=256):
    B, S, _ = x.shape
    n1 = S // FFT_RADIX2
    chan, cf, sf = tables
    wshape = jax.ShapeDtypeStruct((B, FFT_RADIX2, n1, D_MODEL), BF16)
    wblk = pl.BlockSpec((1, FFT_RADIX2, n1, GROUP_CH), lambda b, g: (b, 0, 0, g))
    wr, wi = pl.pallas_call(
        _chan_dft_kernel,
        grid=(B, FOURIER_GROUPS),
        in_specs=[pl.BlockSpec((1, S, LANES), lambda b, g: (b, 0, 2 * g)),
                  pl.BlockSpec((1, S, LANES), lambda b, g: (b, 0, 2 * g + 1)),
                  pl.BlockSpec((GROUP_CH, 2 * GROUP_CH), lambda b, g: (0, 0))],
        out_specs=[wblk, wblk],
        out_shape=[wshape, wshape],
        compiler_params=_params("arbitrary", "arbitrary"),
        name="fourier_channel_dft",
    )(x, x, chan)
    cols = pl.BlockSpec((1, FFT_RADIX2, n1, tn), lambda b, j: (b, 0, 0, j))
    full = pl.BlockSpec((FFT_RADIX2, n1, n1), lambda b, j: (0, 0, 0))
    y = pl.pallas_call(
        _seq_dft_kernel,
        grid=(B, D_MODEL // tn),
        in_specs=[cols, cols, full, full],
        out_specs=cols,
        out_shape=wshape,
        scratch_shapes=[pltpu.VMEM((FFT_RADIX2, n1, tn), F32), pltpu.VMEM((FFT_RADIX2, n1, tn), F32)],
        compiler_params=_params("arbitrary", "arbitrary"),
        name="fourier_sequence_dft",
    )(wr, wi, cf, sf)
    return y.reshape(B * S, D_MODEL)


def _proj_ln_route_kernel(a_ref, b_ref, x_ref, w_ref, gain_ref, bias_ref, wr_ref, br_ref,
                          xt_ref, ri_ref, rg_ref, cnt_ref, carry_scr):
    tm = x_ref.shape[0]
    half = a_ref.shape[1]

    @pl.when(pl.program_id(0) == 0)
    def _():
        carry_scr[...] = jnp.zeros_like(carry_scr)

    mix = (jnp.dot(a_ref[...], w_ref[:half, :], preferred_element_type=F32)
           + jnp.dot(b_ref[...], w_ref[half:, :], preferred_element_type=F32))
    y = _layer_norm(ALPHA * x_ref[...] + mix, gain_ref[...], bias_ref[...])
    _store_token_tiles(xt_ref, y)

    logits = jnp.dot(y.astype(BF16), wr_ref[...], preferred_element_type=F32) + br_ref[...]
    lane = lax.broadcasted_iota(I32, (tm, LANES), 1)
    lanef = lane.astype(F32)
    ninf = -jnp.inf
    big = 1e9

    def rmax(v):
        return jnp.max(v, axis=1, keepdims=True)

    def first_lane(hit):
        return jnp.min(jnp.where(hit, lanef, big), axis=1, keepdims=True)

    cmask = (lane >= N_EXPERTS) & (lane < N_EXPERTS + N_GROUPS)
    cl = jnp.where(cmask, logits, ninf)
    cmax = rmax(cl)
    group = first_lane(cl == cmax) - float(N_EXPERTS)
    p_group = 1.0 / jnp.sum(jnp.where(cmask, jnp.exp(cl - cmax), 0.0), axis=1, keepdims=True)
    lo = group * float(EXPERTS_PER_GROUP)
    fmask = (lanef >= lo) & (lanef < lo + float(EXPERTS_PER_GROUP))
    fl = jnp.where(fmask, logits, ninf)
    v1 = rmax(fl)
    e1 = first_lane(fl == v1)
    fl2 = jnp.where(lanef == e1, ninf, fl)
    v2 = rmax(fl2)
    e2 = first_lane(fl2 == v2)
    t = jnp.exp(v2 - v1)
    g1 = p_group / (1.0 + t)
    g2 = p_group * t / (1.0 + t)

    r = lax.broadcasted_iota(I32, (tm, tm), 0)
    cidx = lax.broadcasted_iota(I32, (tm, tm), 1)
    lower = jnp.where(r > cidx, 1.0, 0.0).astype(BF16)
    ranks = []
    for slot, chosen in enumerate((e1, e2)):
        hit = lanef == chosen
        onehot = jnp.where(hit, 1.0, 0.0)
        carry = carry_scr[slot:slot + 1, :]
        prefix = jnp.dot(lower, onehot.astype(BF16), preferred_element_type=F32) + carry
        ranks.append(jnp.sum(jnp.where(hit, prefix, 0.0), axis=1, keepdims=True))
        carry_scr[slot:slot + 1, :] = carry + jnp.sum(onehot, axis=0, keepdims=True)
    cnt_ref[...] = carry_scr[...]
    record = jnp.where(lane == 0, e1, jnp.where(lane == 1, e2, jnp.where(lane == 2, ranks[0], ranks[1])))
    ri_ref[...] = record.T[:ROUTE_ROWS, :].astype(I32)
    rg_ref[...] = jnp.where(lane == 0, g1, g2)


def _proj_ln_route(a, b, x2, w_bf16, gain, bias, w_router, b_router, tm):
    T = x2.shape[0]
    half = D_MODEL // 2
    row = lambda i: (i, 0)
    const = lambda i: (0, 0)
    a_spec = pl.BlockSpec((tm, half), row)
    b_spec = pl.BlockSpec((tm, half), row if b is not a else (lambda i: (i, 1)))
    return pl.pallas_call(
        _proj_ln_route_kernel,
        grid=(T // tm,),
        in_specs=[a_spec, b_spec, pl.BlockSpec((tm, D_MODEL), row),
                  pl.BlockSpec((D_MODEL, D_MODEL), const),
                  pl.BlockSpec((1, D_MODEL), const), pl.BlockSpec((1, D_MODEL), const),
                  pl.BlockSpec((D_MODEL, LANES), const), pl.BlockSpec((1, LANES), const)],
        out_specs=[pl.BlockSpec((tm * TOKEN_ROWS, LANES), row), pl.BlockSpec((ROUTE_ROWS, tm), lambda i: (0, i)),
                   pl.BlockSpec((tm, LANES), row), pl.BlockSpec((2, LANES), const)],
        out_shape=[jax.ShapeDtypeStruct((T * TOKEN_ROWS, LANES), F32), jax.ShapeDtypeStruct((ROUTE_ROWS, T), I32),
                   jax.ShapeDtypeStruct((T, LANES), F32), jax.ShapeDtypeStruct((2, LANES), F32)],
        scratch_shapes=[pltpu.VMEM((2, LANES), F32)],
        compiler_params=_params("arbitrary"),
        name="proj_ln_router",
    )(a, b, x2, w_bf16, gain, bias, w_router, b_router)


def _expert_kernel(te_ref, base_ref, nv_ref, src_ref, x_hbm, wg_ref, wu_ref, wd_ref, y_ref,
                   wg_scr, wu_scr, wd_scr, xbuf0, xbuf1, xbuf2, gsem):
    i = pl.program_id(0)
    nv = nv_ref[0]
    tile = xbuf0.shape[0] // TOKEN_ROWS
    xbufs = (xbuf0, xbuf1, xbuf2)
    n_buf = len(xbufs)

    def token_rows(first_row):
        return pl.ds(pl.multiple_of(first_row, TOKEN_ROWS), TOKEN_ROWS)

    def gather_row(base, r, q):
        return pltpu.make_async_copy(x_hbm.at[token_rows(src_ref[base + r]), :],
                                     xbufs[q].at[token_rows(r * TOKEN_ROWS), :], gsem.at[q])

    def wait_gather(q):
        pltpu.make_async_copy(x_hbm.at[pl.ds(0, tile * TOKEN_ROWS), :], xbufs[q], gsem.at[q]).wait()

    @pl.when(i == 0)
    def _():
        def first(r, carry):
            gather_row(base_ref[0], r, 0).start()
            gather_row(base_ref[1], r, 1).start()
            return carry

        lax.fori_loop(0, tile, first, 0, unroll=ROW_DMA_UNROLL)

    @pl.when(i < nv)
    def _():
        @pl.when((i == 0) | (te_ref[i] != te_ref[jnp.maximum(i - 1, 0)]))
        def _():
            wg_scr[...] = wg_ref[0, 0].astype(BF16)
            wu_scr[...] = wu_ref[0, 0].astype(BF16)
            wd_scr[...] = wd_ref[0, 0].astype(BF16)

        base_ahead = base_ref[i + 2]
        for q in range(n_buf):
            @pl.when(i % n_buf == q)
            def _():
                ahead = (q + 2) % n_buf
                wait_gather(q)
                for r in range(tile):
                    gather_row(base_ahead, r, ahead).start(priority=r % 2)
                xb = _load_token_tiles(xbufs[q]).astype(BF16)
                hg = jnp.dot(xb, wg_scr[...], preferred_element_type=F32)
                hu = jnp.dot(xb, wu_scr[...], preferred_element_type=F32)
                hid = (_silu(hg) * hu).astype(BF16)
                _store_token_tiles(y_ref, jnp.dot(hid, wd_scr[...], preferred_element_type=F32))

                @pl.when(i == nv - 1)
                def _():
                    wait_gather((q + 1) % n_buf)
                    wait_gather(ahead)

    @pl.when(i >= nv)
    def _():
        y_ref[...] = jnp.zeros_like(y_ref)


def _experts(x1t, src, tile_expert, tile_base, n_valid, w_gate, w_up, w_down, layer, tile):
    n_tiles = tile_expert.shape[0]
    wsel = lambda i, te, *_: (layer, te[i], 0, 0)
    buf = pltpu.VMEM((tile * TOKEN_ROWS, LANES), F32)
    return pl.pallas_call(
        _expert_kernel,
        grid_spec=pltpu.PrefetchScalarGridSpec(
            num_scalar_prefetch=4,
            grid=(n_tiles,),
            in_specs=[pl.BlockSpec(memory_space=pl.ANY),
                      pl.BlockSpec((1, 1, D_MODEL, EXPERT_HIDDEN), wsel),
                      pl.BlockSpec((1, 1, D_MODEL, EXPERT_HIDDEN), wsel),
                      pl.BlockSpec((1, 1, EXPERT_HIDDEN, D_MODEL), wsel)],
            out_specs=pl.BlockSpec((tile * TOKEN_ROWS, LANES), lambda i, *_: (i, 0)),
            scratch_shapes=[pltpu.VMEM((D_MODEL, EXPERT_HIDDEN), BF16),
                            pltpu.VMEM((D_MODEL, EXPERT_HIDDEN), BF16),
                            pltpu.VMEM((EXPERT_HIDDEN, D_MODEL), BF16),
                            buf, buf, buf, pltpu.SemaphoreType.DMA((3,))],
        ),
        out_shape=jax.ShapeDtypeStruct((n_tiles * tile * TOKEN_ROWS, LANES), F32),
        compiler_params=_params("arbitrary"),
        name="moe_experts",
    )(tile_expert, tile_base, n_valid, src, x1t, w_gate, w_up, w_down)


def _combine_ln_kernel(pos_cur, pos_nxt, pos_ahead, x_ref, g_ref, gain_ref, bias_ref, ys_hbm, o_ref,
                       buf0, buf1, buf2, sem):
    i = pl.program_id(0)
    n = pl.num_programs(0)
    tm = g_ref.shape[0]
    bufs = (buf0, buf1, buf2)
    n_buf = len(bufs)

    def gather_row(pos_ref, j, q):
        src = pl.ds(pl.multiple_of(pos_ref[0, 0, j], TOKEN_ROWS), TOKEN_ROWS)
        return pltpu.make_async_copy(ys_hbm.at[src, :], bufs[q].at[pl.ds(j * TOKEN_ROWS, TOKEN_ROWS), :],
                                     sem.at[q])

    def wait_gather(q):
        pltpu.make_async_copy(ys_hbm.at[pl.ds(0, 2 * tm * TOKEN_ROWS), :], bufs[q], sem.at[q]).wait()

    @pl.when(i == 0)
    def _():
        def first(j, carry):
            gather_row(pos_cur, j, 0).start()
            gather_row(pos_nxt, j, 1).start()
            return carry

        lax.fori_loop(0, 2 * tm, first, 0, unroll=ROW_DMA_UNROLL)

    for q in range(n_buf):
        @pl.when(i % n_buf == q)
        def _():
            ahead = (q + 2) % n_buf
            wait_gather(q)
            for j in range(2 * tm):
                gather_row(pos_ahead, j, ahead).start(priority=j % 2)
            g = g_ref[...]
            half = tm * TOKEN_ROWS
            ffn = (_load_token_tiles(bufs[q].at[pl.ds(0, half), :]) * g[:, 0:1]
                   + _load_token_tiles(bufs[q].at[pl.ds(half, half), :]) * g[:, 1:2])
            o_ref[...] = _layer_norm(ALPHA * _load_token_tiles(x_ref) + ffn, gain_ref[...], bias_ref[...])

            @pl.when(i == n - 1)
            def _():
                wait_gather((q + 1) % n_buf)
                wait_gather(ahead)


def _combine_ln(x1t, ys, pos, gates, gain, bias, tm):
    T = x1t.shape[0] // TOKEN_ROWS
    n = T // tm
    pos3 = jnp.stack([pos[0].reshape(n, tm), pos[1].reshape(n, tm)], axis=1).reshape(n, 1, 2 * tm)
    row = lambda i: (i, 0)
    const = lambda i: (0, 0)
    smem = lambda k: pl.BlockSpec((1, 1, 2 * tm), lambda i: (jnp.minimum(i + k, n - 1), 0, 0),
                                  memory_space=pltpu.SMEM)
    buf = pltpu.VMEM((2 * tm * TOKEN_ROWS, LANES), F32)
    return pl.pallas_call(
        _combine_ln_kernel,
        grid=(n,),
        in_specs=[smem(0), smem(1), smem(2),
                  pl.BlockSpec((tm * TOKEN_ROWS, LANES), row), pl.BlockSpec((tm, LANES), row),
                  pl.BlockSpec((1, D_MODEL), const), pl.BlockSpec((1, D_MODEL), const),
                  pl.BlockSpec(memory_space=pl.ANY)],
        out_specs=pl.BlockSpec((tm, D_MODEL), row),
        out_shape=jax.ShapeDtypeStruct((T, D_MODEL), F32),
        scratch_shapes=[buf, buf, buf, pltpu.SemaphoreType.DMA((3,))],
        compiler_params=_params("arbitrary"),
        name="moe_combine_ln",
    )(pos3, pos3, pos3, x1t, gates, gain, bias, ys)


def _moe(x1t, route_t, route_g, counts_f, w_gate, w_up, w_down, layer, gain, bias, tile, tm):
    T = x1t.shape[0] // TOKEN_ROWS
    n_pairs = 2 * T
    slot_counts = counts_f[:, :N_EXPERTS].astype(I32)
    counts = slot_counts[0] + slot_counts[1]
    eids = jnp.arange(N_EXPERTS, dtype=I32)
    expert, rank = route_t[0:2], route_t[2:4]
    key = expert.reshape(-1) * n_pairs + jnp.arange(n_pairs, dtype=I32)
    pair = jnp.sort(key) % n_pairs
    src = jnp.concatenate([pair % T, jnp.zeros((tile,), I32)]) * TOKEN_ROWS
    tiles_per = (counts + tile - 1) // tile
    tile_ends = jnp.cumsum(tiles_per)
    first_tile = tile_ends - tiles_per
    first_row = jnp.cumsum(counts) - counts
    n_tiles = n_pairs // tile + N_EXPERTS
    tile_ids = jnp.arange(n_tiles + 2, dtype=I32)
    owner = jnp.sum((tile_ids[:, None] >= tile_ends[None, :]).astype(I32), axis=1)
    valid = owner < N_EXPERTS
    last_expert = jnp.max(jnp.where(counts > 0, eids, 0))
    tile_expert = jnp.minimum(owner, last_expert).astype(I32)
    onehot = owner[:, None] == eids[None, :]
    pick = lambda v: jnp.sum(jnp.where(onehot, v[None, :], 0), axis=1)
    tile_base = pick(first_row) + (tile_ids - pick(first_tile)) * tile
    tile_base = jnp.where(valid, tile_base, n_pairs).astype(I32)
    n_valid = tile_ends[-1:].astype(I32)
    ys = _experts(x1t, src, tile_expert[:n_tiles], tile_base, n_valid, w_gate, w_up, w_down, layer, tile)
    start = jnp.stack([first_tile * tile, first_tile * tile + slot_counts[0]])
    offset = sum(jnp.where(expert == e, start[:, e:e + 1], 0) for e in range(N_EXPERTS))
    return _combine_ln(x1t, ys, ((offset + rank) * TOKEN_ROWS).astype(I32), route_g, gain, bias, tm)


def _pick(n, pref):
    t = min(n, pref)
    while n % t:
        t //= 2
    return t


def kernel(x, w_in_even, ret_decay_logit, ret_gn_gain, sink_logit, w_out_even, w_out_fourier,
           ln1_gain, ln1_bias, ln2_gain, ln2_bias, router_coarse_w, router_coarse_b,
           router_fine_w, router_fine_b, expert_w_gate, expert_w_up, expert_w_down):
    B, S, D = x.shape
    assert D == D_MODEL and S % (FFT_RADIX2 * 8) == 0 and S % WINDOW == 0 and S % RET_BLOCK == 0
    T = B * S
    tm = _pick(S, 512)
    tile = _pick(T, 512)
    rope_tabs = _rope_tables(S)
    fourier_tabs = _fourier_tables(S)
    row = lambda v: v.reshape(1, -1).astype(F32)

    x2 = x.reshape(T, D).astype(F32)
    for layer in range(DEPTH):
        if layer % 2 == 0:
            e = layer // 2
            qa, ka_t, va, ga, qb, kb2, vb2 = _inproj(x2, w_in_even[e].astype(BF16), rope_tabs, S, tm)
            shp = lambda v: v.reshape(B, S, v.shape[-1])
            ya = _retention(shp(qa), ka_t, shp(va), shp(ga), ret_decay_logit[e], ret_gn_gain[e])
            yb = _winattn(shp(qb), shp(kb2), shp(vb2), sink_logit[e], _pick(S // WINDOW, 4))
            a, b = ya.reshape(T, RET_WIDTH), yb.reshape(T, WIN_WIDTH)
            w_out = w_out_even[e]
        else:
            a = b = _fourier(x2.reshape(B, S, D), fourier_tabs)
            w_out = w_out_fourier[layer // 2]
        w_router = jnp.zeros((D, LANES), F32)
        w_router = w_router.at[:, :N_EXPERTS].set(router_fine_w[layer])
        w_router = w_router.at[:, N_EXPERTS:N_EXPERTS + N_GROUPS].set(router_coarse_w[layer])
        b_router = jnp.zeros((1, LANES), F32)
        b_router = b_router.at[0, :N_EXPERTS].set(router_fine_b[layer])
        b_router = b_router.at[0, N_EXPERTS:N_EXPERTS + N_GROUPS].set(router_coarse_b[layer])
        x1t, route_t, route_g, counts = _proj_ln_route(
            a, b, x2, w_out.astype(BF16), row(ln1_gain[layer]), row(ln1_bias[layer]),
            w_router.astype(BF16), b_router, tm)
        x2 = _moe(x1t, route_t, route_g, counts, expert_w_gate, expert_w_up, expert_w_down, layer,
                  row(ln2_gain[layer]), row(ln2_bias[layer]), tile, tm)
    return x2.reshape(B, S, D).astype(x.dtype)
```

```python
import math

import jax
import jax.numpy as jnp
from jax import lax
from jax.experimental import pallas as pl
from jax.experimental.pallas import tpu as pltpu

F32 = jnp.float32
BF16 = jnp.bfloat16
I32 = jnp.int32

D_MODEL = 1024
DEPTH = 4
RET_HEADS = 4
RET_DIM = 128
RET_BLOCK = 256
RET_WIDTH = RET_HEADS * RET_DIM
WIN_Q_HEADS = 8
WIN_KV_HEADS = 2
WIN_DIM = 64
WINDOW = 128
WIN_WIDTH = WIN_Q_HEADS * WIN_DIM
WIN_KV = WIN_KV_HEADS * WIN_DIM
FOURIER_GROUPS = 4
GROUP_CH = D_MODEL // FOURIER_GROUPS
FFT_RADIX2 = 16
ROPE_THETA = 10000.0
N_GROUPS = 4
EXPERTS_PER_GROUP = 8
N_EXPERTS = N_GROUPS * EXPERTS_PER_GROUP
EXPERT_HIDDEN = D_MODEL // 2
LN_EPS = 1e-5
GN_EPS = 1e-6
ALPHA = (2.0 * DEPTH) ** 0.25
IN_EVEN = 2 * RET_WIDTH + 2 * RET_WIDTH + WIN_WIDTH + 2 * WIN_KV

LANES = 128
TOKEN_ROWS = D_MODEL // LANES
VMEM_LIMIT_BYTES = 48 * 1024 * 1024
NEG_MASK = -1e30
LOG2E = math.log2(math.e)
ROUTE_ROWS = 8
ROW_DMA_UNROLL = 8

_NT = (((1,), (1,)), ((), ()))


def _params(*sem):
    return pltpu.CompilerParams(dimension_semantics=sem, vmem_limit_bytes=VMEM_LIMIT_BYTES)


def _silu(v):
    return v / (1.0 + jnp.exp(-v))


def _layer_norm(z, gain, bias):
    mu = jnp.mean(z, axis=-1, keepdims=True)
    zc = z - mu
    var = jnp.mean(zc * zc, axis=-1, keepdims=True)
    return zc * lax.rsqrt(var + LN_EPS) * gain + bias


def _store_token_tiles(ref, v):
    n = v.shape[0]
    for c in range(TOKEN_ROWS):
        ref[pl.ds(c, n, stride=TOKEN_ROWS), :] = v[:, c * LANES:(c + 1) * LANES]


def _load_token_tiles(ref):
    n = ref.shape[0] // TOKEN_ROWS
    return jnp.concatenate([ref[pl.ds(c, n, stride=TOKEN_ROWS), :] for c in range(TOKEN_ROWS)], axis=1)


def _rope128(h, cos, sin_signed):
    return h * cos + pltpu.roll(h, 64, 1) * sin_signed


def _rope64x2(h, cos, sin_lo, sin_hi):
    return h * cos + pltpu.roll(h, 96, 1) * sin_lo + pltpu.roll(h, 32, 1) * sin_hi


def _inproj_kernel(x_ref, w_ref, cr_ref, sr_ref, cw_ref, slo_ref, shi_ref,
                   qa_ref, ka_ref, va_ref, ga_ref, qb_ref, kb_ref, vb_ref):
    xb = x_ref[...].astype(BF16)

    def seg(lo, hi):
        return jnp.dot(xb, w_ref[:, lo:hi], preferred_element_type=F32)

    cr, sr = cr_ref[...], sr_ref[...]
    q = seg(0, RET_WIDTH)
    k = seg(RET_WIDTH, 2 * RET_WIDTH)
    for h in range(RET_HEADS):
        sl = slice(LANES * h, LANES * (h + 1))
        qa_ref[:, sl] = _rope128(q[:, sl], cr, sr).astype(BF16)
        ka_ref[sl, :] = (_rope128(k[:, sl], cr, sr) * RET_DIM ** -0.5).T.astype(BF16)
    va_ref[...] = seg(2 * RET_WIDTH, 3 * RET_WIDTH).astype(BF16)
    ga_ref[...] = _silu(seg(3 * RET_WIDTH, 4 * RET_WIDTH)).astype(BF16)
    cw, slo, shi = cw_ref[...], slo_ref[...], shi_ref[...]
    base = 4 * RET_WIDTH
    q = seg(base, base + WIN_WIDTH)
    for p in range(WIN_WIDTH // LANES):
        sl = slice(LANES * p, LANES * (p + 1))
        qb_ref[:, sl] = (_rope64x2(q[:, sl], cw, slo, shi) * (WIN_DIM ** -0.5 * LOG2E)).astype(BF16)
    kv = seg(base + WIN_WIDTH, base + WIN_WIDTH + 2 * WIN_KV)
    kb = _rope64x2(kv[:, :WIN_KV], cw, slo, shi)
    vb = kv[:, WIN_KV:]
    kb_ref[:, :WIN_KV] = kb.astype(BF16)
    kb_ref[:, WIN_KV:] = pltpu.roll(kb, WIN_DIM, 1).astype(BF16)
    vb_ref[:, :WIN_KV] = vb.astype(BF16)
    vb_ref[:, WIN_KV:] = pltpu.roll(vb, WIN_DIM, 1).astype(BF16)


def _rope_tables(seq):
    pos = jnp.arange(seq, dtype=F32)[:, None]
    half = RET_DIM // 2
    inv = ROPE_THETA ** (-jnp.arange(half, dtype=F32) / half)
    ang = pos * inv[None, :]
    cr = jnp.concatenate([jnp.cos(ang), jnp.cos(ang)], axis=1)
    sr = jnp.concatenate([-jnp.sin(ang), jnp.sin(ang)], axis=1)
    half = WIN_DIM // 2
    inv = ROPE_THETA ** (-jnp.arange(half, dtype=F32) / half)
    ang = pos * inv[None, :]
    c, s, z = jnp.cos(ang), jnp.sin(ang), jnp.zeros_like(ang)
    cw = jnp.concatenate([c, c, c, c], axis=1)
    slo = jnp.concatenate([-s, z, -s, z], axis=1)
    shi = jnp.concatenate([z, s, z, s], axis=1)
    return cr, sr, cw, slo, shi


def _inproj(x2, w_bf16, tables, seq, tm):
    T = x2.shape[0]
    nseq = seq // tm
    row = lambda i: (i, 0)
    tab = lambda i: (i % nseq, 0)
    widths = (RET_WIDTH, None, RET_WIDTH, RET_WIDTH, WIN_WIDTH, 2 * WIN_KV, 2 * WIN_KV)
    return pl.pallas_call(
        _inproj_kernel,
        grid=(T // tm,),
        in_specs=[pl.BlockSpec((tm, D_MODEL), row),
                  pl.BlockSpec((D_MODEL, IN_EVEN), lambda i: (0, 0))]
                 + [pl.BlockSpec((tm, LANES), tab)] * 5,
        out_specs=[pl.BlockSpec((tm, w), row) if w else pl.BlockSpec((RET_WIDTH, tm), lambda i: (0, i))
                   for w in widths],
        out_shape=[jax.ShapeDtypeStruct((T, w) if w else (RET_WIDTH, T), BF16) for w in widths],
        compiler_params=_params("arbitrary"),
        name="inproj_rope",
    )(x2, w_bf16, *tables)


def _retention_kernel(logit_ref, q_ref, kt_ref, v_ref, g_ref, gain_ref, o_ref,
                      dmat_scr, kv_scr, state_scr):
    h = pl.program_id(1)
    C = RET_BLOCK
    dk = RET_DIM
    n_chunks = q_ref.shape[1] // C

    def log_gamma(d):
        v = jnp.full((1, 1), logit_ref[d, h], F32)
        return -(jnp.maximum(-v, 0.0) + jnp.log(1.0 + jnp.exp(-jnp.abs(v))))

    lgf, lgb = log_gamma(0), log_gamma(1)
    i = lax.broadcasted_iota(I32, (C, C), 0)
    j = lax.broadcasted_iota(I32, (C, C), 1)
    diff = (i - j).astype(F32)
    dmat_scr[...] = jnp.where(diff >= 0, jnp.exp(lgf * jnp.maximum(diff, 0.0)),
                              jnp.exp(lgb * jnp.maximum(-diff, 0.0)))
    col = lax.broadcasted_iota(I32, (C, 1), 0).astype(F32)
    lane = lax.broadcasted_iota(I32, (1, C), 1).astype(F32)
    xi_f, xi_b = jnp.exp(lgf * (col + 1.0)), jnp.exp(lgb * (C - col))
    zeta_f, zeta_b = jnp.exp(lgf * (C - 1.0 - lane)), jnp.exp(lgb * lane)
    dec_f, dec_b = jnp.exp(lgf * C), jnp.exp(lgb * C)
    gain = gain_ref[...]

    def span(n):
        return pl.ds(pl.multiple_of(n * C, C), C)

    def kv_pass(n, carry):
        kt = kt_ref[:, span(n)].astype(F32)
        lhs = jnp.concatenate([(kt * zeta_f).astype(BF16), (kt * zeta_b).astype(BF16)], axis=0)
        kv_scr[n] = jnp.dot(lhs, v_ref[0, span(n), :], preferred_element_type=F32)
        return carry

    lax.fori_loop(0, n_chunks, kv_pass, 0, unroll=4)

    def scan_f(n, state):
        state_scr[n, :dk, :] = state.astype(BF16)
        return state * dec_f + kv_scr[n, :dk, :]

    def scan_b(t, state):
        n = n_chunks - 1 - t
        state_scr[n, dk:, :] = state.astype(BF16)
        return state * dec_b + kv_scr[n, dk:, :]

    zero = jnp.zeros((dk, dk), F32)
    lax.fori_loop(0, n_chunks, scan_f, zero)
    lax.fori_loop(0, n_chunks, scan_b, zero)

    def out_pass(n, carry):
        q = q_ref[0, span(n), :]
        s = jnp.dot(q, kt_ref[:, span(n)], preferred_element_type=F32)
        p = (s * dmat_scr[...]).astype(BF16)
        qf = q.astype(F32)
        qx = jnp.concatenate([(qf * xi_f).astype(BF16), (qf * xi_b).astype(BF16)], axis=1)
        y = (jnp.dot(p, v_ref[0, span(n), :], preferred_element_type=F32)
             + jnp.dot(qx, state_scr[n], preferred_element_type=F32))
        mu = jnp.mean(y, axis=-1, keepdims=True)
        yc = y - mu
        var = jnp.mean(yc * yc, axis=-1, keepdims=True)
        yn = yc * lax.rsqrt(var + GN_EPS) * gain
        o_ref[0, span(n), :] = (g_ref[0, span(n), :].astype(F32) * yn).astype(BF16)
        return carry

    lax.fori_loop(0, n_chunks, out_pass, 0, unroll=4)


def _retention(qa, ka_t, va, ga, decay_logit, gn_gain):
    B, S, _ = qa.shape
    n_chunks = S // RET_BLOCK
    head = pl.BlockSpec((1, S, RET_DIM), lambda b, h: (b, 0, h))
    return pl.pallas_call(
        _retention_kernel,
        grid=(B, RET_HEADS),
        in_specs=[pl.BlockSpec(memory_space=pltpu.SMEM), head,
                  pl.BlockSpec((RET_DIM, S), lambda b, h: (h, b)), head, head,
                  pl.BlockSpec((1, RET_DIM), lambda b, h: (0, h))],
        out_specs=head,
        out_shape=jax.ShapeDtypeStruct((B, S, RET_WIDTH), BF16),
        scratch_shapes=[pltpu.VMEM((RET_BLOCK, RET_BLOCK), F32),
                        pltpu.VMEM((n_chunks, 2 * RET_DIM, RET_DIM), F32),
                        pltpu.VMEM((n_chunks, 2 * RET_DIM, RET_DIM), BF16)],
        compiler_params=_params("arbitrary", "arbitrary"),
        name="retention_gn_gate",
    )(decay_logit.astype(F32), qa, ka_t, va, ga, gn_gain.reshape(1, RET_WIDTH).astype(F32))


def _winattn_kernel(sink_ref, q_ref, kp_ref, kc_ref, kn_ref, vp_ref, vc_ref, vn_ref, o_ref):
    step = pl.program_id(1)
    n_steps = pl.num_programs(1)
    W = WINDOW
    nq = q_ref.shape[1] // W
    group = WIN_Q_HEADS // WIN_KV_HEADS
    k_all = jnp.concatenate([kp_ref[0], kc_ref[0], kn_ref[0]], axis=0)
    v_all = jnp.concatenate([vp_ref[0], vc_ref[0], vn_ref[0]], axis=0)
    lo_half = lax.broadcasted_iota(I32, (1, LANES), 1) < WIN_DIM

    def placements(slab, g):
        first, second = slab[:, :LANES], slab[:, LANES:]
        zero = jnp.zeros_like(first)
        if g == 0:
            return jnp.where(lo_half, first, zero), jnp.where(lo_half, zero, second)
        return jnp.where(lo_half, second, zero), jnp.where(lo_half, zero, first)

    rows2 = lax.broadcasted_iota(I32, (2 * W, 1), 0)
    qi = lax.broadcasted_iota(I32, (2 * W, 3 * W), 0) & (W - 1)
    cj = lax.broadcasted_iota(I32, (2 * W, 3 * W), 1)
    rel = cj - qi
    band = (rel >= 0) & (rel <= 2 * W)

    for g in range(WIN_KV_HEADS):
        k_even, k_odd = placements(k_all, g)
        v_even, v_odd = placements(v_all, g)
        c0 = group * WIN_DIM * g
        sink_even = jnp.where(rows2 < W, sink_ref[0, group * g], sink_ref[0, group * g + 2])
        sink_odd = jnp.where(rows2 < W, sink_ref[0, group * g + 1], sink_ref[0, group * g + 3])
        for jq in range(nq):
            qrows = slice(jq * W, (jq + 1) * W)
            krows = slice(jq * W, (jq + 3) * W)
            mask = band
            if jq == 0:
                mask = mask & (cj >= jnp.where(step == 0, W, 0))
            if jq == nq - 1:
                mask = mask & (cj < jnp.where(step == n_steps - 1, 2 * W, 3 * W))
            q2 = jnp.concatenate([q_ref[0, qrows, c0:c0 + LANES],
                                  q_ref[0, qrows, c0 + LANES:c0 + 2 * LANES]], axis=0)

            def softmax_parts(k_placed, sink):
                s = lax.dot_general(q2, k_placed[krows], _NT, preferred_element_type=F32)
                s = jnp.concatenate([jnp.where(mask[:, :W], s[:, :W], NEG_MASK), s[:, W:2 * W],
                                     jnp.where(mask[:, 2 * W:], s[:, 2 * W:], NEG_MASK)], axis=1)
                m = jnp.maximum(jnp.max(s, axis=1, keepdims=True), sink)
                e = jnp.exp2(s - m)
                return e.astype(BF16), jnp.sum(e, axis=1, keepdims=True) + jnp.exp2(sink - m)

            p_even, den_even = softmax_parts(k_even, sink_even)
            p_odd, den_odd = softmax_parts(k_odd, sink_odd)
            o = (jnp.dot(p_even, v_even[krows], preferred_element_type=F32)
                 + jnp.dot(p_odd, v_odd[krows], preferred_element_type=F32))
            o = (o / jnp.where(lo_half, den_even, den_odd)).astype(BF16)
            o_ref[0, qrows, c0:c0 + LANES] = o[:W]
            o_ref[0, qrows, c0 + LANES:c0 + 2 * LANES] = o[W:]


def _winattn(qb, kb2, vb2, sink_logit, blocks_per_step):
    B, S, _ = qb.shape
    nq = blocks_per_step
    nb = S // WINDOW
    prev = pl.BlockSpec((1, WINDOW, 2 * WIN_KV), lambda b, s: (b, jnp.maximum(s * nq - 1, 0), 0))
    cur = pl.BlockSpec((1, nq * WINDOW, 2 * WIN_KV), lambda b, s: (b, s, 0))
    nxt = pl.BlockSpec((1, WINDOW, 2 * WIN_KV), lambda b, s: (b, jnp.minimum((s + 1) * nq, nb - 1), 0))
    qspec = pl.BlockSpec((1, nq * WINDOW, WIN_WIDTH), lambda b, s: (b, s, 0))
    return pl.pallas_call(
        _winattn_kernel,
        grid=(B, nb // nq),
        in_specs=[pl.BlockSpec(memory_space=pltpu.SMEM), qspec, prev, cur, nxt, prev, cur, nxt],
        out_specs=qspec,
        out_shape=jax.ShapeDtypeStruct((B, S, WIN_WIDTH), BF16),
        compiler_params=_params("arbitrary", "arbitrary"),
        name="window_attention",
    )(sink_logit.reshape(1, WIN_Q_HEADS).astype(F32) * LOG2E, qb, kb2, kb2, kb2, vb2, vb2, vb2)


def _chan_dft_kernel(xa_ref, xb_ref, m_ref, wr_ref, wi_ref):
    n1 = xa_ref.shape[1] // FFT_RADIX2
    for s in range(FFT_RADIX2):
        rows = pl.ds(s, n1, stride=FFT_RADIX2)
        w = (jnp.dot(xa_ref[0, rows, :].astype(BF16), m_ref[:LANES, :], preferred_element_type=F32)
             + jnp.dot(xb_ref[0, rows, :].astype(BF16), m_ref[LANES:, :], preferred_element_type=F32))
        wr_ref[0, s] = w[:, :GROUP_CH].astype(BF16)
        wi_ref[0, s] = w[:, GROUP_CH:].astype(BF16)


def _real_dft16(ar, ai):
    n = FFT_RADIX2
    cs = [math.cos(2 * math.pi * m / n) for m in range(n)]
    sn = [math.sin(2 * math.pi * m / n) for m in range(n)]

    def axpy(acc, coef, v):
        if abs(coef) < 1e-12:
            return acc
        if abs(abs(coef) - 1.0) < 1e-12:
            if acc is None:
                return v if coef > 0 else -v
            return acc + v if coef > 0 else acc - v
        t = coef * v
        return t if acc is None else acc + t

    p = {s: ar[s] + ar[n - s] for s in range(1, n // 2)}
    d = {s: ai[s] - ai[n - s] for s in range(1, n // 2)}
    base = (ar[0] + ar[n // 2], ar[0] - ar[n // 2])
    y = [None] * n
    for k in range(n // 2 + 1):
        e = base[k % 2]
        for s in range(1, n // 2):
            e = axpy(e, cs[(s * k) % n], p[s])
        if k in (0, n // 2):
            y[k] = e
            continue
        o = None
        for s in range(1, n // 2):
            o = axpy(o, sn[(s * k) % n], d[s])
        y[k] = e + o
        y[n - k] = e - o
    return y


def _seq_dft_kernel(wr_ref, wi_ref, cf_ref, sf_ref, y_ref, ar_scr, ai_scr):
    n1 = wr_ref.shape[2]
    for s in range(FFT_RADIX2):
        cf, sf, wr, wi = cf_ref[s], sf_ref[s], wr_ref[0, s], wi_ref[0, s]
        ar_scr[s] = (jnp.dot(cf, wr, preferred_element_type=F32)
                     + jnp.dot(sf, wi, preferred_element_type=F32))
        ai_scr[s] = (jnp.dot(cf, wi, preferred_element_type=F32)
                     - jnp.dot(sf, wr, preferred_element_type=F32))

    def rows8(r, carry):
        rows = pl.ds(pl.multiple_of(r * 8, 8), 8)
        y = _real_dft16([ar_scr[s, rows, :] for s in range(FFT_RADIX2)],
                        [ai_scr[s, rows, :] for s in range(FFT_RADIX2)])
        for k in range(FFT_RADIX2):
            y_ref[0, k, rows, :] = y[k].astype(BF16)
        return carry

    lax.fori_loop(0, n1 // 8, rows8, 0)


def _fourier_tables(seq):
    n1 = seq // FFT_RADIX2
    c = jnp.arange(GROUP_CH, dtype=I32)
    m = (c[:, None] * c[None, :]) % GROUP_CH
    ang = m.astype(F32) * (2.0 * math.pi / GROUP_CH)
    scale = (seq * GROUP_CH) ** -0.5
    chan = (jnp.concatenate([jnp.cos(ang), -jnp.sin(ang)], axis=1) * scale).astype(BF16)
    k1 = jnp.arange(n1, dtype=I32)[None, :, None]
    s1 = jnp.arange(n1, dtype=I32)[None, None, :]
    s2 = jnp.arange(FFT_RADIX2, dtype=I32)[:, None, None]
    m = (FFT_RADIX2 * s1 * k1 + s2 * k1) % seq
    ang = m.astype(F32) * (2.0 * math.pi / seq)
    return chan, jnp.cos(ang).astype(BF16), jnp.sin(ang).astype(BF16)


def _fourier(x, tables, tn=256):
    B, S, _ = x.shape
    n1 = S // FFT_RADIX2
    chan, cf, sf = tables
    wshape = jax.ShapeDtypeStruct((B, FFT_RADIX2, n1, D_MODEL), BF16)
    wblk = pl.BlockSpec((1, FFT_RADIX2, n1, GROUP_CH), lambda b, g: (b, 0, 0, g))
    wr, wi = pl.pallas_call(
        _chan_dft_kernel,
        grid=(B, FOURIER_GROUPS),
        in_specs=[pl.BlockSpec((1, S, LANES), lambda b, g: (b, 0, 2 * g)),
                  pl.BlockSpec((1, S, LANES), lambda b, g: (b, 0, 2 * g + 1)),
                  pl.BlockSpec((GROUP_CH, 2 * GROUP_CH), lambda b, g: (0, 0))],
        out_specs=[wblk, wblk],
        out_shape=[wshape, wshape],
        compiler_params=_params("arbitrary", "arbitrary"),
        name="fourier_channel_dft",
    )(x, x, chan)
    cols = pl.BlockSpec((1, FFT_RADIX2, n1, tn), lambda b, j: (b, 0, 0, j))
    full = pl.BlockSpec((FFT_RADIX2, n1, n1), lambda b, j: (0, 0, 0))
    y = pl.pallas_call(
        _seq_dft_kernel,
        grid=(B, D_MODEL // tn),
        in_specs=[cols, cols, full, full],
        out_specs=cols,
        out_shape=wshape,
        scratch_shapes=[pltpu.VMEM((FFT_RADIX2, n1, tn), F32), pltpu.VMEM((FFT_RADIX2, n1, tn), F32)],
        compiler_params=_params("arbitrary", "arbitrary"),
        name="fourier_sequence_dft",
    )(wr, wi, cf, sf)
    return y.reshape(B * S, D_MODEL)


def _proj_ln_route_kernel(a_ref, b_ref, x_ref, w_ref, gain_ref, bias_ref, wr_ref, br_ref,
                          xt_ref, ri_ref, rg_ref, cnt_ref, carry_scr):
    tm = x_ref.shape[0]
    half = a_ref.shape[1]

    @pl.when(pl.program_id(0) == 0)
    def _():
        carry_scr[...] = jnp.zeros_like(carry_scr)

    mix = (jnp.dot(a_ref[...], w_ref[:half, :], preferred_element_type=F32)
           + jnp.dot(b_ref[...], w_ref[half:, :], preferred_element_type=F32))
    y = _layer_norm(ALPHA * x_ref[...] + mix, gain_ref[...], bias_ref[...])
    _store_token_tiles(xt_ref, y)

    logits = jnp.dot(y.astype(BF16), wr_ref[...], preferred_element_type=F32) + br_ref[...]
    lane = lax.broadcasted_iota(I32, (tm, LANES), 1)
    lanef = lane.astype(F32)
    ninf = -jnp.inf
    big = 1e9

    def rmax(v):
        return jnp.max(v, axis=1, keepdims=True)

    def first_lane(hit):
        return jnp.min(jnp.where(hit, lanef, big), axis=1, keepdims=True)

    cmask = (lane >= N_EXPERTS) & (lane < N_EXPERTS + N_GROUPS)
    cl = jnp.where(cmask, logits, ninf)
    cmax = rmax(cl)
    group = first_lane(cl == cmax) - float(N_EXPERTS)
    p_group = 1.0 / jnp.sum(jnp.where(cmask, jnp.exp(cl - cmax), 0.0), axis=1, keepdims=True)
    lo = group * float(EXPERTS_PER_GROUP)
    fmask = (lanef >= lo) & (lanef < lo + float(EXPERTS_PER_GROUP))
    fl = jnp.where(fmask, logits, ninf)
    v1 = rmax(fl)
    e1 = first_lane(fl == v1)
    fl2 = jnp.where(lanef == e1, ninf, fl)
    v2 = rmax(fl2)
    e2 = first_lane(fl2 == v2)
    t = jnp.exp(v2 - v1)
    g1 = p_group / (1.0 + t)
    g2 = p_group * t / (1.0 + t)

    r = lax.broadcasted_iota(I32, (tm, tm), 0)
    cidx = lax.broadcasted_iota(I32, (tm, tm), 1)
    lower = jnp.where(r > cidx, 1.0, 0.0).astype(BF16)
    ranks = []
    for slot, chosen in enumerate((e1, e2)):
        hit = lanef == chosen
        onehot = jnp.where(hit, 1.0, 0.0)
        carry = carry_scr[slot:slot + 1, :]
        prefix = jnp.dot(lower, onehot.astype(BF16), preferred_element_type=F32) + carry
        ranks.append(jnp.sum(jnp.where(hit, prefix, 0.0), axis=1, keepdims=True))
        carry_scr[slot:slot + 1, :] = carry + jnp.sum(onehot, axis=0, keepdims=True)
    cnt_ref[...] = carry_scr[...]
    record = jnp.where(lane == 0, e1, jnp.where(lane == 1, e2, jnp.where(lane == 2, ranks[0], ranks[1])))
    ri_ref[...] = record.T[:ROUTE_ROWS, :].astype(I32)
    rg_ref[...] = jnp.where(lane == 0, g1, g2)


def _proj_ln_route(a, b, x2, w_bf16, gain, bias, w_router, b_router, tm):
    T = x2.shape[0]
    half = D_MODEL // 2
    row = lambda i: (i, 0)
    const = lambda i: (0, 0)
    a_spec = pl.BlockSpec((tm, half), row)
    b_spec = pl.BlockSpec((tm, half), row if b is not a else (lambda i: (i, 1)))
    return pl.pallas_call(
        _proj_ln_route_kernel,
        grid=(T // tm,),
        in_specs=[a_spec, b_spec, pl.BlockSpec((tm, D_MODEL), row),
                  pl.BlockSpec((D_MODEL, D_MODEL), const),
                  pl.BlockSpec((1, D_MODEL), const), pl.BlockSpec((1, D_MODEL), const),
                  pl.BlockSpec((D_MODEL, LANES), const), pl.BlockSpec((1, LANES), const)],
        out_specs=[pl.BlockSpec((tm * TOKEN_ROWS, LANES), row), pl.BlockSpec((ROUTE_ROWS, tm), lambda i: (0, i)),
                   pl.BlockSpec((tm, LANES), row), pl.BlockSpec((2, LANES), const)],
        out_shape=[jax.ShapeDtypeStruct((T * TOKEN_ROWS, LANES), F32), jax.ShapeDtypeStruct((ROUTE_ROWS, T), I32),
                   jax.ShapeDtypeStruct((T, LANES), F32), jax.ShapeDtypeStruct((2, LANES), F32)],
        scratch_shapes=[pltpu.VMEM((2, LANES), F32)],
        compiler_params=_params("arbitrary"),
        name="proj_ln_router",
    )(a, b, x2, w_bf16, gain, bias, w_router, b_router)


def _expert_kernel(te_ref, base_ref, nv_ref, src_ref, x_hbm, wg_ref, wu_ref, wd_ref, y_ref,
                   wg_scr, wu_scr, wd_scr, xbuf0, xbuf1, xbuf2, gsem):
    i = pl.program_id(0)
    nv = nv_ref[0]
    tile = xbuf0.shape[0] // TOKEN_ROWS
    xbufs = (xbuf0, xbuf1, xbuf2)
    n_buf = len(xbufs)

    def token_rows(first_row):
        return pl.ds(pl.multiple_of(first_row, TOKEN_ROWS), TOKEN_ROWS)

    def gather_row(base, r, q):
        return pltpu.make_async_copy(x_hbm.at[token_rows(src_ref[base + r]), :],
                                     xbufs[q].at[token_rows(r * TOKEN_ROWS), :], gsem.at[q])

    def wait_gather(q):
        pltpu.make_async_copy(x_hbm.at[pl.ds(0, tile * TOKEN_ROWS), :], xbufs[q], gsem.at[q]).wait()

    @pl.when(i == 0)
    def _():
        def first(r, carry):
            gather_row(base_ref[0], r, 0).start()
            gather_row(base_ref[1], r, 1).start()
            return carry

        lax.fori_loop(0, tile, first, 0, unroll=ROW_DMA_UNROLL)

    @pl.when(i < nv)
    def _():
        @pl.when((i == 0) | (te_ref[i] != te_ref[jnp.maximum(i - 1, 0)]))
        def _():
            wg_scr[...] = wg_ref[0, 0].astype(BF16)
            wu_scr[...] = wu_ref[0, 0].astype(BF16)
            wd_scr[...] = wd_ref[0, 0].astype(BF16)

        base_ahead = base_ref[i + 2]
        for q in range(n_buf):
            @pl.when(i % n_buf == q)
            def _():
                ahead = (q + 2) % n_buf
                wait_gather(q)
                for r in range(tile):
                    gather_row(base_ahead, r, ahead).start(priority=r % 2)
                xb = _load_token_tiles(xbufs[q]).astype(BF16)
                hg = jnp.dot(xb, wg_scr[...], preferred_element_type=F32)
                hu = jnp.dot(xb, wu_scr[...], preferred_element_type=F32)
                hid = (_silu(hg) * hu).astype(BF16)
                _store_token_tiles(y_ref, jnp.dot(hid, wd_scr[...], preferred_element_type=F32))

                @pl.when(i == nv - 1)
                def _():
                    wait_gather((q + 1) % n_buf)
                    wait_gather(ahead)

    @pl.when(i >= nv)
    def _():
        y_ref[...] = jnp.zeros_like(y_ref)


def _experts(x1t, src, tile_expert, tile_base, n_valid, w_gate, w_up, w_down, layer, tile):
    n_tiles = tile_expert.shape[0]
    wsel = lambda i, te, *_: (layer, te[i], 0, 0)
    buf = pltpu.VMEM((tile * TOKEN_ROWS, LANES), F32)
    return pl.pallas_call(
        _expert_kernel,
        grid_spec=pltpu.PrefetchScalarGridSpec(
            num_scalar_prefetch=4,
            grid=(n_tiles,),
            in_specs=[pl.BlockSpec(memory_space=pl.ANY),
                      pl.BlockSpec((1, 1, D_MODEL, EXPERT_HIDDEN), wsel),
                      pl.BlockSpec((1, 1, D_MODEL, EXPERT_HIDDEN), wsel),
                      pl.BlockSpec((1, 1, EXPERT_HIDDEN, D_MODEL), wsel)],
            out_specs=pl.BlockSpec((tile * TOKEN_ROWS, LANES), lambda i, *_: (i, 0)),
            scratch_shapes=[pltpu.VMEM((D_MODEL, EXPERT_HIDDEN), BF16),
                            pltpu.VMEM((D_MODEL, EXPERT_HIDDEN), BF16),
                            pltpu.VMEM((EXPERT_HIDDEN, D_MODEL), BF16),
                            buf, buf, buf, pltpu.SemaphoreType.DMA((3,))],
        ),
        out_shape=jax.ShapeDtypeStruct((n_tiles * tile * TOKEN_ROWS, LANES), F32),
        compiler_params=_params("arbitrary"),
        name="moe_experts",
    )(tile_expert, tile_base, n_valid, src, x1t, w_gate, w_up, w_down)


def _combine_ln_kernel(pos_cur, pos_nxt, pos_ahead, x_ref, g_ref, gain_ref, bias_ref, ys_hbm, o_ref,
                       buf0, buf1, buf2, sem):
    i = pl.program_id(0)
    n = pl.num_programs(0)
    tm = g_ref.shape[0]
    bufs = (buf0, buf1, buf2)
    n_buf = len(bufs)

    def gather_row(pos_ref, j, q):
        src = pl.ds(pl.multiple_of(pos_ref[0, 0, j], TOKEN_ROWS), TOKEN_ROWS)
        return pltpu.make_async_copy(ys_hbm.at[src, :], bufs[q].at[pl.ds(j * TOKEN_ROWS, TOKEN_ROWS), :],
                                     sem.at[q])

    def wait_gather(q):
        pltpu.make_async_copy(ys_hbm.at[pl.ds(0, 2 * tm * TOKEN_ROWS), :], bufs[q], sem.at[q]).wait()

    @pl.when(i == 0)
    def _():
        def first(j, carry):
            gather_row(pos_cur, j, 0).start()
            gather_row(pos_nxt, j, 1).start()
            return carry

        lax.fori_loop(0, 2 * tm, first, 0, unroll=ROW_DMA_UNROLL)

    for q in range(n_buf):
        @pl.when(i % n_buf == q)
        def _():
            ahead = (q + 2) % n_buf
            wait_gather(q)
            for j in range(2 * tm):
                gather_row(pos_ahead, j, ahead).start(priority=j % 2)
            g = g_ref[...]
            half = tm * TOKEN_ROWS
            ffn = (_load_token_tiles(bufs[q].at[pl.ds(0, half), :]) * g[:, 0:1]
                   + _load_token_tiles(bufs[q].at[pl.ds(half, half), :]) * g[:, 1:2])
            o_ref[...] = _layer_norm(ALPHA * _load_token_tiles(x_ref) + ffn, gain_ref[...], bias_ref[...])

            @pl.when(i == n - 1)
            def _():
                wait_gather((q + 1) % n_buf)
                wait_gather(ahead)


def _combine_ln(x1t, ys, pos, gates, gain, bias, tm):
    T = x1t.shape[0] // TOKEN_ROWS
    n = T // tm
    pos3 = jnp.stack([pos[0].reshape(n, tm), pos[1].reshape(n, tm)], axis=1).reshape(n, 1, 2 * tm)
    row = lambda i: (i, 0)
    const = lambda i: (0, 0)
    smem = lambda k: pl.BlockSpec((1, 1, 2 * tm), lambda i: (jnp.minimum(i + k, n - 1), 0, 0),
                                  memory_space=pltpu.SMEM)
    buf = pltpu.VMEM((2 * tm * TOKEN_ROWS, LANES), F32)
    return pl.pallas_call(
        _combine_ln_kernel,
        grid=(n,),
        in_specs=[smem(0), smem(1), smem(2),
                  pl.BlockSpec((tm * TOKEN_ROWS, LANES), row), pl.BlockSpec((tm, LANES), row),
                  pl.BlockSpec((1, D_MODEL), const), pl.BlockSpec((1, D_MODEL), const),
                  pl.BlockSpec(memory_space=pl.ANY)],
        out_specs=pl.BlockSpec((tm, D_MODEL), row),
        out_shape=jax.ShapeDtypeStruct((T, D_MODEL), F32),
        scratch_shapes=[buf, buf, buf, pltpu.SemaphoreType.DMA((3,))],
        compiler_params=_params("arbitrary"),
        name="moe_combine_ln",
    )(pos3, pos3, pos3, x1t, gates, gain, bias, ys)


def _moe(x1t, route_t, route_g, counts_f, w_gate, w_up, w_down, layer, gain, bias, tile, tm):
    T = x1t.shape[0] // TOKEN_ROWS
    n_pairs = 2 * T
    slot_counts = counts_f[:, :N_EXPERTS].astype(I32)
    counts = slot_counts[0] + slot_counts[1]
    eids = jnp.arange(N_EXPERTS, dtype=I32)
    expert, rank = route_t[0:2], route_t[2:4]
    key = expert.reshape(-1) * n_pairs + jnp.arange(n_pairs, dtype=I32)
    pair = jnp.sort(key) % n_pairs
    src = jnp.concatenate([pair % T, jnp.zeros((tile,), I32)]) * TOKEN_ROWS
    tiles_per = (counts + tile - 1) // tile
    tile_ends = jnp.cumsum(tiles_per)
    first_tile = tile_ends - tiles_per
    first_row = jnp.cumsum(counts) - counts
    n_tiles = n_pairs // tile + N_EXPERTS
    tile_ids = jnp.arange(n_tiles + 2, dtype=I32)
    owner = jnp.sum((tile_ids[:, None] >= tile_ends[None, :]).astype(I32), axis=1)
    valid = owner < N_EXPERTS
    last_expert = jnp.max(jnp.where(counts > 0, eids, 0))
    tile_expert = jnp.minimum(owner, last_expert).astype(I32)
    onehot = owner[:, None] == eids[None, :]
    pick = lambda v: jnp.sum(jnp.where(onehot, v[None, :], 0), axis=1)
    tile_base = pick(first_row) + (tile_ids - pick(first_tile)) * tile
    tile_base = jnp.where(valid, tile_base, n_pairs).astype(I32)
    n_valid = tile_ends[-1:].astype(I32)
    ys = _experts(x1t, src, tile_expert[:n_tiles], tile_base, n_valid, w_gate, w_up, w_down, layer, tile)
    start = jnp.stack([first_tile * tile, first_tile * tile + slot_counts[0]])
    offset = sum(jnp.where(expert == e, start[:, e:e + 1], 0) for e in range(N_EXPERTS))
    return _combine_ln(x1t, ys, ((offset + rank) * TOKEN_ROWS).astype(I32), route_g, gain, bias, tm)


def _pick(n, pref):
    t = min(n, pref)
    while n % t:
        t //= 2
    return t


def kernel(x, w_in_even, ret_decay_logit, ret_gn_gain, sink_logit, w_out_even, w_out_fourier,
           ln1_gain, ln1_bias, ln2_gain, ln2_bias, router_coarse_w, router_coarse_b,
           router_fine_w, router_fine_b, expert_w_gate, expert_w_up, expert_w_down):
    B, S, D = x.shape
    assert D == D_MODEL and S % (FFT_RADIX2 * 8) == 0 and S % WINDOW == 0 and S % RET_BLOCK == 0
    T = B * S
    tm = _pick(S, 512)
    tile = _pick(T, 512)
    rope_tabs = _rope_tables(S)
    fourier_tabs = _fourier_tables(S)
    row = lambda v: v.reshape(1, -1).astype(F32)

    x2 = x.reshape(T, D).astype(F32)
    for layer in range(DEPTH):
        if layer % 2 == 0:
            e = layer // 2
            qa, ka_t, va, ga, qb, kb2, vb2 = _inproj(x2, w_in_even[e].astype(BF16), rope_tabs, S, tm)
            shp = lambda v: v.reshape(B, S, v.shape[-1])
            ya = _retention(shp(qa), ka_t, shp(va), shp(ga), ret_decay_logit[e], ret_gn_gain[e])
            yb = _winattn(shp(qb), shp(kb2), shp(vb2), sink_logit[e], _pick(S // WINDOW, 8))
            a, b = ya.reshape(T, RET_WIDTH), yb.reshape(T, WIN_WIDTH)
            w_out = w_out_even[e]
        else:
            a = b = _fourier(x2.reshape(B, S, D), fourier_tabs)
            w_out = w_out_fourier[layer // 2]
        w_router = jnp.zeros((D, LANES), F32)
        w_router = w_router.at[:, :N_EXPERTS].set(router_fine_w[layer])
        w_router = w_router.at[:, N_EXPERTS:N_EXPERTS + N_GROUPS].set(router_coarse_w[layer])
        b_router = jnp.zeros((1, LANES), F32)
        b_router = b_router.at[0, :N_EXPERTS].set(router_fine_b[layer])
        b_router = b_router.at[0, N_EXPERTS:N_EXPERTS + N_GROUPS].set(router_coarse_b[layer])
        x1t, route_t, route_g, counts = _proj_ln_route(
            a, b, x2, w_out.astype(BF16), row(ln1_gain[layer]), row(ln1_bias[layer]),
            w_router.astype(BF16), b_router, tm)
        x2 = _moe(x1t, route_t, route_g, counts, expert_w_gate, expert_w_up, expert_w_down, layer,
                  row(ln2_gain[layer]), row(ln2_bias[layer]), tile, tm)
    return x2.reshape(B, S, D).astype(x.dtype)
```

```python
import math

import jax
import jax.numpy as jnp
from jax import lax
from jax.experimental import pallas as pl
from jax.experimental.pallas import tpu as pltpu

F32 = jnp.float32
BF16 = jnp.bfloat16
I32 = jnp.int32

D_MODEL = 1024
DEPTH = 4
RET_HEADS = 4
RET_DIM = 128
RET_BLOCK = 256
RET_WIDTH = RET_HEADS * RET_DIM
WIN_Q_HEADS = 8
WIN_KV_HEADS = 2
WIN_DIM = 64
WINDOW = 128
WIN_WIDTH = WIN_Q_HEADS * WIN_DIM
WIN_KV = WIN_KV_HEADS * WIN_DIM
FOURIER_GROUPS = 4
GROUP_CH = D_MODEL // FOURIER_GROUPS
FFT_RADIX2 = 16
ROPE_THETA = 10000.0
N_GROUPS = 4
EXPERTS_PER_GROUP = 8
N_EXPERTS = N_GROUPS * EXPERTS_PER_GROUP
EXPERT_HIDDEN = D_MODEL // 2
LN_EPS = 1e-5
GN_EPS = 1e-6
ALPHA = (2.0 * DEPTH) ** 0.25
IN_EVEN = 2 * RET_WIDTH + 2 * RET_WIDTH + WIN_WIDTH + 2 * WIN_KV

LANES = 128
TOKEN_ROWS = D_MODEL // LANES
VMEM_LIMIT_BYTES = 48 * 1024 * 1024
NEG_MASK = -1e30
LOG2E = math.log2(math.e)
ROUTE_ROWS = 8
ROW_DMA_UNROLL = 8

_NT = (((1,), (1,)), ((), ()))


def _params(*sem):
    return pltpu.CompilerParams(dimension_semantics=sem, vmem_limit_bytes=VMEM_LIMIT_BYTES)


def _silu(v):
    return v / (1.0 + jnp.exp(-v))


def _layer_norm(z, gain, bias):
    mu = jnp.mean(z, axis=-1, keepdims=True)
    zc = z - mu
    var = jnp.mean(zc * zc, axis=-1, keepdims=True)
    return zc * lax.rsqrt(var + LN_EPS) * gain + bias


def _store_token_tiles(ref, v):
    n = v.shape[0]
    for c in range(TOKEN_ROWS):
        ref[pl.ds(c, n, stride=TOKEN_ROWS), :] = v[:, c * LANES:(c + 1) * LANES]


def _load_token_tiles(ref):
    n = ref.shape[0] // TOKEN_ROWS
    return jnp.concatenate([ref[pl.ds(c, n, stride=TOKEN_ROWS), :] for c in range(TOKEN_ROWS)], axis=1)


def _rope128(h, cos, sin_signed):
    return h * cos + pltpu.roll(h, 64, 1) * sin_signed


def _rope64x2(h, cos, sin_lo, sin_hi):
    return h * cos + pltpu.roll(h, 96, 1) * sin_lo + pltpu.roll(h, 32, 1) * sin_hi


def _inproj_kernel(x_ref, w_ref, cr_ref, sr_ref, cw_ref, slo_ref, shi_ref,
                   qa_ref, ka_ref, va_ref, ga_ref, qb_ref, kb_ref, vb_ref):
    xb = x_ref[...].astype(BF16)

    def seg(lo, hi):
        return jnp.dot(xb, w_ref[:, lo:hi], preferred_element_type=F32)

    cr, sr = cr_ref[...], sr_ref[...]
    q = seg(0, RET_WIDTH)
    k = seg(RET_WIDTH, 2 * RET_WIDTH)
    for h in range(RET_HEADS):
        sl = slice(LANES * h, LANES * (h + 1))
        qa_ref[:, sl] = _rope128(q[:, sl], cr, sr).astype(BF16)
        ka_ref[sl, :] = (_rope128(k[:, sl], cr, sr) * RET_DIM ** -0.5).T.astype(BF16)
    va_ref[...] = seg(2 * RET_WIDTH, 3 * RET_WIDTH).astype(BF16)
    ga_ref[...] = _silu(seg(3 * RET_WIDTH, 4 * RET_WIDTH)).astype(BF16)
    cw, slo, shi = cw_ref[...], slo_ref[...], shi_ref[...]
    base = 4 * RET_WIDTH
    q = seg(base, base + WIN_WIDTH)
    for p in range(WIN_WIDTH // LANES):
        sl = slice(LANES * p, LANES * (p + 1))
        qb_ref[:, sl] = (_rope64x2(q[:, sl], cw, slo, shi) * (WIN_DIM ** -0.5 * LOG2E)).astype(BF16)
    kv = seg(base + WIN_WIDTH, base + WIN_WIDTH + 2 * WIN_KV)
    kb = _rope64x2(kv[:, :WIN_KV], cw, slo, shi)
    vb = kv[:, WIN_KV:]
    kb_ref[:, :WIN_KV] = kb.astype(BF16)
    kb_ref[:, WIN_KV:] = pltpu.roll(kb, WIN_DIM, 1).astype(BF16)
    vb_ref[:, :WIN_KV] = vb.astype(BF16)
    vb_ref[:, WIN_KV:] = pltpu.roll(vb, WIN_DIM, 1).astype(BF16)


def _rope_tables(seq):
    pos = jnp.arange(seq, dtype=F32)[:, None]
    half = RET_DIM // 2
    inv = ROPE_THETA ** (-jnp.arange(half, dtype=F32) / half)
    ang = pos * inv[None, :]
    cr = jnp.concatenate([jnp.cos(ang), jnp.cos(ang)], axis=1)
    sr = jnp.concatenate([-jnp.sin(ang), jnp.sin(ang)], axis=1)
    half = WIN_DIM // 2
    inv = ROPE_THETA ** (-jnp.arange(half, dtype=F32) / half)
    ang = pos * inv[None, :]
    c, s, z = jnp.cos(ang), jnp.sin(ang), jnp.zeros_like(ang)
    cw = jnp.concatenate([c, c, c, c], axis=1)
    slo = jnp.concatenate([-s, z, -s, z], axis=1)
    shi = jnp.concatenate([z, s, z, s], axis=1)
    return cr, sr, cw, slo, shi


def _inproj(x2, w_bf16, tables, seq, tm):
    T = x2.shape[0]
    nseq = seq // tm
    row = lambda i: (i, 0)
    tab = lambda i: (i % nseq, 0)
    widths = (RET_WIDTH, None, RET_WIDTH, RET_WIDTH, WIN_WIDTH, 2 * WIN_KV, 2 * WIN_KV)
    return pl.pallas_call(
        _inproj_kernel,
        grid=(T // tm,),
        in_specs=[pl.BlockSpec((tm, D_MODEL), row),
                  pl.BlockSpec((D_MODEL, IN_EVEN), lambda i: (0, 0))]
                 + [pl.BlockSpec((tm, LANES), tab)] * 5,
        out_specs=[pl.BlockSpec((tm, w), row) if w else pl.BlockSpec((RET_WIDTH, tm), lambda i: (0, i))
                   for w in widths],
        out_shape=[jax.ShapeDtypeStruct((T, w) if w else (RET_WIDTH, T), BF16) for w in widths],
        compiler_params=_params("arbitrary"),
        name="inproj_rope",
    )(x2, w_bf16, *tables)


def _retention_kernel(logit_ref, q_ref, kt_ref, v_ref, g_ref, gain_ref, o_ref,
                      dmat_scr, kv_scr, state_scr):
    h = pl.program_id(1)
    C = RET_BLOCK
    dk = RET_DIM
    n_chunks = q_ref.shape[1] // C

    def log_gamma(d):
        v = jnp.full((1, 1), logit_ref[d, h], F32)
        return -(jnp.maximum(-v, 0.0) + jnp.log(1.0 + jnp.exp(-jnp.abs(v))))

    lgf, lgb = log_gamma(0), log_gamma(1)
    i = lax.broadcasted_iota(I32, (C, C), 0)
    j = lax.broadcasted_iota(I32, (C, C), 1)
    diff = (i - j).astype(F32)
    dmat_scr[...] = jnp.where(diff >= 0, jnp.exp(lgf * jnp.maximum(diff, 0.0)),
                              jnp.exp(lgb * jnp.maximum(-diff, 0.0)))
    col = lax.broadcasted_iota(I32, (C, 1), 0).astype(F32)
    lane = lax.broadcasted_iota(I32, (1, C), 1).astype(F32)
    xi_f, xi_b = jnp.exp(lgf * (col + 1.0)), jnp.exp(lgb * (C - col))
    zeta_f, zeta_b = jnp.exp(lgf * (C - 1.0 - lane)), jnp.exp(lgb * lane)
    dec_f, dec_b = jnp.exp(lgf * C), jnp.exp(lgb * C)
    gain = gain_ref[...]

    def span(n):
        return pl.ds(pl.multiple_of(n * C, C), C)

    def kv_pass(n, carry):
        kt = kt_ref[:, span(n)].astype(F32)
        lhs = jnp.concatenate([(kt * zeta_f).astype(BF16), (kt * zeta_b).astype(BF16)], axis=0)
        kv_scr[n] = jnp.dot(lhs, v_ref[0, span(n), :], preferred_element_type=F32)
        return carry

    lax.fori_loop(0, n_chunks, kv_pass, 0, unroll=8)

    def scan_f(n, state):
        state_scr[n, :dk, :] = state.astype(BF16)
        return state * dec_f + kv_scr[n, :dk, :]

    def scan_b(t, state):
        n = n_chunks - 1 - t
        state_scr[n, dk:, :] = state.astype(BF16)
        return state * dec_b + kv_scr[n, dk:, :]

    zero = jnp.zeros((dk, dk), F32)
    lax.fori_loop(0, n_chunks, scan_f, zero)
    lax.fori_loop(0, n_chunks, scan_b, zero)

    def out_pass(n, carry):
        q = q_ref[0, span(n), :]
        s = jnp.dot(q, kt_ref[:, span(n)], preferred_element_type=F32)
        p = (s * dmat_scr[...]).astype(BF16)
        qf = q.astype(F32)
        qx = jnp.concatenate([(qf * xi_f).astype(BF16), (qf * xi_b).astype(BF16)], axis=1)
        y = (jnp.dot(p, v_ref[0, span(n), :], preferred_element_type=F32)
             + jnp.dot(qx, state_scr[n], preferred_element_type=F32))
        mu = jnp.mean(y, axis=-1, keepdims=True)
        yc = y - mu
        var = jnp.mean(yc * yc, axis=-1, keepdims=True)
        yn = yc * lax.rsqrt(var + GN_EPS) * gain
        o_ref[0, span(n), :] = (g_ref[0, span(n), :].astype(F32) * yn).astype(BF16)
        return carry

    lax.fori_loop(0, n_chunks, out_pass, 0, unroll=4)


def _retention(qa, ka_t, va, ga, decay_logit, gn_gain):
    B, S, _ = qa.shape
    n_chunks = S // RET_BLOCK
    head = pl.BlockSpec((1, S, RET_DIM), lambda b, h: (b, 0, h))
    return pl.pallas_call(
        _retention_kernel,
        grid=(B, RET_HEADS),
        in_specs=[pl.BlockSpec(memory_space=pltpu.SMEM), head,
                  pl.BlockSpec((RET_DIM, S), lambda b, h: (h, b)), head, head,
                  pl.BlockSpec((1, RET_DIM), lambda b, h: (0, h))],
        out_specs=head,
        out_shape=jax.ShapeDtypeStruct((B, S, RET_WIDTH), BF16),
        scratch_shapes=[pltpu.VMEM((RET_BLOCK, RET_BLOCK), F32),
                        pltpu.VMEM((n_chunks, 2 * RET_DIM, RET_DIM), F32),
                        pltpu.VMEM((n_chunks, 2 * RET_DIM, RET_DIM), BF16)],
        compiler_params=_params("arbitrary", "arbitrary"),
        name="retention_gn_gate",
    )(decay_logit.astype(F32), qa, ka_t, va, ga, gn_gain.reshape(1, RET_WIDTH).astype(F32))


def _winattn_kernel(sink_ref, q_ref, kp_ref, kc_ref, kn_ref, vp_ref, vc_ref, vn_ref, o_ref):
    step = pl.program_id(1)
    n_steps = pl.num_programs(1)
    W = WINDOW
    nq = q_ref.shape[1] // W
    group = WIN_Q_HEADS // WIN_KV_HEADS
    k_all = jnp.concatenate([kp_ref[0], kc_ref[0], kn_ref[0]], axis=0)
    v_all = jnp.concatenate([vp_ref[0], vc_ref[0], vn_ref[0]], axis=0)
    lo_half = lax.broadcasted_iota(I32, (1, LANES), 1) < WIN_DIM

    def placements(slab, g):
        first, second = slab[:, :LANES], slab[:, LANES:]
        zero = jnp.zeros_like(first)
        if g == 0:
            return jnp.where(lo_half, first, zero), jnp.where(lo_half, zero, second)
        return jnp.where(lo_half, second, zero), jnp.where(lo_half, zero, first)

    rows2 = lax.broadcasted_iota(I32, (2 * W, 1), 0)
    qi = lax.broadcasted_iota(I32, (2 * W, 3 * W), 0) & (W - 1)
    cj = lax.broadcasted_iota(I32, (2 * W, 3 * W), 1)
    rel = cj - qi
    band = (rel >= 0) & (rel <= 2 * W)

    for g in range(WIN_KV_HEADS):
        k_even, k_odd = placements(k_all, g)
        v_even, v_odd = placements(v_all, g)
        c0 = group * WIN_DIM * g
        sink_even = jnp.where(rows2 < W, sink_ref[0, group * g], sink_ref[0, group * g + 2])
        sink_odd = jnp.where(rows2 < W, sink_ref[0, group * g + 1], sink_ref[0, group * g + 3])
        for jq in range(nq):
            qrows = slice(jq * W, (jq + 1) * W)
            krows = slice(jq * W, (jq + 3) * W)
            mask = band
            if jq == 0:
                mask = mask & (cj >= jnp.where(step == 0, W, 0))
            if jq == nq - 1:
                mask = mask & (cj < jnp.where(step == n_steps - 1, 2 * W, 3 * W))
            q2 = jnp.concatenate([q_ref[0, qrows, c0:c0 + LANES],
                                  q_ref[0, qrows, c0 + LANES:c0 + 2 * LANES]], axis=0)

            def softmax_parts(k_placed, sink):
                s = lax.dot_general(q2, k_placed[krows], _NT, preferred_element_type=F32)
                s = jnp.concatenate([jnp.where(mask[:, :W], s[:, :W], NEG_MASK), s[:, W:2 * W],
                                     jnp.where(mask[:, 2 * W:], s[:, 2 * W:], NEG_MASK)], axis=1)
                m = jnp.maximum(jnp.max(s, axis=1, keepdims=True), sink)
                e = jnp.exp2(s - m)
                return e.astype(BF16), jnp.sum(e, axis=1, keepdims=True) + jnp.exp2(sink - m)

            p_even, den_even = softmax_parts(k_even, sink_even)
            p_odd, den_odd = softmax_parts(k_odd, sink_odd)
            o = (jnp.dot(p_even, v_even[krows], preferred_element_type=F32)
                 + jnp.dot(p_odd, v_odd[krows], preferred_element_type=F32))
            o = (o / jnp.where(lo_half, den_even, den_odd)).astype(BF16)
            o_ref[0, qrows, c0:c0 + LANES] = o[:W]
            o_ref[0, qrows, c0 + LANES:c0 + 2 * LANES] = o[W:]


def _winattn(qb, kb2, vb2, sink_logit, blocks_per_step):
    B, S, _ = qb.shape
    nq = blocks_per_step
    nb = S // WINDOW
    prev = pl.BlockSpec((1, WINDOW, 2 * WIN_KV), lambda b, s: (b, jnp.maximum(s * nq - 1, 0), 0))
    cur = pl.BlockSpec((1, nq * WINDOW, 2 * WIN_KV), lambda b, s: (b, s, 0))
    nxt = pl.BlockSpec((1, WINDOW, 2 * WIN_KV), lambda b, s: (b, jnp.minimum((s + 1) * nq, nb - 1), 0))
    qspec = pl.BlockSpec((1, nq * WINDOW, WIN_WIDTH), lambda b, s: (b, s, 0))
    return pl.pallas_call(
        _winattn_kernel,
        grid=(B, nb // nq),
        in_specs=[pl.BlockSpec(memory_space=pltpu.SMEM), qspec, prev, cur, nxt, prev, cur, nxt],
        out_specs=qspec,
        out_shape=jax.ShapeDtypeStruct((B, S, WIN_WIDTH), BF16),
        compiler_params=_params("arbitrary", "arbitrary"),
        name="window_attention",
    )(sink_logit.reshape(1, WIN_Q_HEADS).astype(F32) * LOG2E, qb, kb2, kb2, kb2, vb2, vb2, vb2)


def _chan_dft_kernel(xa_ref, xb_ref, m_ref, wr_ref, wi_ref):
    n1 = xa_ref.shape[1] // FFT_RADIX2
    for s in range(FFT_RADIX2):
        rows = pl.ds(s, n1, stride=FFT_RADIX2)
        w = (jnp.dot(xa_ref[0, rows, :].astype(BF16), m_ref[:LANES, :], preferred_element_type=F32)
             + jnp.dot(xb_ref[0, rows, :].astype(BF16), m_ref[LANES:, :], preferred_element_type=F32))
        wr_ref[0, s] = w[:, :GROUP_CH].astype(BF16)
        wi_ref[0, s] = w[:, GROUP_CH:].astype(BF16)


def _real_dft16(ar, ai):
    n = FFT_RADIX2
    cs = [math.cos(2 * math.pi * m / n) for m in range(n)]
    sn = [math.sin(2 * math.pi * m / n) for m in range(n)]

    def axpy(acc, coef, v):
        if abs(coef) < 1e-12:
            return acc
        if abs(abs(coef) - 1.0) < 1e-12:
            if acc is None:
                return v if coef > 0 else -v
            return acc + v if coef > 0 else acc - v
        t = coef * v
        return t if acc is None else acc + t

    p = {s: ar[s] + ar[n - s] for s in range(1, n // 2)}
    d = {s: ai[s] - ai[n - s] for s in range(1, n // 2)}
    base = (ar[0] + ar[n // 2], ar[0] - ar[n // 2])
    y = [None] * n
    for k in range(n // 2 + 1):
        e = base[k % 2]
        for s in range(1, n // 2):
            e = axpy(e, cs[(s * k) % n], p[s])
        if k in (0, n // 2):
            y[k] = e
            continue
        o = None
        for s in range(1, n // 2):
            o = axpy(o, sn[(s * k) % n], d[s])
        y[k] = e + o
        y[n - k] = e - o
    return y


def _seq_dft_kernel(wr_ref, wi_ref, cf_ref, sf_ref, y_ref, ar_scr, ai_scr):
    n1 = wr_ref.shape[2]
    for s in range(FFT_RADIX2):
        cf, sf, wr, wi = cf_ref[s], sf_ref[s], wr_ref[0, s], wi_ref[0, s]
        ar_scr[s] = (jnp.dot(cf, wr, preferred_element_type=F32)
                     + jnp.dot(sf, wi, preferred_element_type=F32))
        ai_scr[s] = (jnp.dot(cf, wi, preferred_element_type=F32)
                     - jnp.dot(sf, wr, preferred_element_type=F32))

    def rows8(r, carry):
        rows = pl.ds(pl.multiple_of(r * 8, 8), 8)
        y = _real_dft16([ar_scr[s, rows, :] for s in range(FFT_RADIX2)],
                        [ai_scr[s, rows, :] for s in range(FFT_RADIX2)])
        for k in range(FFT_RADIX2):
            y_ref[0, k, rows, :] = y[k].astype(BF16)
        return carry

    lax.fori_loop(0, n1 // 8, rows8, 0, unroll=2)


def _fourier_tables(seq):
    n1 = seq // FFT_RADIX2
    c = jnp.arange(GROUP_CH, dtype=I32)
    m = (c[:, None] * c[None, :]) % GROUP_CH
    ang = m.astype(F32) * (2.0 * math.pi / GROUP_CH)
    scale = (seq * GROUP_CH) ** -0.5
    chan = (jnp.concatenate([jnp.cos(ang), -jnp.sin(ang)], axis=1) * scale).astype(BF16)
    k1 = jnp.arange(n1, dtype=I32)[None, :, None]
    s1 = jnp.arange(n1, dtype=I32)[None, None, :]
    s2 = jnp.arange(FFT_RADIX2, dtype=I32)[:, None, None]
    m = (FFT_RADIX2 * s1 * k1 + s2 * k1) % seq
    ang = m.astype(F32) * (2.0 * math.pi / seq)
    return chan, jnp.cos(ang).astype(BF16), jnp.sin(ang).astype(BF16)


def _fourier(x, tables, tn=256):
    B, S, _ = x.shape
    n1 = S // FFT_RADIX2
    chan, cf, sf = tables
    wshape = jax.ShapeDtypeStruct((B, FFT_RADIX2, n1, D_MODEL), BF16)
    wblk = pl.BlockSpec((1, FFT_RADIX2, n1, GROUP_CH), lambda b, g: (b, 0, 0, g))
    wr, wi = pl.pallas_call(
        _chan_dft_kernel,
        grid=(B, FOURIER_GROUPS),
        in_specs=[pl.BlockSpec((1, S, LANES), lambda b, g: (b, 0, 2 * g)),
                  pl.BlockSpec((1, S, LANES), lambda b, g: (b, 0, 2 * g + 1)),
                  pl.BlockSpec((GROUP_CH, 2 * GROUP_CH), lambda b, g: (0, 0))],
        out_specs=[wblk, wblk],
        out_shape=[wshape, wshape],
        compiler_params=_params("arbitrary", "arbitrary"),
        name="fourier_channel_dft",
    )(x, x, chan)
    cols = pl.BlockSpec((1, FFT_RADIX2, n1, tn), lambda b, j: (b, 0, 0, j))
    full = pl.BlockSpec((FFT_RADIX2, n1, n1), lambda b, j: (0, 0, 0))
    y = pl.pallas_call(
        _seq_dft_kernel,
        grid=(B, D_MODEL // tn),
        in_specs=[cols, cols, full, full],
        out_specs=cols,
        out_shape=wshape,
        scratch_shapes=[pltpu.VMEM((FFT_RADIX2, n1, tn), F32), pltpu.VMEM((FFT_RADIX2, n1, tn), F32)],
        compiler_params=_params("arbitrary", "arbitrary"),
        name="fourier_sequence_dft",
    )(wr, wi, cf, sf)
    return y.reshape(B * S, D_MODEL)


def _proj_ln_route_kernel(a_ref, b_ref, x_ref, w_ref, gain_ref, bias_ref, wr_ref, br_ref,
                          xt_ref, ri_ref, rg_ref, cnt_ref, carry_scr):
    tm = x_ref.shape[0]
    half = a_ref.shape[1]

    @pl.when(pl.program_id(0) == 0)
    def _():
        carry_scr[...] = jnp.zeros_like(carry_scr)

    mix = (jnp.dot(a_ref[...], w_ref[:half, :], preferred_element_type=F32)
           + jnp.dot(b_ref[...], w_ref[half:, :], preferred_element_type=F32))
    y = _layer_norm(ALPHA * x_ref[...] + mix, gain_ref[...], bias_ref[...])
    _store_token_tiles(xt_ref, y)

    logits = jnp.dot(y.astype(BF16), wr_ref[...], preferred_element_type=F32) + br_ref[...]
    lane = lax.broadcasted_iota(I32, (tm, LANES), 1)
    lanef = lane.astype(F32)
    ninf = -jnp.inf
    big = 1e9

    def rmax(v):
        return jnp.max(v, axis=1, keepdims=True)

    def first_lane(hit):
        return jnp.min(jnp.where(hit, lanef, big), axis=1, keepdims=True)

    cmask = (lane >= N_EXPERTS) & (lane < N_EXPERTS + N_GROUPS)
    cl = jnp.where(cmask, logits, ninf)
    cmax = rmax(cl)
    group = first_lane(cl == cmax) - float(N_EXPERTS)
    p_group = 1.0 / jnp.sum(jnp.where(cmask, jnp.exp(cl - cmax), 0.0), axis=1, keepdims=True)
    lo = group * float(EXPERTS_PER_GROUP)
    fmask = (lanef >= lo) & (lanef < lo + float(EXPERTS_PER_GROUP))
    fl = jnp.where(fmask, logits, ninf)
    v1 = rmax(fl)
    e1 = first_lane(fl == v1)
    fl2 = jnp.where(lanef == e1, ninf, fl)
    v2 = rmax(fl2)
    e2 = first_lane(fl2 == v2)
    t = jnp.exp(v2 - v1)
    g1 = p_group / (1.0 + t)
    g2 = p_group * t / (1.0 + t)

    r = lax.broadcasted_iota(I32, (tm, tm), 0)
    cidx = lax.broadcasted_iota(I32, (tm, tm), 1)
    lower = jnp.where(r > cidx, 1.0, 0.0).astype(BF16)
    ranks = []
    for slot, chosen in enumerate((e1, e2)):
        hit = lanef == chosen
        onehot = jnp.where(hit, 1.0, 0.0)
        carry = carry_scr[slot:slot + 1, :]
        prefix = jnp.dot(lower, onehot.astype(BF16), preferred_element_type=F32) + carry
        ranks.append(jnp.sum(jnp.where(hit, prefix, 0.0), axis=1, keepdims=True))
        carry_scr[slot:slot + 1, :] = carry + jnp.sum(onehot, axis=0, keepdims=True)
    cnt_ref[...] = carry_scr[...]
    record = jnp.where(lane == 0, e1, jnp.where(lane == 1, e2, jnp.where(lane == 2, ranks[0], ranks[1])))
    ri_ref[...] = record.T[:ROUTE_ROWS, :].astype(I32)
    rg_ref[...] = jnp.where(lane == 0, g1, g2)


def _proj_ln_route(a, b, x2, w_bf16, gain, bias, w_router, b_router, tm):
    T = x2.shape[0]
    half = D_MODEL // 2
    row = lambda i: (i, 0)
    const = lambda i: (0, 0)
    a_spec = pl.BlockSpec((tm, half), row)
    b_spec = pl.BlockSpec((tm, half), row if b is not a else (lambda i: (i, 1)))
    return pl.pallas_call(
        _proj_ln_route_kernel,
        grid=(T // tm,),
        in_specs=[a_spec, b_spec, pl.BlockSpec((tm, D_MODEL), row),
                  pl.BlockSpec((D_MODEL, D_MODEL), const),
                  pl.BlockSpec((1, D_MODEL), const), pl.BlockSpec((1, D_MODEL), const),
                  pl.BlockSpec((D_MODEL, LANES), const), pl.BlockSpec((1, LANES), const)],
        out_specs=[pl.BlockSpec((tm * TOKEN_ROWS, LANES), row), pl.BlockSpec((ROUTE_ROWS, tm), lambda i: (0, i)),
                   pl.BlockSpec((tm, LANES), row), pl.BlockSpec((2, LANES), const)],
        out_shape=[jax.ShapeDtypeStruct((T * TOKEN_ROWS, LANES), F32), jax.ShapeDtypeStruct((ROUTE_ROWS, T), I32),
                   jax.ShapeDtypeStruct((T, LANES), F32), jax.ShapeDtypeStruct((2, LANES), F32)],
        scratch_shapes=[pltpu.VMEM((2, LANES), F32)],
        compiler_params=_params("arbitrary"),
        name="proj_ln_router",
    )(a, b, x2, w_bf16, gain, bias, w_router, b_router)


def _expert_kernel(te_ref, base_ref, nv_ref, src_ref, x_hbm, wg_ref, wu_ref, wd_ref, y_ref,
                   wg_scr, wu_scr, wd_scr, xbuf0, xbuf1, xbuf2, gsem):
    i = pl.program_id(0)
    nv = nv_ref[0]
    tile = xbuf0.shape[0] // TOKEN_ROWS
    xbufs = (xbuf0, xbuf1, xbuf2)
    n_buf = len(xbufs)

    def token_rows(first_row):
        return pl.ds(pl.multiple_of(first_row, TOKEN_ROWS), TOKEN_ROWS)

    def gather_row(base, r, q):
        return pltpu.make_async_copy(x_hbm.at[token_rows(src_ref[base + r]), :],
                                     xbufs[q].at[token_rows(r * TOKEN_ROWS), :], gsem.at[q])

    def wait_gather(q):
        pltpu.make_async_copy(x_hbm.at[pl.ds(0, tile * TOKEN_ROWS), :], xbufs[q], gsem.at[q]).wait()

    @pl.when(i == 0)
    def _():
        def first(r, carry):
            gather_row(base_ref[0], r, 0).start()
            gather_row(base_ref[1], r, 1).start()
            return carry

        lax.fori_loop(0, tile, first, 0, unroll=ROW_DMA_UNROLL)

    @pl.when(i < nv)
    def _():
        @pl.when((i == 0) | (te_ref[i] != te_ref[jnp.maximum(i - 1, 0)]))
        def _():
            wg_scr[...] = wg_ref[0, 0].astype(BF16)
            wu_scr[...] = wu_ref[0, 0].astype(BF16)
            wd_scr[...] = wd_ref[0, 0].astype(BF16)

        base_ahead = base_ref[i + 2]
        for q in range(n_buf):
            @pl.when(i % n_buf == q)
            def _():
                ahead = (q + 2) % n_buf
                wait_gather(q)
                for r in range(tile):
                    gather_row(base_ahead, r, ahead).start(priority=r % 2)
                xb = _load_token_tiles(xbufs[q]).astype(BF16)
                hg = jnp.dot(xb, wg_scr[...], preferred_element_type=F32)
                hu = jnp.dot(xb, wu_scr[...], preferred_element_type=F32)
                hid = (_silu(hg) * hu).astype(BF16)
                _store_token_tiles(y_ref, jnp.dot(hid, wd_scr[...], preferred_element_type=F32))

                @pl.when(i == nv - 1)
                def _():
                    wait_gather((q + 1) % n_buf)
                    wait_gather(ahead)

    @pl.when(i >= nv)
    def _():
        y_ref[...] = jnp.zeros_like(y_ref)


def _experts(x1t, src, tile_expert, tile_base, n_valid, w_gate, w_up, w_down, layer, tile):
    n_tiles = tile_expert.shape[0]
    wsel = lambda i, te, *_: (layer, te[i], 0, 0)
    buf = pltpu.VMEM((tile * TOKEN_ROWS, LANES), F32)
    return pl.pallas_call(
        _expert_kernel,
        grid_spec=pltpu.PrefetchScalarGridSpec(
            num_scalar_prefetch=4,
            grid=(n_tiles,),
            in_specs=[pl.BlockSpec(memory_space=pl.ANY),
                      pl.BlockSpec((1, 1, D_MODEL, EXPERT_HIDDEN), wsel),
                      pl.BlockSpec((1, 1, D_MODEL, EXPERT_HIDDEN), wsel),
                      pl.BlockSpec((1, 1, EXPERT_HIDDEN, D_MODEL), wsel)],
            out_specs=pl.BlockSpec((tile * TOKEN_ROWS, LANES), lambda i, *_: (i, 0)),
            scratch_shapes=[pltpu.VMEM((D_MODEL, EXPERT_HIDDEN), BF16),
                            pltpu.VMEM((D_MODEL, EXPERT_HIDDEN), BF16),
                            pltpu.VMEM((EXPERT_HIDDEN, D_MODEL), BF16),
                            buf, buf, buf, pltpu.SemaphoreType.DMA((3,))],
        ),
        out_shape=jax.ShapeDtypeStruct((n_tiles * tile * TOKEN_ROWS, LANES), F32),
        compiler_params=_params("arbitrary"),
        name="moe_experts",
    )(tile_expert, tile_base, n_valid, src, x1t, w_gate, w_up, w_down)


def _combine_ln_kernel(pos_cur, pos_nxt, pos_ahead, x_ref, g_ref, gain_ref, bias_ref, ys_hbm, o_ref,
                       buf0, buf1, buf2, sem):
    i = pl.program_id(0)
    n = pl.num_programs(0)
    tm = g_ref.shape[0]
    bufs = (buf0, buf1, buf2)
    n_buf = len(bufs)

    def gather_row(pos_ref, j, q):
        src = pl.ds(pl.multiple_of(pos_ref[0, 0, j], TOKEN_ROWS), TOKEN_ROWS)
        return pltpu.make_async_copy(ys_hbm.at[src, :], bufs[q].at[pl.ds(j * TOKEN_ROWS, TOKEN_ROWS), :],
                                     sem.at[q])

    def wait_gather(q):
        pltpu.make_async_copy(ys_hbm.at[pl.ds(0, 2 * tm * TOKEN_ROWS), :], bufs[q], sem.at[q]).wait()

    @pl.when(i == 0)
    def _():
        def first(j, carry):
            gather_row(pos_cur, j, 0).start()
            gather_row(pos_nxt, j, 1).start()
            return carry

        lax.fori_loop(0, 2 * tm, first, 0, unroll=ROW_DMA_UNROLL)

    for q in range(n_buf):
        @pl.when(i % n_buf == q)
        def _():
            ahead = (q + 2) % n_buf
            wait_gather(q)
            for j in range(2 * tm):
                gather_row(pos_ahead, j, ahead).start(priority=j % 2)
            g = g_ref[...]
            half = tm * TOKEN_ROWS
            ffn = (_load_token_tiles(bufs[q].at[pl.ds(0, half), :]) * g[:, 0:1]
                   + _load_token_tiles(bufs[q].at[pl.ds(half, half), :]) * g[:, 1:2])
            o_ref[...] = _layer_norm(ALPHA * _load_token_tiles(x_ref) + ffn, gain_ref[...], bias_ref[...])

            @pl.when(i == n - 1)
            def _():
                wait_gather((q + 1) % n_buf)
                wait_gather(ahead)


def _combine_ln(x1t, ys, pos, gates, gain, bias, tm):
    T = x1t.shape[0] // TOKEN_ROWS
    n = T // tm
    pos3 = jnp.stack([pos[0].reshape(n, tm), pos[1].reshape(n, tm)], axis=1).reshape(n, 1, 2 * tm)
    row = lambda i: (i, 0)
    const = lambda i: (0, 0)
    smem = lambda k: pl.BlockSpec((1, 1, 2 * tm), lambda i: (jnp.minimum(i + k, n - 1), 0, 0),
                                  memory_space=pltpu.SMEM)
    buf = pltpu.VMEM((2 * tm * TOKEN_ROWS, LANES), F32)
    return pl.pallas_call(
        _combine_ln_kernel,
        grid=(n,),
        in_specs=[smem(0), smem(1), smem(2),
                  pl.BlockSpec((tm * TOKEN_ROWS, LANES), row), pl.BlockSpec((tm, LANES), row),
                  pl.BlockSpec((1, D_MODEL), const), pl.BlockSpec((1, D_MODEL), const),
                  pl.BlockSpec(memory_space=pl.ANY)],
        out_specs=pl.BlockSpec((tm, D_MODEL), row),
        out_shape=jax.ShapeDtypeStruct((T, D_MODEL), F32),
        scratch_shapes=[buf, buf, buf, pltpu.SemaphoreType.DMA((3,))],
        compiler_params=_params("arbitrary"),
        name="moe_combine_ln",
    )(pos3, pos3, pos3, x1t, gates, gain, bias, ys)


def _moe(x1t, route_t, route_g, counts_f, w_gate, w_up, w_down, layer, gain, bias, tile, tm):
    T = x1t.shape[0] // TOKEN_ROWS
    n_pairs = 2 * T
    slot_counts = counts_f[:, :N_EXPERTS].astype(I32)
    counts = slot_counts[0] + slot_counts[1]
    eids = jnp.arange(N_EXPERTS, dtype=I32)
    expert, rank = route_t[0:2], route_t[2:4]
    key = expert.reshape(-1) * n_pairs + jnp.arange(n_pairs, dtype=I32)
    pair = jnp.sort(key) % n_pairs
    src = jnp.concatenate([pair % T, jnp.zeros((tile,), I32)]) * TOKEN_ROWS
    tiles_per = (counts + tile - 1) // tile
    tile_ends = jnp.cumsum(tiles_per)
    first_tile = tile_ends - tiles_per
    first_row = jnp.cumsum(counts) - counts
    n_tiles = n_pairs // tile + N_EXPERTS
    tile_ids = jnp.arange(n_tiles + 2, dtype=I32)
    owner = jnp.sum((tile_ids[:, None] >= tile_ends[None, :]).astype(I32), axis=1)
    valid = owner < N_EXPERTS
    last_expert = jnp.max(jnp.where(counts > 0, eids, 0))
    tile_expert = jnp.minimum(owner, last_expert).astype(I32)
    onehot = owner[:, None] == eids[None, :]
    pick = lambda v: jnp.sum(jnp.where(onehot, v[None, :], 0), axis=1)
    tile_base = pick(first_row) + (tile_ids - pick(first_tile)) * tile
    tile_base = jnp.where(valid, tile_base, n_pairs).astype(I32)
    n_valid = tile_ends[-1:].astype(I32)
    ys = _experts(x1t, src, tile_expert[:n_tiles], tile_base, n_valid, w_gate, w_up, w_down, layer, tile)
    start = jnp.stack([first_tile * tile, first_tile * tile + slot_counts[0]])
    offset = sum(jnp.where(expert == e, start[:, e:e + 1], 0) for e in range(N_EXPERTS))
    return _combine_ln(x1t, ys, ((offset + rank) * TOKEN_ROWS).astype(I32), route_g, gain, bias, tm)


def _pick(n, pref):
    t = min(n, pref)
    while n % t:
        t //= 2
    return t


def kernel(x, w_in_even, ret_decay_logit, ret_gn_gain, sink_logit, w_out_even, w_out_fourier,
           ln1_gain, ln1_bias, ln2_gain, ln2_bias, router_coarse_w, router_coarse_b,
           router_fine_w, router_fine_b, expert_w_gate, expert_w_up, expert_w_down):
    B, S, D = x.shape
    assert D == D_MODEL and S % (FFT_RADIX2 * 8) == 0 and S % WINDOW == 0 and S % RET_BLOCK == 0
    T = B * S
    tm = _pick(S, 512)
    tile = _pick(T, 512)
    rope_tabs = _rope_tables(S)
    fourier_tabs = _fourier_tables(S)
    row = lambda v: v.reshape(1, -1).astype(F32)

    x2 = x.reshape(T, D).astype(F32)
    for layer in range(DEPTH):
        if layer % 2 == 0:
            e = layer // 2
            qa, ka_t, va, ga, qb, kb2, vb2 = _inproj(x2, w_in_even[e].astype(BF16), rope_tabs, S, tm)
            shp = lambda v: v.reshape(B, S, v.shape[-1])
            ya = _retention(shp(qa), ka_t, shp(va), shp(ga), ret_decay_logit[e], ret_gn_gain[e])
            yb = _winattn(shp(qb), shp(kb2), shp(vb2), sink_logit[e], _pick(S // WINDOW, 8))
            a, b = ya.reshape(T, RET_WIDTH), yb.reshape(T, WIN_WIDTH)
            w_out = w_out_even[e]
        else:
            a = b = _fourier(x2.reshape(B, S, D), fourier_tabs)
            w_out = w_out_fourier[layer // 2]
        w_router = jnp.zeros((D, LANES), F32)
        w_router = w_router.at[:, :N_EXPERTS].set(router_fine_w[layer])
        w_router = w_router.at[:, N_EXPERTS:N_EXPERTS + N_GROUPS].set(router_coarse_w[layer])
        b_router = jnp.zeros((1, LANES), F32)
        b_router = b_router.at[0, :N_EXPERTS].set(router_fine_b[layer])
        b_router = b_router.at[0, N_EXPERTS:N_EXPERTS + N_GROUPS].set(router_coarse_b[layer])
        x1t, route_t, route_g, counts = _proj_ln_route(
            a, b, x2, w_out.astype(BF16), row(ln1_gain[layer]), row(ln1_bias[layer]),
            w_router.astype(BF16), b_router, tm)
        x2 = _moe(x1t, route_t, route_g, counts, expert_w_gate, expert_w_up, expert_w_down, layer,
                  row(ln2_gain[layer]), row(ln2_bias[layer]), tile, tm)
    return x2.reshape(B, S, D).astype(x.dtype)
```
